```python
import math
import jax, jax.numpy as jnp
from jax import lax
import numpy as np

D_MODEL = 2048
BATCH = 2
SEQ = 4096
DEPTH = 4
DEC_BATCH = 8
DEC_SEQ = 1
PAST_LEN = 16384
PAGE_SIZE = 128

N_MIXERS = 3
A_WINDOWS = (128, 512, 2048)
A_DILATIONS = (1, 4, 16)
A_GROUPS = 3
A_HEADS_PER_GROUP = 8
A_HEAD_DIM = 128
A_HEADS = A_GROUPS * A_HEADS_PER_GROUP
A_BLOCK = 128
N_BUCKETS = 32
MAX_DISTANCE = 2048
B_HEADS = 16
B_HEAD_DIM = D_MODEL // B_HEADS
C_HEADS = 4
C_KEY_DIM = D_MODEL // 2 // C_HEADS
C_VAL_DIM = D_MODEL // C_HEADS
C_GATE_RANK = 16
C_GATE_NORMALIZER = 16.0
CHUNK = 64
MEM_TOKENS = 256
MEM_HEADS = 4
MEM_HEAD_DIM = 128
D_FF = 4 * D_MODEL
EPS = 1e-6

kernel_name = "hybrid_dilated_hgrn2_gla_decoder_step"


def n_layers_of(kind):
    return len(range(kind, DEPTH, N_MIXERS))


def rmsnorm(x, g):
    xf = x.astype(jnp.float32)
    r = lax.rsqrt(jnp.mean(xf * xf, axis=-1, keepdims=True) + EPS)
    return (xf * r).astype(x.dtype) * g.astype(x.dtype)


def t5_bucket(dist):
    dist = np.asarray(dist, np.int64)
    max_exact = N_BUCKETS // 2
    d = np.maximum(dist, 1).astype(np.float64)
    large = max_exact + (np.log(d / max_exact) / math.log(MAX_DISTANCE / max_exact)
                         * (N_BUCKETS - max_exact)).astype(np.int64)
    return np.where(dist < max_exact, dist, np.minimum(large, N_BUCKETS - 1)).astype(np.int32)


def masked_softmax_stats(s, mask):
    s = jnp.where(mask, s, -jnp.inf)
    mx = jnp.max(s, axis=-1, keepdims=True)
    p = jnp.exp(s - mx)
    l = jnp.sum(p, axis=-1, keepdims=True)
    return p / l, (mx + jnp.log(l))[..., 0]


def dilated_attn_prompt(q, k, v, table_g, d, span):
    B, T, H, hd = q.shape
    L = T // d
    nb = -(-L // A_BLOCK)
    Lp = nb * A_BLOCK

    def split(a, front):
        a = a.reshape(B, L, d, H, hd).transpose(0, 2, 1, 3, 4)
        return jnp.pad(a, ((0, 0), (0, 0), (front, Lp - L), (0, 0), (0, 0)))

    qb = split(q, 0).reshape(B, d, nb, A_BLOCK, H, hd)
    kb = split(k, A_BLOCK).reshape(B, d, nb + 1, A_BLOCK, H, hd)
    vb = split(v, A_BLOCK).reshape(B, d, nb + 1, A_BLOCK, H, hd)
    kk = jnp.concatenate([kb[:, :, :-1], kb[:, :, 1:]], axis=3)
    vv = jnp.concatenate([vb[:, :, :-1], vb[:, :, 1:]], axis=3)
    i = np.arange(A_BLOCK)[:, None]
    j = np.arange(2 * A_BLOCK)[None, :]
    steps = i + A_BLOCK - j
    band = (steps >= 0) & (steps <= span)
    key_pos = np.arange(nb)[:, None] * A_BLOCK + np.arange(2 * A_BLOCK)[None, :] - A_BLOCK
    mask = band[None] & (key_pos >= 0)[:, None, :]
    bucket = t5_bucket(np.clip(steps, 0, span) * d)
    bias = jnp.transpose(jnp.take(table_g, jnp.asarray(bucket), axis=0), (2, 0, 1)).astype(jnp.float32)
    s = jnp.einsum('brnqhd,brnkhd->brnhqk', qb, kk).astype(jnp.float32) * (hd ** -0.5) + bias
    p, lse = masked_softmax_stats(s, mask[:, None])
    o = jnp.einsum('brnhqk,brnkhd->brnqhd', p.astype(v.dtype), vv)
    o = o.reshape(B, d, Lp, H, hd)[:, :, :L].transpose(0, 2, 1, 3, 4).reshape(B, T, H, hd)
    lse = lse.transpose(0, 1, 2, 4, 3).reshape(B, d, Lp, H)[:, :, :L].transpose(0, 2, 1, 3).reshape(B, T, H)
    return o, lse


def dilated_attn_sample(q, k, v, buf, table_g, d, span):
    B, S, H, hd = q.shape
    Lw = buf.shape[1]
    cat = jnp.concatenate([buf, jnp.stack([k, v], axis=2)], axis=1)
    steps = np.arange(span + 1)
    idx = Lw + np.arange(S)[:, None] - steps[None, :] * d
    valid = idx >= 0
    g = cat[:, np.clip(idx, 0, None)]
    bias = jnp.take(table_g, jnp.asarray(t5_bucket(steps * d)), axis=0).T.astype(jnp.float32)
    s = jnp.einsum('bshd,bskhd->bhsk', q, g[:, :, :, 0]).astype(jnp.float32) * (hd ** -0.5) + bias[:, None, :]
    p, lse = masked_softmax_stats(s, valid)
    o = jnp.einsum('bhsk,bskhd->bshd', p.astype(v.dtype), g[:, :, :, 1])
    return o, jnp.transpose(lse, (0, 2, 1)), cat[:, S:]


def mixer_a(hn, w_in, w_out, table, bufs):
    B, T, _ = hn.shape
    qkv = (hn @ w_in).reshape(B, T, 3, A_GROUPS, A_HEADS_PER_GROUP, A_HEAD_DIM)
    outs, lses, new_bufs = [], [], []
    for g in range(A_GROUPS):
        q, k, v = qkv[:, :, 0, g], qkv[:, :, 1, g], qkv[:, :, 2, g]
        tg = table[:, g * A_HEADS_PER_GROUP:(g + 1) * A_HEADS_PER_GROUP]
        d = A_DILATIONS[g]
        span = A_WINDOWS[g] // d
        if bufs is None:
            o, lse = dilated_attn_prompt(q, k, v, tg, d, span)
            lw = min(A_WINDOWS[g], T)
            nbuf = jnp.stack([k[:, T - lw:], v[:, T - lw:]], axis=2)
        else:
            o, lse, nbuf = dilated_attn_sample(q, k, v, bufs[g], tg, d, span)
        outs.append(o)
        lses.append(lse)
        new_bufs.append(nbuf)
    w = jax.nn.softmax(jnp.stack(lses, axis=0), axis=0)
    o = jnp.einsum('gbth,gbthd->bthd', w, jnp.stack(outs, axis=0).astype(jnp.float32)).astype(hn.dtype)
    return o.reshape(B, T, A_HEADS_PER_GROUP * A_HEAD_DIM) @ w_out, new_bufs


def chunked_gated_linear(q, k, v, log_f, S0):
    B, T, H, _ = q.shape
    dv = v.shape[-1]
    C = min(CHUNK, T)
    n = -(-T // C)
    pad = n * C - T

    def prep(a):
        a = jnp.pad(a.astype(jnp.float32), ((0, 0), (0, pad), (0, 0), (0, 0)))
        return a.reshape(B, n, C, H, a.shape[-1]).transpose(1, 0, 2, 3, 4)

    tri = jnp.tril(jnp.ones((C, C), dtype=bool))[None, :, :, None, None]

    def step(S, inp):
        qc, kc, vc, gc = inp
        b = jnp.cumsum(gc, axis=1)
        o_inter = jnp.einsum('bthk,bhkv->bthv', qc * jnp.exp(b), S)
        rel = b[:, :, None] - b[:, None, :]
        decay = jnp.exp(jnp.where(tri, rel, -jnp.inf))
        A = jnp.einsum('bthk,btshk->bhts', qc, decay * kc[:, None])
        o_intra = jnp.einsum('bhts,bshv->bthv', A, vc)
        b_last = b[:, -1]
        S_new = jnp.exp(b_last)[..., None] * S + jnp.einsum('bshk,bshv->bhkv', kc * jnp.exp(b_last[:, None] - b), vc)
        return S_new, o_inter + o_intra

    S, o = lax.scan(step, S0.astype(jnp.float32), (prep(q), prep(k), prep(v), prep(log_f)))
    o = o.transpose(1, 0, 2, 3, 4).reshape(B, n * C, H, dv)[:, :T]
    return o.astype(v.dtype), S.astype(S0.dtype)


def mixer_b(hn, w_in, w_out, lb, gnorm, S0):
    B, T, _ = hn.shape
    shp = (B, T, B_HEADS, B_HEAD_DIM)
    q, fz, i, gate = jnp.split(hn @ w_in, 4, axis=-1)
    f = lb + (1.0 - lb) * jax.nn.sigmoid(fz.astype(jnp.float32))
    o, S = chunked_gated_linear(jax.nn.silu(q).reshape(shp), (1.0 - f).reshape(shp), i.reshape(shp),
                                jnp.log(f).reshape(shp), S0)
    o = rmsnorm(o, gnorm).reshape(B, T, D_MODEL) * jax.nn.silu(gate)
    return o @ w_out, S


def mixer_c(hn, w_in, w_gate, b_gate, w_out, gnorm, S0):
    B, T, _ = hn.shape
    dk, dv = C_HEADS * C_KEY_DIM, C_HEADS * C_VAL_DIM
    q, k, v, r, glow = jnp.split(hn @ w_in, [dk, 2 * dk, 2 * dk + dv, 2 * dk + 2 * dv], axis=-1)
    log_g = jax.nn.log_sigmoid((glow @ w_gate + b_gate).astype(jnp.float32)) / C_GATE_NORMALIZER
    kshp = (B, T, C_HEADS, C_KEY_DIM)
    o, S = chunked_gated_linear((q * (C_KEY_DIM ** -0.5)).reshape(kshp), k.reshape(kshp),
                                v.reshape(B, T, C_HEADS, C_VAL_DIM), log_g.reshape(kshp), S0)
    o = rmsnorm(o, gnorm).reshape(B, T, dv) * jax.nn.silu(r)
    return o @ w_out, S


def mem_attend(hn, w_q, w_o, kv):
    B, T, _ = hn.shape
    q = (hn @ w_q).reshape(B, T, MEM_HEADS, MEM_HEAD_DIM)
    s = jnp.einsum('bthd,bmhd->bhtm', q, kv[:, :, 0]).astype(jnp.float32) * (MEM_HEAD_DIM ** -0.5)
    p = jax.nn.softmax(s, axis=-1).astype(kv.dtype)
    o = jnp.einsum('bhtm,bmhd->bthd', p, kv[:, :, 1]).reshape(B, T, MEM_HEADS * MEM_HEAD_DIM)
    return o @ w_o


def squared_relu_mlp(hn, w_up, w_down):
    return jnp.square(jax.nn.relu(hn @ w_up)) @ w_down


def run_trunk(x, mem_kvs, a_bufs, b_states, c_states, rel_bias_table, a_w_in, a_w_out,
              b_w_in, b_w_out, b_lower_bound, b_gnorm, c_w_in, c_w_gate, c_b_gate, c_w_out, c_gnorm,
              mem_w_q, mem_w_out, norm_mix, norm_mem, norm_mlp, mlp_w_up, mlp_w_down, norm_final):
    prompt = a_bufs is None
    Bsz = x.shape[0]
    sm = jax.nn.softmax(b_lower_bound.astype(jnp.float32), axis=0)
    lower_bounds = jnp.cumsum(sm, axis=0) - sm
    new_a, new_b, new_c = [], [], []
    h = x
    for li in range(DEPTH):
        kind, j = li % N_MIXERS, li // N_MIXERS
        hn = rmsnorm(h, norm_mix[li])
        if kind == 0:
            bufs = None if prompt else [buf[j] for buf in a_bufs]
            out, nbufs = mixer_a(hn, a_w_in[j], a_w_out[j], rel_bias_table, bufs)
            new_a.append(nbufs)
        elif kind == 1:
            S0 = jnp.zeros((Bsz, B_HEADS, B_HEAD_DIM, B_HEAD_DIM), x.dtype) if prompt else b_states[j]
            out, S = mixer_b(hn, b_w_in[j], b_w_out[j], lower_bounds[li], b_gnorm[j], S0)
            new_b.append(S)
        else:
            S0 = jnp.zeros((Bsz, C_HEADS, C_KEY_DIM, C_VAL_DIM), x.dtype) if prompt else c_states[j]
            out, S = mixer_c(hn, c_w_in[j], c_w_gate[j], c_b_gate[j], c_w_out[j], c_gnorm[j], S0)
            new_c.append(S)
        h = h + out
        h = h + mem_attend(rmsnorm(h, norm_mem[li]), mem_w_q[li], mem_w_out[li], mem_kvs[li])
        h = h + squared_relu_mlp(rmsnorm(h, norm_mlp[li]), mlp_w_up[li], mlp_w_down[li])
    y = rmsnorm(h, norm_final)
    win = [jnp.stack([nb[g] for nb in new_a], axis=0) for g in range(A_GROUPS)]
    return y, win, jnp.stack(new_b, axis=0), jnp.stack(new_c, axis=0)


def setup_inputs(seed: int = 0) -> dict:
    key = jax.random.key(seed)
    keys = iter(jax.random.split(key, 48))

    def nrm(shape, scale=1.0):
        return jax.random.normal(next(keys), shape, jnp.float32) * scale

    def gain(shape):
        return 1.0 + 0.05 * nrm(shape)

    n_a, n_b, n_c = n_layers_of(0), n_layers_of(1), n_layers_of(2)
    a_width = A_HEADS_PER_GROUP * A_HEAD_DIM
    c_dk, c_dv = C_HEADS * C_KEY_DIM, C_HEADS * C_VAL_DIM
    m_width = MEM_HEADS * MEM_HEAD_DIM
    return {
        "x_prompt": nrm((BATCH, SEQ, D_MODEL)),
        "x_sample": nrm((DEC_BATCH, DEC_SEQ, D_MODEL)),
        "cache_win0": nrm((n_a, DEC_BATCH, min(A_WINDOWS[0], PAST_LEN), 2, A_HEADS_PER_GROUP, A_HEAD_DIM)),
        "cache_win1": nrm((n_a, DEC_BATCH, min(A_WINDOWS[1], PAST_LEN), 2, A_HEADS_PER_GROUP, A_HEAD_DIM)),
        "cache_win2": nrm((n_a, DEC_BATCH, min(A_WINDOWS[2], PAST_LEN), 2, A_HEADS_PER_GROUP, A_HEAD_DIM)),
        "state_hgrn": nrm((n_b, DEC_BATCH, B_HEADS, B_HEAD_DIM, B_HEAD_DIM), 0.3),
        "state_gla": nrm((n_c, DEC_BATCH, C_HEADS, C_KEY_DIM, C_VAL_DIM), 1.0),
        "cache_mem": nrm((DEPTH, DEC_BATCH, MEM_TOKENS, 2, MEM_HEADS, MEM_HEAD_DIM)),
        "mem_prompt": nrm((BATCH, MEM_TOKENS, D_MODEL)),
        "rel_bias_table": nrm((N_BUCKETS, A_HEADS), 0.5),
        "a_w_in": nrm((n_a, D_MODEL, 3 * A_GROUPS * a_width), D_MODEL ** -0.5),
        "a_w_out": nrm((n_a, a_width, D_MODEL), a_width ** -0.5),
        "b_w_in": nrm((n_b, D_MODEL, 4 * D_MODEL), D_MODEL ** -0.5),
        "b_w_out": nrm((n_b, D_MODEL, D_MODEL), D_MODEL ** -0.5),
        "b_lower_bound": nrm((DEPTH, D_MODEL), 0.5),
        "b_gnorm": gain((n_b, B_HEAD_DIM)),
        "c_w_in": nrm((n_c, D_MODEL, 2 * c_dk + 2 * c_dv + C_GATE_RANK), D_MODEL ** -0.5),
        "c_w_gate": nrm((n_c, C_GATE_RANK, c_dk), C_GATE_RANK ** -0.5),
        "c_b_gate": nrm((n_c, c_dk), 0.1),
        "c_w_out": nrm((n_c, c_dv, D_MODEL), c_dv ** -0.5),
        "c_gnorm": gain((n_c, C_VAL_DIM)),
        "mem_norm": gain((DEPTH, D_MODEL)),
        "mem_w_kv": nrm((DEPTH, D_MODEL, 2 * m_width), D_MODEL ** -0.5),
        "mem_w_q": nrm((DEPTH, D_MODEL, m_width), D_MODEL ** -0.5),
        "mem_w_out": nrm((DEPTH, m_width, D_MODEL), m_width ** -0.5),
        "norm_mix": gain((DEPTH, D_MODEL)),
        "norm_mem": gain((DEPTH, D_MODEL)),
        "norm_mlp": gain((DEPTH, D_MODEL)),
        "mlp_w_up": nrm((DEPTH, D_MODEL, D_FF), D_MODEL ** -0.5),
        "mlp_w_down": nrm((DEPTH, D_FF, D_MODEL), D_FF ** -0.5),
        "norm_final": gain((D_MODEL,)),
    }


def reference(x_prompt, x_sample, cache_win0, cache_win1, cache_win2, state_hgrn, state_gla, cache_mem,
              mem_prompt, rel_bias_table, a_w_in, a_w_out, b_w_in, b_w_out, b_lower_bound, b_gnorm,
              c_w_in, c_w_gate, c_b_gate, c_w_out, c_gnorm, mem_norm, mem_w_kv, mem_w_q, mem_w_out,
              norm_mix, norm_mem, norm_mlp, mlp_w_up, mlp_w_down, norm_final):
    weights = (rel_bias_table, a_w_in, a_w_out, b_w_in, b_w_out, b_lower_bound, b_gnorm,
               c_w_in, c_w_gate, c_b_gate, c_w_out, c_gnorm, mem_w_q, mem_w_out,
               norm_mix, norm_mem, norm_mlp, mlp_w_up, mlp_w_down, norm_final)
    Bm, M, _ = mem_prompt.shape
    p_mem = jnp.stack([(rmsnorm(mem_prompt, mem_norm[li]) @ mem_w_kv[li]).reshape(Bm, M, 2, MEM_HEADS, MEM_HEAD_DIM)
                       for li in range(DEPTH)], axis=0)
    y_prompt, p_win, p_hgrn, p_gla = run_trunk(x_prompt, p_mem, None, None, None, *weights)
    y_sample, s_win, s_hgrn, s_gla = run_trunk(x_sample, cache_mem, (cache_win0, cache_win1, cache_win2),
                                               state_hgrn, state_gla, *weights)
    p_win0, p_win1, p_win2 = p_win
    s_win0, s_win1, s_win2 = s_win
    return (y_prompt, y_sample, p_win0, p_win1, p_win2, p_hgrn, p_gla, p_mem, s_win0, s_win1, s_win2, s_hgrn, s_gla)
```

```python
import functools
import math

import numpy as np
import jax
import jax.numpy as jnp
from jax import lax
from jax.experimental import pallas as pl
from jax.experimental.pallas import tpu as pltpu

F32 = jnp.float32
BF16 = jnp.bfloat16

EPS = 1e-6
N_MIXERS = 3
A_WINDOWS = (128, 512, 2048)
A_DILATIONS = (1, 4, 16)
A_BLOCK = 128
N_BUCKETS = 32
MAX_DISTANCE = 2048
GLA_GATE_NORMALIZER = 16.0
LA_BASE = 16

V7X_VMEM_BYTES = 64 * 1024 * 1024
VMEM_LIMIT = V7X_VMEM_BYTES - 8 * 1024 * 1024
SUBLANES = 8
LANES = 128


def _cparams(n_axes):
    return pltpu.CompilerParams(dimension_semantics=("arbitrary",) * n_axes,
                                vmem_limit_bytes=VMEM_LIMIT)


def _tile(n, target, align):
    if n <= target:
        return n
    t = (target // align) * align
    while n % t:
        t -= align
    return t


def _rms(x, g):
    r = lax.rsqrt(jnp.mean(x * x, axis=-1, keepdims=True) + EPS)
    return (x * r) * g


def _dot(a, b):
    return jnp.dot(a, b, preferred_element_type=F32)


def _dot_nt(a, b):
    return lax.dot_general(a, b, (((1,), (1,)), ((), ())), preferred_element_type=F32)


def _dot_tn(a, b):
    return lax.dot_general(a, b, (((0,), (0,)), ((), ())), preferred_element_type=F32)


def _linear_body(*refs, has_gain, has_res):
    refs = list(refs)
    x_ref, w_ref = refs[0], refs[1]
    pos = 2
    g_ref = r_ref = None
    if has_gain:
        g_ref = refs[pos]
        pos += 1
    if has_res:
        r_ref = refs[pos]
        pos += 1
    o_ref, xs_ref = refs[pos], refs[pos + 1]

    @pl.when(pl.program_id(1) == 0)
    def _():
        x = x_ref[...]
        if has_gain:
            x = _rms(x, g_ref[...])
        xs_ref[...] = x.astype(BF16)

    acc = _dot(xs_ref[...], w_ref[...].astype(BF16))
    if has_res:
        acc = acc + r_ref[...]
    o_ref[...] = acc


def linear(x, w, layer, *, n_out=None, gain=None, res=None, bm, bn):
    M, K = x.shape
    N = w.shape[-1] if n_out is None else n_out
    bm, bn = _tile(M, bm, SUBLANES), _tile(N, bn, LANES)
    in_specs = [pl.BlockSpec((bm, K), lambda i, j: (i, 0)),
                pl.BlockSpec((None, K, bn), lambda i, j: (layer, 0, j))]
    args = [x, w]
    if gain is not None:
        g_arr, g_idx = gain
        in_specs.append(pl.BlockSpec((None, 1, K), lambda i, j: (g_idx, 0, 0)))
        args.append(g_arr)
    if res is not None:
        in_specs.append(pl.BlockSpec((bm, bn), lambda i, j: (i, j)))
        args.append(res)
    return pl.pallas_call(
        functools.partial(_linear_body, has_gain=gain is not None, has_res=res is not None),
        grid=(M // bm, N // bn),
        in_specs=in_specs,
        out_specs=pl.BlockSpec((bm, bn), lambda i, j: (i, j)),
        out_shape=jax.ShapeDtypeStruct((M, N), F32),
        scratch_shapes=[pltpu.VMEM((bm, K), BF16)],
        compiler_params=_cparams(2),
        name="linear",
    )(*args)


def _mlp_body(*refs, has_final):
    if has_final:
        x_ref, g_ref, wu_ref, wd_ref, fg_ref, o_ref, xs_ref = refs
    else:
        x_ref, g_ref, wu_ref, wd_ref, o_ref, xs_ref = refs
    f = pl.program_id(1)

    @pl.when(f == 0)
    def _():
        x = x_ref[...]
        xs_ref[...] = _rms(x, g_ref[...]).astype(BF16)
        o_ref[...] = x

    hdn = _dot(xs_ref[...], wu_ref[...].astype(BF16))
    hdn = jnp.square(jnp.maximum(hdn, 0.0)).astype(BF16)
    o_ref[...] += _dot(hdn, wd_ref[...].astype(BF16))

    if has_final:
        @pl.when(f == pl.num_programs(1) - 1)
        def _():
            o_ref[...] = _rms(o_ref[...], fg_ref[...])


def mlp(x, gains, w_up, w_down, layer, *, final_gain=None, bm, bf):
    M, D = x.shape
    FF = w_up.shape[-1]
    bm, bf = _tile(M, bm, SUBLANES), _tile(FF, bf, LANES)
    in_specs = [pl.BlockSpec((bm, D), lambda i, f: (i, 0)),
                pl.BlockSpec((None, 1, D), lambda i, f: (layer, 0, 0)),
                pl.BlockSpec((None, D, bf), lambda i, f: (layer, 0, f)),
                pl.BlockSpec((None, bf, D), lambda i, f: (layer, f, 0))]
    args = [x, gains, w_up, w_down]
    if final_gain is not None:
        in_specs.append(pl.BlockSpec((1, D), lambda i, f: (0, 0)))
        args.append(final_gain)
    return pl.pallas_call(
        functools.partial(_mlp_body, has_final=final_gain is not None),
        grid=(M // bm, FF // bf),
        in_specs=in_specs,
        out_specs=pl.BlockSpec((bm, D), lambda i, f: (i, 0)),
        out_shape=jax.ShapeDtypeStruct((M, D), F32),
        scratch_shapes=[pltpu.VMEM((bm, D), BF16)],
        compiler_params=_cparams(2),
        name="mlp",
    )(*args)


def _mem_body(x_ref, g_ref, wq_ref, kv_ref, wo_ref, o_ref, *, heads, hd):
    x = x_ref[...]
    xn = _rms(x, g_ref[...]).astype(BF16)
    q = _dot(xn, wq_ref[...].astype(BF16))
    width = heads * hd
    scale = hd ** -0.5
    outs = []
    for h in range(heads):
        qh = q[:, h * hd:(h + 1) * hd].astype(BF16)
        kh = kv_ref[:, h * hd:(h + 1) * hd].astype(BF16)
        vh = kv_ref[:, width + h * hd:width + (h + 1) * hd].astype(BF16)
        s = _dot_nt(qh, kh) * scale
        p = jnp.exp(s - jnp.max(s, axis=-1, keepdims=True))
        p = p / jnp.sum(p, axis=-1, keepdims=True)
        outs.append(_dot(p.astype(BF16), vh))
    o = jnp.concatenate(outs, axis=-1).astype(BF16)
    o_ref[...] = x + _dot(o, wo_ref[...].astype(BF16))


def mem_attend(x, gains, w_q, kv, w_o, layer, *, heads, hd, bm):
    B, T, D = x.shape
    Mt = kv.shape[1]
    width = heads * hd
    bm = _tile(T, bm, SUBLANES)
    return pl.pallas_call(
        functools.partial(_mem_body, heads=heads, hd=hd),
        grid=(B, T // bm),
        in_specs=[pl.BlockSpec((None, bm, D), lambda b, i: (b, i, 0)),
                  pl.BlockSpec((None, 1, D), lambda b, i: (layer, 0, 0)),
                  pl.BlockSpec((None, D, width), lambda b, i: (layer, 0, 0)),
                  pl.BlockSpec((None, Mt, 2 * width), lambda b, i: (b, 0, 0)),
                  pl.BlockSpec((None, width, D), lambda b, i: (layer, 0, 0))],
        out_specs=pl.BlockSpec((None, bm, D), lambda b, i: (b, i, 0)),
        out_shape=jax.ShapeDtypeStruct((B, T, D), F32),
        compiler_params=_cparams(2),
        name="mem_attend",
    )(x, gains, w_q, kv, w_o)


def t5_bucket(dist):
    dist = np.asarray(dist, np.int64)
    max_exact = N_BUCKETS // 2
    d = np.maximum(dist, 1).astype(np.float64)
    large = max_exact + (np.log(d / max_exact) / math.log(MAX_DISTANCE / max_exact)
                         * (N_BUCKETS - max_exact)).astype(np.int64)
    return np.where(dist < max_exact, dist, np.minimum(large, N_BUCKETS - 1)).astype(np.int32)


def _dil_attn_body(q_ref, kc_ref, kp_ref, vc_ref, vp_ref, bias_ref, o_ref, lse_ref, *, heads, hd):
    n = pl.program_id(2)
    blk = q_ref.shape[0]
    i = lax.broadcasted_iota(jnp.int32, (blk, blk), 0)
    j = lax.broadcasted_iota(jnp.int32, (blk, blk), 1)
    mask_cur = j <= i
    mask_prev = jnp.logical_and(j >= i, n > 0)
    scale = hd ** -0.5
    neg = -jnp.inf
    lses = []
    for h in range(heads):
        sl = slice(h * hd, (h + 1) * hd)
        q = q_ref[:, sl].astype(BF16)
        sc = _dot_nt(q, kc_ref[:, sl].astype(BF16)) * scale + bias_ref[h, :, blk:]
        sp = _dot_nt(q, kp_ref[:, sl].astype(BF16)) * scale + bias_ref[h, :, :blk]
        sc = jnp.where(mask_cur, sc, neg)
        sp = jnp.where(mask_prev, sp, neg)
        m = jnp.maximum(jnp.max(sc, axis=-1, keepdims=True), jnp.max(sp, axis=-1, keepdims=True))
        pc = jnp.exp(sc - m)
        pp = jnp.exp(sp - m)
        l = jnp.sum(pc, axis=-1, keepdims=True) + jnp.sum(pp, axis=-1, keepdims=True)
        o = _dot(pc.astype(BF16), vc_ref[:, sl].astype(BF16)) + _dot(pp.astype(BF16), vp_ref[:, sl].astype(BF16))
        o_ref[:, sl] = o / l
        lses.append(m + jnp.log(l))
    lse_ref[...] = jnp.concatenate(lses, axis=-1)


def dilated_attention_prompt(qkv, bias, g, d, *, groups, heads, hd):
    B, T, _ = qkv.shape
    W = heads * hd
    L = T // d
    assert T % d == 0 and L % A_BLOCK == 0
    nb = L // A_BLOCK
    per_res = 3 * groups
    x = qkv.reshape(B, L, d * per_res * W)

    def spec(which, prev):
        def imap(b, r, n):
            row = jnp.maximum(n - 1, 0) if prev else n
            return (b, row, r * per_res + which * groups + g)
        return pl.BlockSpec((None, A_BLOCK, W), imap)

    o, lse = pl.pallas_call(
        functools.partial(_dil_attn_body, heads=heads, hd=hd),
        grid=(B, d, nb),
        in_specs=[spec(0, False), spec(1, False), spec(1, True), spec(2, False), spec(2, True),
                  pl.BlockSpec((heads, A_BLOCK, 2 * A_BLOCK), lambda b, r, n: (0, 0, 0))],
        out_specs=[pl.BlockSpec((None, A_BLOCK, W), lambda b, r, n: (b, n, r)),
                   pl.BlockSpec((None, None, A_BLOCK, heads), lambda b, r, n: (b, r, n, 0))],
        out_shape=[jax.ShapeDtypeStruct((B, L, d * W), F32),
                   jax.ShapeDtypeStruct((B, d, L, heads), F32)],
        compiler_params=_cparams(3),
        name="dilated_attention_prompt",
    )(x, x, x, x, x, bias)
    return o.reshape(B, T, W), jnp.transpose(lse, (0, 2, 1, 3)).reshape(B, T, heads)


def _merge_out_body(o0_ref, o1_ref, o2_ref, l0_ref, l1_ref, l2_ref, w_ref, r_ref, out_ref, xs_ref, *, heads, hd):
    @pl.when(pl.program_id(1) == 0)
    def _():
        ls = [l0_ref[...], l1_ref[...], l2_ref[...]]
        mx = jnp.maximum(jnp.maximum(ls[0], ls[1]), ls[2])
        es = [jnp.exp(l - mx) for l in ls]
        den = es[0] + es[1] + es[2]
        ws = [e / den for e in es]
        for h in range(heads):
            sl = slice(h * hd, (h + 1) * hd)
            acc = ws[0][:, h:h + 1] * o0_ref[:, sl]
            acc = acc + ws[1][:, h:h + 1] * o1_ref[:, sl]
            acc = acc + ws[2][:, h:h + 1] * o2_ref[:, sl]
            xs_ref[:, sl] = acc.astype(BF16)

    out_ref[...] = r_ref[...] + _dot(xs_ref[...], w_ref[...].astype(BF16))


def merge_out_proj(os_, lses, w, layer, res, *, heads, hd, bm, bn):
    M, W = os_[0].shape
    N = w.shape[-1]
    bm, bn = _tile(M, bm, SUBLANES), _tile(N, bn, LANES)
    ospec = pl.BlockSpec((bm, W), lambda i, j: (i, 0))
    lspec = pl.BlockSpec((bm, heads), lambda i, j: (i, 0))
    return pl.pallas_call(
        functools.partial(_merge_out_body, heads=heads, hd=hd),
        grid=(M // bm, N // bn),
        in_specs=[ospec, ospec, ospec, lspec, lspec, lspec,
                  pl.BlockSpec((None, W, bn), lambda i, j: (layer, 0, j)),
                  pl.BlockSpec((bm, bn), lambda i, j: (i, j))],
        out_specs=pl.BlockSpec((bm, bn), lambda i, j: (i, j)),
        out_shape=jax.ShapeDtypeStruct((M, N), F32),
        scratch_shapes=[pltpu.VMEM((bm, W), BF16)],
        compiler_params=_cparams(2),
        name="merge_out_proj",
    )(*os_, *lses, w, res)


def _dec_attn_body(qkv_ref, b0_ref, b1_ref, b2_ref, bias_ref, o_ref, *, groups, heads, hd, span):
    bufs = (b0_ref, b1_ref, b2_ref)
    W = heads * hd
    scale = hd ** -0.5

    def bf_round(a):
        return a.astype(BF16).astype(F32)

    for h in range(heads):
        outs, lses = [], []
        for g in range(groups):
            base = g * W + h * hd
            q = qkv_ref[:, base:base + hd].astype(BF16)
            kn = qkv_ref[:, groups * W + base:groups * W + base + hd]
            vn = qkv_ref[:, 2 * groups * W + base:2 * groups * W + base + hd]
            kb = bufs[g][:, h * hd:(h + 1) * hd].astype(BF16)
            vb = bufs[g][:, W + h * hd:W + (h + 1) * hd].astype(BF16)
            q8 = jnp.broadcast_to(q, (SUBLANES, hd))
            s = _dot_nt(q8, kb)[0:1] * scale + bias_ref[g, h:h + 1, :span]
            s_self = (jnp.sum(q.astype(F32) * bf_round(kn), axis=-1, keepdims=True) * scale
                      + bias_ref[g, h:h + 1, span:span + 1])
            m = jnp.maximum(jnp.max(s, axis=-1, keepdims=True), s_self)
            p = jnp.exp(s - m)
            p_self = jnp.exp(s_self - m)
            l = jnp.sum(p, axis=-1, keepdims=True) + p_self
            p8 = jnp.broadcast_to((p / l).astype(BF16), (SUBLANES, span))
            o = _dot(p8, vb)[0:1] + bf_round(p_self / l) * bf_round(vn)
            outs.append(o)
            lses.append(m + jnp.log(l))
        mx = functools.reduce(jnp.maximum, lses)
        es = [jnp.exp(l - mx) for l in lses]
        den = functools.reduce(lambda a, b: a + b, es)
        acc = (es[0] / den) * outs[0]
        for g in range(1, groups):
            acc = acc + (es[g] / den) * outs[g]
        o_ref[:, h * hd:(h + 1) * hd] = acc


def dilated_attention_sample(qkv, bufs, bias, *, groups, heads, hd, span):
    B = qkv.shape[0]
    W = heads * hd
    return pl.pallas_call(
        functools.partial(_dec_attn_body, groups=groups, heads=heads, hd=hd, span=span),
        grid=(B,),
        in_specs=[pl.BlockSpec((None, 1, 3 * groups * W), lambda b: (b, 0, 0))]
        + [pl.BlockSpec((None, span, 2 * W), lambda b: (b, 0, 0)) for _ in range(groups)]
        + [pl.BlockSpec((groups, heads, 2 * span), lambda b: (0, 0, 0))],
        out_specs=pl.BlockSpec((None, 1, W), lambda b: (b, 0, 0)),
        out_shape=jax.ShapeDtypeStruct((B, 1, W), F32),
        compiler_params=_cparams(1),
        name="dilated_attention_sample",
    )(qkv, *bufs, bias)


def _gla_gate_body(x_ref, g_ref, wl_ref, wg_ref, b_ref, o_ref):
    xn = _rms(x_ref[...], g_ref[...]).astype(BF16)
    low = _dot(xn, wl_ref[...].astype(BF16))
    z = _dot(low.astype(BF16), wg_ref[...].astype(BF16)) + b_ref[...]
    log_sig = jnp.minimum(z, 0.0) - jnp.log(1.0 + jnp.exp(-jnp.abs(z)))
    o_ref[...] = log_sig / GLA_GATE_NORMALIZER


def gla_gate(x, gain, w_low, w_gate, b_gate, layer, *, bm):
    M, D = x.shape
    g_arr, g_idx = gain
    rank, dk = w_gate.shape[1], w_gate.shape[2]
    bm = _tile(M, bm, SUBLANES)
    return pl.pallas_call(
        _gla_gate_body,
        grid=(M // bm,),
        in_specs=[pl.BlockSpec((bm, D), lambda i: (i, 0)),
                  pl.BlockSpec((None, 1, D), lambda i: (g_idx, 0, 0)),
                  pl.BlockSpec((D, rank), lambda i: (0, 0)),
                  pl.BlockSpec((None, rank, dk), lambda i: (layer, 0, 0)),
                  pl.BlockSpec((None, 1, dk), lambda i: (layer, 0, 0))],
        out_specs=pl.BlockSpec((bm, dk), lambda i: (i, 0)),
        out_shape=jax.ShapeDtypeStruct((M, dk), F32),
        compiler_params=_cparams(1),
        name="gla_gate",
    )(x, g_arr, w_low, w_gate, b_gate)


def _cumsum_rows(x):
    rows = x.shape[0]
    row = lax.broadcasted_iota(jnp.int32, x.shape, 0)
    s = 1
    while s < rows:
        x = x + jnp.where(row >= s, pltpu.roll(x, s, axis=0), 0.0)
        s *= 2
    return x


def _block_refs(bcs, m, pairs):
    rows, dk = bcs.shape
    nb = rows // m
    ends = [bcs[(k + 1) * m - 1:(k + 1) * m, :] for k in range(nb)]
    parts = []
    for k in range(nb):
        if pairs:
            r = ends[k - 1] if k % 2 == 1 else ends[k]
        else:
            r = ends[k - 1] if k > 0 else jnp.zeros((1, dk), F32)
        parts.append(jnp.broadcast_to(r, (m, dk)))
    return parts[0] if nb == 1 else jnp.concatenate(parts, axis=0)


def _la_body(a_ref, b_ref, c_ref, gate_ref, e_ref, gn_ref, s0_ref, y_ref, so_ref, st_ref,
             *, hgrn2, heads, dk, dv, chunk, base, valid, q_scale):
    n = pl.program_id(1)

    @pl.when(n == 0)
    def _():
        for h in range(heads):
            st_ref[h] = s0_ref[h].T

    tq = lax.broadcasted_iota(jnp.int32, (chunk, chunk), 0)
    ts = lax.broadcasted_iota(jnp.int32, (chunk, chunk), 1)
    row = lax.broadcasted_iota(jnp.int32, (chunk, 1), 0)
    sh = int(math.log2(base))
    mask_diag = jnp.logical_and((tq >> sh) == (ts >> sh), ts <= tq)
    levels = []
    m = base
    while m < chunk:
        s1 = int(math.log2(m))
        mk = jnp.logical_and((tq >> (s1 + 1)) == (ts >> (s1 + 1)),
                             jnp.logical_and(((tq >> s1) & 1) == 1, ((ts >> s1) & 1) == 0))
        levels.append((m, mk, ((row >> s1) & 1) == 1))
        m *= 2
    gn = gn_ref[...]

    for h in range(heads):
        ksl = slice(h * dk, (h + 1) * dk)
        vsl = slice(h * dv, (h + 1) * dv)
        if hgrn2:
            a = a_ref[:, ksl]
            lb = e_ref[:, ksl]
            f = lb + (1.0 - lb) * jax.nn.sigmoid(b_ref[:, ksl])
            qa = a * jax.nn.sigmoid(a)
            kk = 1.0 - f
            g = jnp.log(f)
        else:
            qa = a_ref[:, ksl] * q_scale
            kk = b_ref[:, ksl]
            g = e_ref[:, ksl]
        v = c_ref[:, vsl]
        if valid < chunk:
            kk = jnp.where(row < valid, kk, 0.0)
            g = jnp.where(row < valid, g, 0.0)
        bcs = _cumsum_rows(g)
        vb = v.astype(BF16)
        st = st_ref[h]

        o = _dot_nt((qa * jnp.exp(bcs)).astype(BF16), st.astype(BF16))

        ref_d = _block_refs(bcs, base, False)
        a_mat = _dot_nt((qa * jnp.exp(bcs - ref_d)).astype(BF16), (kk * jnp.exp(ref_d - bcs)).astype(BF16))
        a_mat = jnp.where(mask_diag, a_mat, 0.0)
        for (m, mk, odd) in levels:
            ref_m = _block_refs(bcs, m, True)
            qm = qa * jnp.exp(jnp.where(odd, bcs - ref_m, 0.0))
            km = kk * jnp.exp(jnp.where(odd, 0.0, ref_m - bcs))
            a_mat = a_mat + jnp.where(mk, _dot_nt(qm.astype(BF16), km.astype(BF16)), 0.0)
        o = o + _dot(a_mat.astype(BF16), vb)

        b_last = bcs[chunk - 1:chunk, :]
        kd = (kk * jnp.exp(b_last - bcs)).astype(BF16)
        st_ref[h] = jnp.exp(b_last) * st + _dot_tn(vb, kd)

        gate = gate_ref[:, vsl]
        y_ref[:, vsl] = _rms(o, gn) * (gate * jax.nn.sigmoid(gate))

    @pl.when(n == pl.num_programs(1) - 1)
    def _():
        for h in range(heads):
            so_ref[h] = st_ref[h].T


def gated_linear_attention(proj, extra, gnorm, s0, *, hgrn2, chunk, valid=None):
    B, T, _ = proj.shape
    _, H, dk, dv = s0.shape
    chunk = min(chunk, T)
    assert T % chunk == 0
    base = min(LA_BASE, chunk)
    valid = chunk if valid is None else valid
    wk, wv = H * dk, H * dv

    def col(width, start):
        assert start % width == 0
        return pl.BlockSpec((None, chunk, width), lambda b, n: (b, n, start // width))

    if hgrn2:
        specs = [col(wk, 0), col(wk, wk), col(wv, 2 * wk), col(wv, 2 * wk + wv),
                 pl.BlockSpec((1, wk), lambda b, n: (0, 0))]
    else:
        specs = [col(wk, 0), col(wk, wk), col(wv, 2 * wk), col(wv, 2 * wk + wv),
                 pl.BlockSpec((None, chunk, wk), lambda b, n: (b, n, 0))]
    specs += [pl.BlockSpec((1, dv), lambda b, n: (0, 0)),
              pl.BlockSpec((None, H, dk, dv), lambda b, n: (b, 0, 0, 0))]
    return pl.pallas_call(
        functools.partial(_la_body, hgrn2=hgrn2, heads=H, dk=dk, dv=dv, chunk=chunk, base=base,
                          valid=valid, q_scale=1.0 if hgrn2 else dk ** -0.5),
        grid=(B, T // chunk),
        in_specs=specs,
        out_specs=[pl.BlockSpec((None, chunk, wv), lambda b, n: (b, n, 0)),
                   pl.BlockSpec((None, H, dk, dv), lambda b, n: (b, 0, 0, 0))],
        out_shape=[jax.ShapeDtypeStruct((B, T, wv), F32),
                   jax.ShapeDtypeStruct((B, H, dk, dv), F32)],
        scratch_shapes=[pltpu.VMEM((H, dv, dk), F32)],
        compiler_params=_cparams(2),
        name="gated_linear_attention",
    )(proj, proj, proj, proj, extra, gnorm, s0)


def _pad_rows(x, rows):
    return jnp.pad(x, ((0, 0), (0, rows - x.shape[1]), (0, 0)))


def _run_trunk(x, mem_kvs, a_bufs, b_states, c_states, P, prompt):
    B, T, D = x.shape
    M = B * T
    depth = P["norm_mix"].shape[0]
    hd_a, hpg = P["a_hd"], P["a_hpg"]
    groups = len(A_DILATIONS)
    Wa = hpg * hd_a
    if prompt:
        tile = dict(bm=1024, bn=512)
        mlp_tile = dict(bm=512, bf=512)
    else:
        tile = dict(bm=SUBLANES, bn=1024)
        mlp_tile = dict(bm=SUBLANES, bf=1024)

    h = x.reshape(M, D)
    new_a, new_b, new_c = [], [], []
    for li in range(depth):
        kind, j = li % N_MIXERS, li // N_MIXERS
        gmix = (P["norm_mix"], li)
        if kind == 0:
            qkv = linear(h, P["a_w_in"], j, gain=gmix, **tile)
            qkv3 = qkv.reshape(B, T, 3 * groups * Wa)
            k_all = qkv3[:, :, groups * Wa:2 * groups * Wa]
            v_all = qkv3[:, :, 2 * groups * Wa:]
            nbufs = []
            if prompt:
                os_, lses = [], []
                for g in range(groups):
                    o, lse = dilated_attention_prompt(qkv3, P["a_bias_prompt"][g], g, A_DILATIONS[g],
                                                      groups=groups, heads=hpg, hd=hd_a)
                    os_.append(o.reshape(M, Wa))
                    lses.append(lse.reshape(M, hpg))
                    lw = min(A_WINDOWS[g], T)
                    kg = k_all[:, T - lw:, g * Wa:(g + 1) * Wa].reshape(B, lw, hpg, hd_a)
                    vg = v_all[:, T - lw:, g * Wa:(g + 1) * Wa].reshape(B, lw, hpg, hd_a)
                    nbufs.append(jnp.stack([kg, vg], axis=2))
                h = merge_out_proj(os_, lses, P["a_w_out"], j, h, heads=hpg, hd=hd_a, bm=256, bn=1024)
            else:
                span = A_BLOCK
                views = []
                for g in range(groups):
                    buf = a_bufs[g][j]
                    d = A_DILATIONS[g]
                    assert buf.shape[1] == span * d and T == 1
                    views.append(buf.reshape(B, span, d * 2 * Wa))
                    kg = k_all[:, :, g * Wa:(g + 1) * Wa].reshape(B, T, hpg, hd_a)
                    vg = v_all[:, :, g * Wa:(g + 1) * Wa].reshape(B, T, hpg, hd_a)
                    nbufs.append(jnp.concatenate([buf[:, T:], jnp.stack([kg, vg], axis=2)], axis=1))
                o = dilated_attention_sample(qkv3, views, P["a_bias_sample"], groups=groups, heads=hpg,
                                             hd=hd_a, span=span)
                h = linear(o.reshape(M, Wa), P["a_w_out"], j, res=h, **tile)
            new_a.append(nbufs)
        elif kind == 1:
            proj = linear(h, P["b_w_in"], j, gain=gmix, **tile).reshape(B, T, -1)
            s0 = jnp.zeros((B,) + P["b_state_shape"], F32) if prompt else b_states[j]
            lb = P["lower_bounds"][li][None, :]
            if prompt:
                y, S = gated_linear_attention(proj, lb, P["b_gnorm"][j][None, :], s0, hgrn2=True, chunk=64)
            else:
                y, S = gated_linear_attention(_pad_rows(proj, SUBLANES), lb, P["b_gnorm"][j][None, :], s0,
                                              hgrn2=True, chunk=SUBLANES, valid=T)
                y = y[:, :T]
            h = linear(y.reshape(M, -1), P["b_w_out"], j, res=h, **tile)
            new_b.append(S)
        else:
            wk = P["c_wk"]
            wv = P["c_state_shape"][0] * P["c_state_shape"][2]
            n_main = 2 * wk + 2 * wv
            proj = linear(h, P["c_w_in"], j, n_out=n_main, gain=gmix, **tile).reshape(B, T, n_main)
            log_g = gla_gate(h, gmix, P["c_w_low"][j], P["c_w_gate"], P["c_b_gate"], j,
                             bm=tile["bm"]).reshape(B, T, wk)
            s0 = jnp.zeros((B,) + P["c_state_shape"], F32) if prompt else c_states[j]
            if prompt:
                y, S = gated_linear_attention(proj, log_g, P["c_gnorm"][j][None, :], s0, hgrn2=False, chunk=64)
            else:
                y, S = gated_linear_attention(_pad_rows(proj, SUBLANES), _pad_rows(log_g, SUBLANES),
                                              P["c_gnorm"][j][None, :], s0, hgrn2=False, chunk=SUBLANES, valid=T)
                y = y[:, :T]
            h = linear(y.reshape(M, -1), P["c_w_out"], j, res=h, **tile)
            new_c.append(S)

        kv = mem_kvs[li]
        kv = kv.reshape(kv.shape[0], kv.shape[1], -1)
        if prompt:
            h = mem_attend(h.reshape(B, T, D), P["norm_mem"], P["mem_w_q"], kv, P["mem_w_out"], li,
                           heads=P["m_heads"], hd=P["m_hd"], bm=512).reshape(M, D)
        else:
            hp = _pad_rows(h.reshape(B, T, D), SUBLANES)
            hp = mem_attend(hp, P["norm_mem"], P["mem_w_q"], kv, P["mem_w_out"], li,
                            heads=P["m_heads"], hd=P["m_hd"], bm=SUBLANES)
            h = hp[:, :T].reshape(M, D)
        final = P["norm_final"] if li == depth - 1 else None
        h = mlp(h, P["norm_mlp"], P["mlp_w_up"], P["mlp_w_down"], li, final_gain=final, **mlp_tile)

    y = h.reshape(B, T, D)
    win = [jnp.stack([nb[g] for nb in new_a], axis=0) for g in range(groups)]
    return y, win, jnp.stack(new_b, axis=0), jnp.stack(new_c, axis=0)


def kernel(x_prompt, x_sample, cache_win0, cache_win1, cache_win2, state_hgrn, state_gla, cache_mem, mem_prompt, rel_bias_table, a_w_in, a_w_out, b_w_in, b_w_out, b_lower_bound, b_gnorm, c_w_in, c_w_gate, c_b_gate, c_w_out, c_gnorm, mem_norm, mem_w_kv, mem_w_q, mem_w_out, norm_mix, norm_mem, norm_mlp, mlp_w_up, mlp_w_down, norm_final):
    depth, D = norm_mix.shape
    hpg, hd_a = cache_win0.shape[-2], cache_win0.shape[-1]
    groups = len(A_DILATIONS)
    span = A_BLOCK
    assert all(w // d == span for w, d in zip(A_WINDOWS, A_DILATIONS))
    m_heads, m_hd = cache_mem.shape[-2], cache_mem.shape[-1]
    c_heads, c_dk, c_dv = state_gla.shape[2:]
    wk = c_heads * c_dk
    n_main = 2 * wk + 2 * c_heads * c_dv

    steps = np.arange(A_BLOCK)[:, None] + A_BLOCK - np.arange(2 * A_BLOCK)[None, :]
    bias_prompt, bias_sample = [], []
    for g, d in enumerate(A_DILATIONS):
        tg = rel_bias_table[:, g * hpg:(g + 1) * hpg]
        bucket = t5_bucket(np.clip(steps, 0, span) * d)
        bias_prompt.append(jnp.transpose(jnp.take(tg, jnp.asarray(bucket), axis=0), (2, 0, 1)))
        dist = np.concatenate([(span - np.arange(span)) * d, np.zeros(span, np.int64)])
        bias_sample.append(jnp.take(tg, jnp.asarray(t5_bucket(dist)), axis=0).T)

    sm = jax.nn.softmax(b_lower_bound.astype(F32), axis=0)
    P = dict(
        a_hd=hd_a, a_hpg=hpg, m_heads=m_heads, m_hd=m_hd,
        a_bias_prompt=bias_prompt, a_bias_sample=jnp.stack(bias_sample, axis=0),
        a_w_in=a_w_in, a_w_out=a_w_out, b_w_in=b_w_in, b_w_out=b_w_out,
        lower_bounds=jnp.cumsum(sm, axis=0) - sm, b_gnorm=b_gnorm, b_state_shape=state_hgrn.shape[2:],
        c_w_in=c_w_in, c_w_low=c_w_in[:, :, n_main:], c_w_gate=c_w_gate,
        c_b_gate=c_b_gate[:, None, :], c_w_out=c_w_out, c_gnorm=c_gnorm, c_state_shape=state_gla.shape[2:],
        c_wk=wk,
        mem_w_q=mem_w_q, mem_w_out=mem_w_out,
        norm_mix=norm_mix[:, None, :], norm_mem=norm_mem[:, None, :], norm_mlp=norm_mlp[:, None, :],
        mlp_w_up=mlp_w_up, mlp_w_down=mlp_w_down, norm_final=norm_final[None, :],
    )

    Bm, Mt, _ = mem_prompt.shape
    mem_flat = mem_prompt.reshape(Bm * Mt, D)
    mem_gain = mem_norm[:, None, :]
    p_mem = jnp.stack([linear(mem_flat, mem_w_kv, li, gain=(mem_gain, li), bm=512, bn=512)
                       .reshape(Bm, Mt, 2, m_heads, m_hd) for li in range(depth)], axis=0)

    y_prompt, p_win, p_hgrn, p_gla = _run_trunk(x_prompt, p_mem, None, None, None, P, True)
    y_sample, s_win, s_hgrn, s_gla = _run_trunk(x_sample, cache_mem, (cache_win0, cache_win1, cache_win2),
                                                state_hgrn, state_gla, P, False)
    return (y_prompt, y_sample, p_win[0], p_win[1], p_win[2], p_hgrn, p_gla, p_mem,
            s_win[0], s_win[1], s_win[2], s_hgrn, s_gla)
```

```python
import functools
import math

import numpy as np
import jax
import jax.numpy as jnp
from jax import lax
from jax.experimental import pallas as pl
from jax.experimental.pallas import tpu as pltpu

F32 = jnp.float32
BF16 = jnp.bfloat16

EPS = 1e-6
N_MIXERS = 3
A_WINDOWS = (128, 512, 2048)
A_DILATIONS = (1, 4, 16)
A_BLOCK = 128
N_BUCKETS = 32
MAX_DISTANCE = 2048
GLA_GATE_NORMALIZER = 16.0
LA_BASE = 16

V7X_VMEM_BYTES = 64 * 1024 * 1024
VMEM_LIMIT = V7X_VMEM_BYTES - 8 * 1024 * 1024
SUBLANES = 8
LANES = 128


def _cparams(n_axes):
    return pltpu.CompilerParams(dimension_semantics=("arbitrary",) * n_axes,
                                vmem_limit_bytes=VMEM_LIMIT)


def _tile(n, target, align):
    if n <= target:
        return n
    t = (target // align) * align
    while n % t:
        t -= align
    return t


def _rms(x, g):
    r = lax.rsqrt(jnp.mean(x * x, axis=-1, keepdims=True) + EPS)
    return (x * r) * g


def _dot(a, b):
    return jnp.dot(a, b, preferred_element_type=F32)


def _dot_nt(a, b):
    return lax.dot_general(a, b, (((1,), (1,)), ((), ())), preferred_element_type=F32)


def _dot_tn(a, b):
    return lax.dot_general(a, b, (((0,), (0,)), ((), ())), preferred_element_type=F32)


def _bf_round(a):
    return a.astype(BF16).astype(F32)


def _linear_body(*refs, has_gain, has_res):
    refs = list(refs)
    x_ref, w_ref = refs[0], refs[1]
    pos = 2
    g_ref = r_ref = None
    if has_gain:
        g_ref = refs[pos]
        pos += 1
    if has_res:
        r_ref = refs[pos]
        pos += 1
    o_ref, xs_ref = refs[pos], refs[pos + 1]

    @pl.when(pl.program_id(1) == 0)
    def _():
        x = x_ref[...]
        if has_gain:
            x = _rms(x, g_ref[...])
        xs_ref[...] = x.astype(BF16)

    acc = _dot(xs_ref[...], w_ref[...].astype(BF16))
    if has_res:
        acc = acc + r_ref[...]
    o_ref[...] = acc


def linear(x, w, layer, *, n_out=None, gain=None, res=None, bm, bn):
    M, K = x.shape
    N = w.shape[-1] if n_out is None else n_out
    bm, bn = _tile(M, bm, SUBLANES), _tile(N, bn, LANES)
    in_specs = [pl.BlockSpec((bm, K), lambda i, j: (i, 0), pipeline_mode=pl.Buffered(1)),
                pl.BlockSpec((None, K, bn), lambda i, j: (layer, 0, j))]
    args = [x, w]
    if gain is not None:
        g_arr, g_idx = gain
        in_specs.append(pl.BlockSpec((None, 1, K), lambda i, j: (g_idx, 0, 0)))
        args.append(g_arr)
    if res is not None:
        in_specs.append(pl.BlockSpec((bm, bn), lambda i, j: (i, j)))
        args.append(res)
    return pl.pallas_call(
        functools.partial(_linear_body, has_gain=gain is not None, has_res=res is not None),
        grid=(M // bm, N // bn),
        in_specs=in_specs,
        out_specs=pl.BlockSpec((bm, bn), lambda i, j: (i, j)),
        out_shape=jax.ShapeDtypeStruct((M, N), F32),
        scratch_shapes=[pltpu.VMEM((bm, K), BF16)],
        compiler_params=_cparams(2),
        name="linear",
    )(*args)


def _mlp_body(*refs, has_final):
    if has_final:
        x_ref, g_ref, wu_ref, wd_ref, fg_ref, o_ref, xs_ref = refs
    else:
        x_ref, g_ref, wu_ref, wd_ref, o_ref, xs_ref = refs
    f = pl.program_id(1)

    @pl.when(f == 0)
    def _():
        x = x_ref[...]
        xs_ref[...] = _rms(x, g_ref[...]).astype(BF16)
        o_ref[...] = x

    hdn = _dot(xs_ref[...], wu_ref[...].astype(BF16))
    hdn = jnp.square(jnp.maximum(hdn, 0.0)).astype(BF16)
    o_ref[...] += _dot(hdn, wd_ref[...].astype(BF16))

    if has_final:
        @pl.when(f == pl.num_programs(1) - 1)
        def _():
            o_ref[...] = _rms(o_ref[...], fg_ref[...])


def mlp(x, gains, w_up, w_down, layer, *, final_gain=None, bm, bf):
    M, D = x.shape
    FF = w_up.shape[-1]
    bm, bf = _tile(M, bm, SUBLANES), _tile(FF, bf, LANES)
    in_specs = [pl.BlockSpec((bm, D), lambda i, f: (i, 0), pipeline_mode=pl.Buffered(1)),
                pl.BlockSpec((None, 1, D), lambda i, f: (layer, 0, 0)),
                pl.BlockSpec((None, D, bf), lambda i, f: (layer, 0, f)),
                pl.BlockSpec((None, bf, D), lambda i, f: (layer, f, 0))]
    args = [x, gains, w_up, w_down]
    if final_gain is not None:
        in_specs.append(pl.BlockSpec((1, D), lambda i, f: (0, 0)))
        args.append(final_gain)
    return pl.pallas_call(
        functools.partial(_mlp_body, has_final=final_gain is not None),
        grid=(M // bm, FF // bf),
        in_specs=in_specs,
        out_specs=pl.BlockSpec((bm, D), lambda i, f: (i, 0)),
        out_shape=jax.ShapeDtypeStruct((M, D), F32),
        scratch_shapes=[pltpu.VMEM((bm, D), BF16)],
        compiler_params=_cparams(2),
        name="mlp",
    )(*args)


def _mem_body(x_ref, g_ref, wq_ref, kv_ref, wo_ref, o_ref, *, heads, hd):
    x = x_ref[...]
    xn = _rms(x, g_ref[...]).astype(BF16)
    q = _dot(xn, wq_ref[...].astype(BF16))
    width = heads * hd
    scale = hd ** -0.5
    outs = []
    for h in range(heads):
        qh = q[:, h * hd:(h + 1) * hd].astype(BF16)
        kh = kv_ref[:, h * hd:(h + 1) * hd].astype(BF16)
        vh = kv_ref[:, width + h * hd:width + (h + 1) * hd].astype(BF16)
        s = _dot_nt(qh, kh) * scale
        p = jnp.exp(s - jnp.max(s, axis=-1, keepdims=True))
        p = p / jnp.sum(p, axis=-1, keepdims=True)
        outs.append(_dot(p.astype(BF16), vh))
    o = jnp.concatenate(outs, axis=-1).astype(BF16)
    o_ref[...] = x + _dot(o, wo_ref[...].astype(BF16))


def mem_attend(x, gains, w_q, kv, w_o, layer, *, heads, hd, bm):
    B, T, D = x.shape
    Mt = kv.shape[2]
    width = heads * hd
    bm = _tile(T, bm, SUBLANES)
    return pl.pallas_call(
        functools.partial(_mem_body, heads=heads, hd=hd),
        grid=(B, T // bm),
        in_specs=[pl.BlockSpec((None, bm, D), lambda b, i: (b, i, 0)),
                  pl.BlockSpec((None, 1, D), lambda b, i: (layer, 0, 0)),
                  pl.BlockSpec((None, D, width), lambda b, i: (layer, 0, 0)),
                  pl.BlockSpec((None, None, Mt, 2 * width), lambda b, i: (layer, b, 0, 0)),
                  pl.BlockSpec((None, width, D), lambda b, i: (layer, 0, 0))],
        out_specs=pl.BlockSpec((None, bm, D), lambda b, i: (b, i, 0)),
        out_shape=jax.ShapeDtypeStruct((B, T, D), F32),
        compiler_params=_cparams(2),
        name="mem_attend",
    )(x, gains, w_q, kv, w_o)


def t5_bucket(dist):
    dist = np.asarray(dist, np.int64)
    max_exact = N_BUCKETS // 2
    d = np.maximum(dist, 1).astype(np.float64)
    large = max_exact + (np.log(d / max_exact) / math.log(MAX_DISTANCE / max_exact)
                         * (N_BUCKETS - max_exact)).astype(np.int64)
    return np.where(dist < max_exact, dist, np.minimum(large, N_BUCKETS - 1)).astype(np.int32)


def _attn_block(q, kc, vc, kp, vp, bias_c, bias_p, mask_c, mask_p, scale):
    qb = q.astype(BF16)
    sc = jnp.where(mask_c, _dot_nt(qb, kc.astype(BF16)) * scale + bias_c, -jnp.inf)
    sp = jnp.where(mask_p, _dot_nt(qb, kp.astype(BF16)) * scale + bias_p, -jnp.inf)
    m = jnp.maximum(jnp.max(sc, axis=-1, keepdims=True), jnp.max(sp, axis=-1, keepdims=True))
    pc = jnp.exp(sc - m)
    pp = jnp.exp(sp - m)
    l = jnp.sum(pc, axis=-1, keepdims=True) + jnp.sum(pp, axis=-1, keepdims=True)
    o = _dot(pc.astype(BF16), vc.astype(BF16)) + _dot(pp.astype(BF16), vp.astype(BF16))
    return o / l, m + jnp.log(l)


def _dil_attn_body(*refs, dils, tile, hd):
    n_groups = len(dils)
    bias_ref, o_ref, og_ref, lse_ref = refs[5 * n_groups:]
    first_tile = pl.program_id(1) == 0
    blk = A_BLOCK
    i = lax.broadcasted_iota(jnp.int32, (blk, blk), 0)
    j = lax.broadcasted_iota(jnp.int32, (blk, blk), 1)
    mask_cur = j <= i
    band_prev = j >= i
    mask_prev_first = jnp.logical_and(band_prev, jnp.logical_not(first_tile))
    scale = hd ** -0.5

    for g, d in enumerate(dils):
        q_ref, kc_ref, vc_ref, kp_ref, vp_ref = refs[5 * g:5 * g + 5]
        stream_rows = blk * d
        n_blocks = tile // stream_rows

        def rows(start, d=d):
            if d == 1:
                return pl.ds(pl.multiple_of(start, blk), blk)
            return pl.ds(start, blk, stride=d)

        def emit(start, kp, vp, mask_p, g=g, q_ref=q_ref, kc_ref=kc_ref, vc_ref=vc_ref, rows=rows):
            o, lse = _attn_block(q_ref[rows(start), :], kc_ref[rows(start), :], vc_ref[rows(start), :],
                                 kp, vp, bias_ref[g, :, blk:], bias_ref[g, :, :blk], mask_cur, mask_p, scale)
            og_ref[g, rows(start), :] = o
            lse_ref[g, rows(start), :] = jnp.broadcast_to(lse, (blk, hd))

        def first(r, carry, emit=emit, rows=rows, kp_ref=kp_ref, vp_ref=vp_ref):
            emit(r, kp_ref[rows(r), :], vp_ref[rows(r), :], mask_prev_first)
            return carry

        def rest(it, carry, emit=emit, rows=rows, kc_ref=kc_ref, vc_ref=vc_ref, d=d, stream_rows=stream_rows):
            start = it % d + (1 + it // d) * stream_rows
            emit(start, kc_ref[rows(start - stream_rows), :], vc_ref[rows(start - stream_rows), :], band_prev)
            return carry

        lax.fori_loop(0, d, first, 0)
        if n_blocks > 1:
            lax.fori_loop(0, d * (n_blocks - 1), rest, 0)

    lses = [lse_ref[g] for g in range(n_groups)]
    mx = functools.reduce(jnp.maximum, lses)
    es = [jnp.exp(l - mx) for l in lses]
    den = functools.reduce(lambda a, b: a + b, es)
    acc = (es[0] / den) * og_ref[0]
    for g in range(1, n_groups):
        acc = acc + (es[g] / den) * og_ref[g]
    o_ref[...] = acc


def dilated_attention_prompt(qkv, bias, *, heads, hd, tile):
    B, T, _ = qkv.shape
    dils = A_DILATIONS
    n_groups = len(dils)
    tile = min(tile, T)
    assert hd == LANES and T % tile == 0 and all(tile % (A_BLOCK * d) == 0 for d in dils)
    sec = n_groups * heads

    def cur(which, g):
        return pl.BlockSpec((None, tile, hd), lambda b, n, h: (b, n, which * sec + g * heads + h))

    def prev(which, g):
        rows = A_BLOCK * dils[g]
        per = tile // rows
        return pl.BlockSpec((None, rows, hd),
                            lambda b, n, h: (b, jnp.maximum(n * per - 1, 0), which * sec + g * heads + h))

    in_specs = []
    for g in range(n_groups):
        in_specs += [cur(0, g), cur(1, g), cur(2, g), prev(1, g), prev(2, g)]
    in_specs.append(pl.BlockSpec((n_groups, None, A_BLOCK, 2 * A_BLOCK), lambda b, n, h: (0, h, 0, 0)))
    return pl.pallas_call(
        functools.partial(_dil_attn_body, dils=dils, tile=tile, hd=hd),
        grid=(B, T // tile, heads),
        in_specs=in_specs,
        out_specs=pl.BlockSpec((None, tile, hd), lambda b, n, h: (b, n, h)),
        out_shape=jax.ShapeDtypeStruct((B, T, heads * hd), F32),
        scratch_shapes=[pltpu.VMEM((n_groups, tile, hd), F32), pltpu.VMEM((n_groups, tile, hd), F32)],
        compiler_params=_cparams(3),
        name="dilated_attention_prompt",
    )(*([qkv] * (5 * n_groups)), bias)


def _dec_attn_body(qkv_ref, *refs, groups, span, scale):
    bufs, bias_ref, o_ref = refs[:groups], refs[groups], refs[groups + 1]
    outs, lses = [], []
    for g in range(groups):
        q = _bf_round(qkv_ref[g])
        kn = _bf_round(qkv_ref[groups + g])
        vn = _bf_round(qkv_ref[2 * groups + g])
        k = _bf_round(bufs[g][:, 0])
        v = _bf_round(bufs[g][:, 1])
        s = jnp.sum(k * q[None], axis=-1, keepdims=True) * scale + bias_ref[g, :span]
        s_self = jnp.sum(q * kn, axis=-1, keepdims=True) * scale + bias_ref[g, span]
        m = jnp.maximum(jnp.max(s, axis=0), s_self)
        p = jnp.exp(s - m[None])
        p_self = jnp.exp(s_self - m)
        l = jnp.sum(p, axis=0) + p_self
        outs.append(jnp.sum(_bf_round(p / l[None]) * v, axis=0) + _bf_round(p_self / l) * vn)
        lses.append(m + jnp.log(l))
    mx = functools.reduce(jnp.maximum, lses)
    es = [jnp.exp(l - mx) for l in lses]
    den = functools.reduce(lambda a, b: a + b, es)
    acc = (es[0] / den) * outs[0]
    for g in range(1, groups):
        acc = acc + (es[g] / den) * outs[g]
    o_ref[...] = acc


def dilated_attention_sample(qkv, caches, layer, bias, *, heads, hd, span):
    B = qkv.shape[0]
    groups = len(caches)
    in_specs = [pl.BlockSpec((None, 3 * groups, heads, hd), lambda b: (b, 0, 0, 0))]
    views = []
    for g, c in enumerate(caches):
        d = A_DILATIONS[g]
        assert c.shape[2] == span * d
        views.append(c.reshape(c.shape[0], B, span, d, 2, heads, hd))
        in_specs.append(pl.BlockSpec((None, None, span, None, 2, heads, hd),
                                     lambda b: (layer, b, 0, 0, 0, 0, 0)))
    in_specs.append(pl.BlockSpec((groups, span + 1, heads, hd), lambda b: (0, 0, 0, 0)))
    return pl.pallas_call(
        functools.partial(_dec_attn_body, groups=groups, span=span, scale=hd ** -0.5),
        grid=(B,),
        in_specs=in_specs,
        out_specs=pl.BlockSpec((None, heads, hd), lambda b: (b, 0, 0)),
        out_shape=jax.ShapeDtypeStruct((B, heads, hd), F32),
        compiler_params=_cparams(1),
        name="dilated_attention_sample",
    )(qkv, *views, bias)


def _window_update_body(*refs, n_win):
    caches, news, outs, sem = refs[:n_win], refs[n_win:2 * n_win], refs[2 * n_win:3 * n_win], refs[3 * n_win]
    copies = []
    for c_ref, n_ref, o_ref in zip(caches, news, outs):
        n_layers, batch, rows = c_ref.shape[:3]
        for j in range(n_layers):
            for b in range(batch):
                copies.append((c_ref.at[j, b, pl.ds(1, rows - 1)], o_ref.at[j, b, pl.ds(0, rows - 1)]))
        copies.append((n_ref, o_ref.at[:, :, pl.ds(rows - 1, 1)]))
    dmas = [pltpu.make_async_copy(src, dst, sem.at[k]) for k, (src, dst) in enumerate(copies)]
    for dma in dmas:
        dma.start()
    for dma in dmas:
        dma.wait()


def window_update(caches, news):
    n_win = len(caches)
    n_copies = sum(c.shape[0] * c.shape[1] + 1 for c in caches)
    any_spec = pl.BlockSpec(memory_space=pl.ANY)
    return pl.pallas_call(
        functools.partial(_window_update_body, n_win=n_win),
        in_specs=[any_spec] * (2 * n_win),
        out_specs=[any_spec] * n_win,
        out_shape=[jax.ShapeDtypeStruct(c.shape, c.dtype) for c in caches],
        scratch_shapes=[pltpu.SemaphoreType.DMA((n_copies,))],
        name="window_update",
    )(*caches, *news)


def _gla_gate_body(x_ref, g_ref, wl_ref, wg_ref, b_ref, o_ref):
    xn = _rms(x_ref[...], g_ref[...]).astype(BF16)
    low = _dot(xn, wl_ref[...].astype(BF16))
    z = _dot(low.astype(BF16), wg_ref[...].astype(BF16)) + b_ref[...]
    log_sig = jnp.minimum(z, 0.0) - jnp.log(1.0 + jnp.exp(-jnp.abs(z)))
    o_ref[...] = log_sig / GLA_GATE_NORMALIZER


def gla_gate(x, gain, w_low, w_gate, b_gate, layer, *, bm):
    M, D = x.shape
    g_arr, g_idx = gain
    rank, dk = w_gate.shape[1], w_gate.shape[2]
    bm = _tile(M, bm, SUBLANES)
    return pl.pallas_call(
        _gla_gate_body,
        grid=(M // bm,),
        in_specs=[pl.BlockSpec((bm, D), lambda i: (i, 0)),
                  pl.BlockSpec((None, 1, D), lambda i: (g_idx, 0, 0)),
                  pl.BlockSpec((D, rank), lambda i: (0, 0)),
                  pl.BlockSpec((None, rank, dk), lambda i: (layer, 0, 0)),
                  pl.BlockSpec((None, 1, dk), lambda i: (layer, 0, 0))],
        out_specs=pl.BlockSpec((bm, dk), lambda i: (i, 0)),
        out_shape=jax.ShapeDtypeStruct((M, dk), F32),
        compiler_params=_cparams(1),
        name="gla_gate",
    )(x, g_arr, w_low, w_gate, b_gate)


def _cumsum_rows(x):
    rows = x.shape[0]
    row = lax.broadcasted_iota(jnp.int32, x.shape, 0)
    s = 1
    while s < rows:
        x = x + jnp.where(row >= s, pltpu.roll(x, s, axis=0), 0.0)
        s *= 2
    return x


def _block_refs(bcs, m, pairs):
    rows, dk = bcs.shape
    nb = rows // m
    ends = [bcs[(k + 1) * m - 1:(k + 1) * m, :] for k in range(nb)]
    parts = []
    for k in range(nb):
        if pairs:
            r = ends[k - 1] if k % 2 == 1 else ends[k]
        else:
            r = ends[k - 1] if k > 0 else jnp.zeros((1, dk), F32)
        parts.append(jnp.broadcast_to(r, (m, dk)))
    return parts[0] if nb == 1 else jnp.concatenate(parts, axis=0)


def _la_body(a_ref, b_ref, c_ref, gate_ref, e_ref, gn_ref, s0_ref, y_ref, so_ref, st_ref,
             *, hgrn2, heads, dk, dv, chunk, base, valid, q_scale):
    n = pl.program_id(1)

    @pl.when(n == 0)
    def _():
        for h in range(heads):
            st_ref[h] = s0_ref[h].T

    tq = lax.broadcasted_iota(jnp.int32, (chunk, chunk), 0)
    ts = lax.broadcasted_iota(jnp.int32, (chunk, chunk), 1)
    row = lax.broadcasted_iota(jnp.int32, (chunk, 1), 0)
    sh = int(math.log2(base))
    mask_diag = jnp.logical_and((tq >> sh) == (ts >> sh), ts <= tq)
    levels = []
    m = base
    while m < chunk:
        s1 = int(math.log2(m))
        mk = jnp.logical_and((tq >> (s1 + 1)) == (ts >> (s1 + 1)),
                             jnp.logical_and(((tq >> s1) & 1) == 1, ((ts >> s1) & 1) == 0))
        levels.append((m, mk, ((row >> s1) & 1) == 1))
        m *= 2
    gn = gn_ref[...]

    for h in range(heads):
        ksl = slice(h * dk, (h + 1) * dk)
        vsl = slice(h * dv, (h + 1) * dv)
        if hgrn2:
            a = a_ref[:, ksl]
            lb = e_ref[:, ksl]
            f = lb + (1.0 - lb) * jax.nn.sigmoid(b_ref[:, ksl])
            qa = a * jax.nn.sigmoid(a)
            kk = 1.0 - f
            g = jnp.log(f)
        else:
            qa = a_ref[:, ksl] * q_scale
            kk = b_ref[:, ksl]
            g = e_ref[:, ksl]
        v = c_ref[:, vsl]
        if valid < chunk:
            kk = jnp.where(row < valid, kk, 0.0)
            g = jnp.where(row < valid, g, 0.0)
        bcs = _cumsum_rows(g)
        vb = v.astype(BF16)
        st = st_ref[h]

        o = _dot_nt((qa * jnp.exp(bcs)).astype(BF16), st.astype(BF16))

        ref_d = _block_refs(bcs, base, False)
        a_mat = _dot_nt((qa * jnp.exp(bcs - ref_d)).astype(BF16), (kk * jnp.exp(ref_d - bcs)).astype(BF16))
        a_mat = jnp.where(mask_diag, a_mat, 0.0)
        for (m, mk, odd) in levels:
            ref_m = _block_refs(bcs, m, True)
            qm = qa * jnp.exp(jnp.where(odd, bcs - ref_m, 0.0))
            km = kk * jnp.exp(jnp.where(odd, 0.0, ref_m - bcs))
            a_mat = a_mat + jnp.where(mk, _dot_nt(qm.astype(BF16), km.astype(BF16)), 0.0)
        o = o + _dot(a_mat.astype(BF16), vb)

        b_last = bcs[chunk - 1:chunk, :]
        kd = (kk * jnp.exp(b_last - bcs)).astype(BF16)
        st_ref[h] = jnp.exp(b_last) * st + _dot_tn(vb, kd)

        gate = gate_ref[:, vsl]
        y_ref[:, vsl] = _rms(o, gn) * (gate * jax.nn.sigmoid(gate))

    @pl.when(n == pl.num_programs(1) - 1)
    def _():
        for h in range(heads):
            so_ref[h] = st_ref[h].T


def gated_linear_attention(proj, extra, gnorm, s0, *, hgrn2, chunk, valid=None):
    B, T, _ = proj.shape
    _, H, dk, dv = s0.shape
    chunk = min(chunk, T)
    assert T % chunk == 0
    base = min(LA_BASE, chunk)
    valid = chunk if valid is None else valid
    wk, wv = H * dk, H * dv

    def col(width, start):
        assert start % width == 0
        return pl.BlockSpec((None, chunk, width), lambda b, n: (b, n, start // width))

    if hgrn2:
        specs = [col(wk, 0), col(wk, wk), col(wv, 2 * wk), col(wv, 2 * wk + wv),
                 pl.BlockSpec((1, wk), lambda b, n: (0, 0))]
    else:
        specs = [col(wk, 0), col(wk, wk), col(wv, 2 * wk), col(wv, 2 * wk + wv),
                 pl.BlockSpec((None, chunk, wk), lambda b, n: (b, n, 0))]
    specs += [pl.BlockSpec((1, dv), lambda b, n: (0, 0)),
              pl.BlockSpec((None, H, dk, dv), lambda b, n: (b, 0, 0, 0))]
    return pl.pallas_call(
        functools.partial(_la_body, hgrn2=hgrn2, heads=H, dk=dk, dv=dv, chunk=chunk, base=base,
                          valid=valid, q_scale=1.0 if hgrn2 else dk ** -0.5),
        grid=(B, T // chunk),
        in_specs=specs,
        out_specs=[pl.BlockSpec((None, chunk, wv), lambda b, n: (b, n, 0)),
                   pl.BlockSpec((None, H, dk, dv), lambda b, n: (b, 0, 0, 0))],
        out_shape=[jax.ShapeDtypeStruct((B, T, wv), F32),
                   jax.ShapeDtypeStruct((B, H, dk, dv), F32)],
        scratch_shapes=[pltpu.VMEM((H, dv, dk), F32)],
        compiler_params=_cparams(2),
        name="gated_linear_attention",
    )(proj, proj, proj, proj, extra, gnorm, s0)


def _pad_rows(x, rows):
    return jnp.pad(x, ((0, 0), (0, rows - x.shape[1]), (0, 0)))


def _run_trunk(x, mem_kvs, a_caches, b_states, c_states, P, prompt):
    B, T, D = x.shape
    M = B * T
    depth = P["norm_mix"].shape[0]
    hd_a, hpg = P["a_hd"], P["a_hpg"]
    groups = len(A_DILATIONS)
    Wa = hpg * hd_a
    if prompt:
        tile = dict(bm=2048, bn=256)
        mlp_tile = dict(bm=1024, bf=256)
    else:
        tile = dict(bm=SUBLANES, bn=1024)
        mlp_tile = dict(bm=SUBLANES, bf=1024)

    h = x.reshape(M, D)
    new_a, new_b, new_c = [], [], []
    for li in range(depth):
        kind, j = li % N_MIXERS, li // N_MIXERS
        gmix = (P["norm_mix"], li)
        if kind == 0:
            qkv = linear(h, P["a_w_in"], j, gain=gmix, **tile)
            if prompt:
                qkv3 = qkv.reshape(B, T, 3 * groups * Wa)
                o = dilated_attention_prompt(qkv3, P["a_bias_prompt"], heads=hpg, hd=hd_a, tile=16 * A_BLOCK)
                nbufs = []
                for g in range(groups):
                    lw = min(A_WINDOWS[g], T)
                    kg = qkv3[:, T - lw:, (groups + g) * Wa:(groups + g + 1) * Wa].reshape(B, lw, hpg, hd_a)
                    vg = qkv3[:, T - lw:, (2 * groups + g) * Wa:(2 * groups + g + 1) * Wa].reshape(B, lw, hpg, hd_a)
                    nbufs.append(jnp.stack([kg, vg], axis=2))
                new_a.append(nbufs)
            else:
                assert T == 1
                qkv5 = qkv.reshape(B, 3, groups, hpg, hd_a)
                o = dilated_attention_sample(qkv5.reshape(B, 3 * groups, hpg, hd_a), a_caches, j,
                                             P["a_bias_sample"], heads=hpg, hd=hd_a, span=A_BLOCK)
                new_a.append([qkv5[:, 1:3, g][:, None] for g in range(groups)])
            h = linear(o.reshape(M, Wa), P["a_w_out"], j, res=h, **tile)
        elif kind == 1:
            proj = linear(h, P["b_w_in"], j, gain=gmix, **tile).reshape(B, T, -1)
            s0 = jnp.zeros((B,) + P["b_state_shape"], F32) if prompt else b_states[j]
            lb = P["lower_bounds"][li][None, :]
            if prompt:
                y, S = gated_linear_attention(proj, lb, P["b_gnorm"][j][None, :], s0, hgrn2=True, chunk=64)
            else:
                y, S = gated_linear_attention(_pad_rows(proj, SUBLANES), lb, P["b_gnorm"][j][None, :], s0,
                                              hgrn2=True, chunk=SUBLANES, valid=T)
                y = y[:, :T]
            h = linear(y.reshape(M, -1), P["b_w_out"], j, res=h, **tile)
            new_b.append(S)
        else:
            wk = P["c_wk"]
            wv = P["c_state_shape"][0] * P["c_state_shape"][2]
            n_main = 2 * wk + 2 * wv
            proj = linear(h, P["c_w_in"], j, n_out=n_main, gain=gmix, **tile).reshape(B, T, n_main)
            log_g = gla_gate(h, gmix, P["c_w_low"][j], P["c_w_gate"], P["c_b_gate"], j,
                             bm=1024).reshape(B, T, wk)
            s0 = jnp.zeros((B,) + P["c_state_shape"], F32) if prompt else c_states[j]
            if prompt:
                y, S = gated_linear_attention(proj, log_g, P["c_gnorm"][j][None, :], s0, hgrn2=False, chunk=64)
            else:
                y, S = gated_linear_attention(_pad_rows(proj, SUBLANES), _pad_rows(log_g, SUBLANES),
                                              P["c_gnorm"][j][None, :], s0, hgrn2=False, chunk=SUBLANES, valid=T)
                y = y[:, :T]
            h = linear(y.reshape(M, -1), P["c_w_out"], j, res=h, **tile)
            new_c.append(S)

        if prompt:
            h = mem_attend(h.reshape(B, T, D), P["norm_mem"], P["mem_w_q"], mem_kvs, P["mem_w_out"], li,
                           heads=P["m_heads"], hd=P["m_hd"], bm=512).reshape(M, D)
        else:
            hp = _pad_rows(h.reshape(B, T, D), SUBLANES)
            hp = mem_attend(hp, P["norm_mem"], P["mem_w_q"], mem_kvs, P["mem_w_out"], li,
                            heads=P["m_heads"], hd=P["m_hd"], bm=SUBLANES)
            h = hp[:, :T].reshape(M, D)
        final = P["norm_final"] if li == depth - 1 else None
        h = mlp(h, P["norm_mlp"], P["mlp_w_up"], P["mlp_w_down"], li, final_gain=final, **mlp_tile)

    y = h.reshape(B, T, D)
    win = [jnp.stack([nb[g] for nb in new_a], axis=0) for g in range(groups)]
    return y, win, jnp.stack(new_b, axis=0), jnp.stack(new_c, axis=0)


def _bias_tables(rel_bias_table, hpg, span):
    steps = np.arange(A_BLOCK)[:, None] + A_BLOCK - np.arange(2 * A_BLOCK)[None, :]
    eye = np.eye(N_BUCKETS, dtype=np.float32)
    prompt, sample = [], []
    for g, d in enumerate(A_DILATIONS):
        tg_t = rel_bias_table[:, g * hpg:(g + 1) * hpg].T
        onehot = eye[:, t5_bucket(np.clip(steps, 0, span) * d).reshape(-1)]
        prompt.append(jnp.dot(tg_t, jnp.asarray(onehot), precision=lax.Precision.HIGHEST)
                      .reshape(hpg, A_BLOCK, 2 * A_BLOCK))
        dist = np.concatenate([(span - np.arange(span)) * d, np.zeros(1, np.int64)])
        row = jnp.dot(tg_t, jnp.asarray(eye[:, t5_bucket(dist)]), precision=lax.Precision.HIGHEST)
        sample.append(jnp.broadcast_to(row.T[:, :, None], (span + 1, hpg, LANES)))
    return jnp.stack(prompt, axis=0), jnp.stack(sample, axis=0)


def kernel(x_prompt, x_sample, cache_win0, cache_win1, cache_win2, state_hgrn, state_gla, cache_mem, mem_prompt, rel_bias_table, a_w_in, a_w_out, b_w_in, b_w_out, b_lower_bound, b_gnorm, c_w_in, c_w_gate, c_b_gate, c_w_out, c_gnorm, mem_norm, mem_w_kv, mem_w_q, mem_w_out, norm_mix, norm_mem, norm_mlp, mlp_w_up, mlp_w_down, norm_final):
    depth, D = norm_mix.shape
    hpg, hd_a = cache_win0.shape[-2], cache_win0.shape[-1]
    span = A_BLOCK
    assert all(w // d == span for w, d in zip(A_WINDOWS, A_DILATIONS))
    m_heads, m_hd = cache_mem.shape[-2], cache_mem.shape[-1]
    c_heads, c_dk, c_dv = state_gla.shape[2:]
    wk = c_heads * c_dk
    n_main = 2 * wk + 2 * c_heads * c_dv
    bias_prompt, bias_sample = _bias_tables(rel_bias_table, hpg, span)

    sm = jax.nn.softmax(b_lower_bound.astype(F32), axis=0)
    P = dict(
        a_hd=hd_a, a_hpg=hpg, m_heads=m_heads, m_hd=m_hd,
        a_bias_prompt=bias_prompt, a_bias_sample=bias_sample,
        a_w_in=a_w_in, a_w_out=a_w_out, b_w_in=b_w_in, b_w_out=b_w_out,
        lower_bounds=jnp.cumsum(sm, axis=0) - sm, b_gnorm=b_gnorm, b_state_shape=state_hgrn.shape[2:],
        c_w_in=c_w_in, c_w_low=c_w_in[:, :, n_main:], c_w_gate=c_w_gate,
        c_b_gate=c_b_gate[:, None, :], c_w_out=c_w_out, c_gnorm=c_gnorm, c_state_shape=state_gla.shape[2:],
        c_wk=wk,
        mem_w_q=mem_w_q, mem_w_out=mem_w_out,
        norm_mix=norm_mix[:, None, :], norm_mem=norm_mem[:, None, :], norm_mlp=norm_mlp[:, None, :],
        mlp_w_up=mlp_w_up, mlp_w_down=mlp_w_down, norm_final=norm_final[None, :],
    )

    Bm, Mt, _ = mem_prompt.shape
    mem_flat = mem_prompt.reshape(Bm * Mt, D)
    mem_gain = mem_norm[:, None, :]
    p_mem = jnp.stack([linear(mem_flat, mem_w_kv, li, gain=(mem_gain, li), bm=512, bn=512)
                       for li in range(depth)], axis=0).reshape(depth, Bm, Mt, 2 * m_heads * m_hd)

    caches = (cache_win0, cache_win1, cache_win2)
    y_prompt, p_win, p_hgrn, p_gla = _run_trunk(x_prompt, p_mem, None, None, None, P, True)
    y_sample, new_rows, s_hgrn, s_gla = _run_trunk(
        x_sample, cache_mem.reshape(cache_mem.shape[:3] + (-1,)), caches, state_hgrn, state_gla, P, False)
    s_win = window_update(caches, new_rows)
    return (y_prompt, y_sample, p_win[0], p_win[1], p_win[2], p_hgrn, p_gla,
            p_mem.reshape(depth, Bm, Mt, 2, m_heads, m_hd),
            s_win[0], s_win[1], s_win[2], s_hgrn, s_gla)
```

```python
import functools
import math

import numpy as np
import jax
import jax.numpy as jnp
from jax import lax
from jax.experimental import pallas as pl
from jax.experimental.pallas import tpu as pltpu

F32 = jnp.float32
BF16 = jnp.bfloat16

EPS = 1e-6
N_MIXERS = 3
A_WINDOWS = (128, 512, 2048)
A_DILATIONS = (1, 4, 16)
A_BLOCK = 128
N_BUCKETS = 32
MAX_DISTANCE = 2048
GLA_GATE_NORMALIZER = 16.0
LA_BASE = 16

V7X_VMEM_BYTES = 64 * 1024 * 1024
VMEM_LIMIT = V7X_VMEM_BYTES - 8 * 1024 * 1024
SUBLANES = 8
LANES = 128


def _cparams(n_axes):
    return pltpu.CompilerParams(dimension_semantics=("arbitrary",) * n_axes,
                                vmem_limit_bytes=VMEM_LIMIT)


def _tile(n, target, align):
    if n <= target:
        return n
    t = (target // align) * align
    while n % t:
        t -= align
    return t


def _rms(x, g):
    r = lax.rsqrt(jnp.mean(x * x, axis=-1, keepdims=True) + EPS)
    return (x * r) * g


def _dot(a, b):
    return jnp.dot(a, b, preferred_element_type=F32)


def _dot_nt(a, b):
    return lax.dot_general(a, b, (((1,), (1,)), ((), ())), preferred_element_type=F32)


def _dot_tn(a, b):
    return lax.dot_general(a, b, (((0,), (0,)), ((), ())), preferred_element_type=F32)


def _bf_round(a):
    return a.astype(BF16).astype(F32)


def _linear_body(*refs, has_gain, has_res):
    refs = list(refs)
    x_ref, w_ref = refs[0], refs[1]
    pos = 2
    g_ref = r_ref = None
    if has_gain:
        g_ref = refs[pos]
        pos += 1
    if has_res:
        r_ref = refs[pos]
        pos += 1
    o_ref, xs_ref = refs[pos], refs[pos + 1]

    @pl.when(pl.program_id(1) == 0)
    def _():
        x = x_ref[...]
        if has_gain:
            x = _rms(x, g_ref[...])
        xs_ref[...] = x.astype(BF16)

    acc = _dot(xs_ref[...], w_ref[...].astype(BF16))
    if has_res:
        acc = acc + r_ref[...]
    o_ref[...] = acc


def linear(x, w, layer, *, n_out=None, gain=None, res=None, bm, bn):
    M, K = x.shape
    N = w.shape[-1] if n_out is None else n_out
    bm, bn = _tile(M, bm, SUBLANES), _tile(N, bn, LANES)
    in_specs = [pl.BlockSpec((bm, K), lambda i, j: (i, 0), pipeline_mode=pl.Buffered(1)),
                pl.BlockSpec((None, K, bn), lambda i, j: (layer, 0, j))]
    args = [x, w]
    if gain is not None:
        g_arr, g_idx = gain
        in_specs.append(pl.BlockSpec((None, 1, K), lambda i, j: (g_idx, 0, 0)))
        args.append(g_arr)
    if res is not None:
        in_specs.append(pl.BlockSpec((bm, bn), lambda i, j: (i, j)))
        args.append(res)
    return pl.pallas_call(
        functools.partial(_linear_body, has_gain=gain is not None, has_res=res is not None),
        grid=(M // bm, N // bn),
        in_specs=in_specs,
        out_specs=pl.BlockSpec((bm, bn), lambda i, j: (i, j)),
        out_shape=jax.ShapeDtypeStruct((M, N), F32),
        scratch_shapes=[pltpu.VMEM((bm, K), BF16)],
        compiler_params=_cparams(2),
        name="linear",
    )(*args)


def _mlp_body(*refs, has_final):
    if has_final:
        x_ref, g_ref, wu_ref, wd_ref, fg_ref, o_ref, xs_ref = refs
    else:
        x_ref, g_ref, wu_ref, wd_ref, o_ref, xs_ref = refs
    f = pl.program_id(1)

    @pl.when(f == 0)
    def _():
        x = x_ref[...]
        xs_ref[...] = _rms(x, g_ref[...]).astype(BF16)
        o_ref[...] = x

    hdn = _dot(xs_ref[...], wu_ref[...].astype(BF16))
    hdn = jnp.square(jnp.maximum(hdn, 0.0)).astype(BF16)
    o_ref[...] += _dot(hdn, wd_ref[...].astype(BF16))

    if has_final:
        @pl.when(f == pl.num_programs(1) - 1)
        def _():
            o_ref[...] = _rms(o_ref[...], fg_ref[...])


def mlp(x, gains, w_up, w_down, layer, *, final_gain=None, bm, bf):
    M, D = x.shape
    FF = w_up.shape[-1]
    bm, bf = _tile(M, bm, SUBLANES), _tile(FF, bf, LANES)
    in_specs = [pl.BlockSpec((bm, D), lambda i, f: (i, 0), pipeline_mode=pl.Buffered(1)),
                pl.BlockSpec((None, 1, D), lambda i, f: (layer, 0, 0)),
                pl.BlockSpec((None, D, bf), lambda i, f: (layer, 0, f)),
                pl.BlockSpec((None, bf, D), lambda i, f: (layer, f, 0))]
    args = [x, gains, w_up, w_down]
    if final_gain is not None:
        in_specs.append(pl.BlockSpec((1, D), lambda i, f: (0, 0)))
        args.append(final_gain)
    return pl.pallas_call(
        functools.partial(_mlp_body, has_final=final_gain is not None),
        grid=(M // bm, FF // bf),
        in_specs=in_specs,
        out_specs=pl.BlockSpec((bm, D), lambda i, f: (i, 0), pipeline_mode=pl.Buffered(1)),
        out_shape=jax.ShapeDtypeStruct((M, D), F32),
        scratch_shapes=[pltpu.VMEM((bm, D), BF16)],
        compiler_params=_cparams(2),
        name="mlp",
    )(*args)


def _mem_body(x_ref, g_ref, wq_ref, kv_ref, wo_ref, o_ref, *, heads, hd):
    x = x_ref[...]
    xn = _rms(x, g_ref[...]).astype(BF16)
    q = _dot(xn, wq_ref[...].astype(BF16))
    width = heads * hd
    scale = hd ** -0.5
    outs = []
    for h in range(heads):
        qh = q[:, h * hd:(h + 1) * hd].astype(BF16)
        kh = kv_ref[:, h * hd:(h + 1) * hd].astype(BF16)
        vh = kv_ref[:, width + h * hd:width + (h + 1) * hd].astype(BF16)
        s = _dot_nt(qh, kh) * scale
        p = jnp.exp(s - jnp.max(s, axis=-1, keepdims=True))
        p = p / jnp.sum(p, axis=-1, keepdims=True)
        outs.append(_dot(p.astype(BF16), vh))
    o = jnp.concatenate(outs, axis=-1).astype(BF16)
    o_ref[...] = x + _dot(o, wo_ref[...].astype(BF16))


def mem_attend(x, gains, w_q, kv, w_o, layer, *, heads, hd, bm):
    B, T, D = x.shape
    Mt = kv.shape[2]
    width = heads * hd
    bm = _tile(T, bm, SUBLANES)
    return pl.pallas_call(
        functools.partial(_mem_body, heads=heads, hd=hd),
        grid=(B, T // bm),
        in_specs=[pl.BlockSpec((None, bm, D), lambda b, i: (b, i, 0)),
                  pl.BlockSpec((None, 1, D), lambda b, i: (layer, 0, 0)),
                  pl.BlockSpec((None, D, width), lambda b, i: (layer, 0, 0)),
                  pl.BlockSpec((None, None, Mt, 2 * width), lambda b, i: (layer, b, 0, 0)),
                  pl.BlockSpec((None, width, D), lambda b, i: (layer, 0, 0))],
        out_specs=pl.BlockSpec((None, bm, D), lambda b, i: (b, i, 0)),
        out_shape=jax.ShapeDtypeStruct((B, T, D), F32),
        compiler_params=_cparams(2),
        name="mem_attend",
    )(x, gains, w_q, kv, w_o)


def t5_bucket(dist):
    dist = np.asarray(dist, np.int64)
    max_exact = N_BUCKETS // 2
    d = np.maximum(dist, 1).astype(np.float64)
    large = max_exact + (np.log(d / max_exact) / math.log(MAX_DISTANCE / max_exact)
                         * (N_BUCKETS - max_exact)).astype(np.int64)
    return np.where(dist < max_exact, dist, np.minimum(large, N_BUCKETS - 1)).astype(np.int32)


ATTN_UNROLL = 4


def _unroll(trip):
    return max(u for u in range(1, ATTN_UNROLL + 2) if trip % u == 0)


def _attn_block(q, k2, v2, bias2, scale):
    blk = q.shape[0]
    s = _dot_nt(q.astype(BF16), k2.astype(BF16)) * scale + bias2
    m = jnp.max(jnp.maximum(s[:, :blk], s[:, blk:]), axis=-1, keepdims=True)
    p = jnp.exp(s - m)
    l = jnp.sum(p[:, :blk] + p[:, blk:], axis=-1, keepdims=True)
    o = _dot(p.astype(BF16), v2.astype(BF16))
    return o / l, m + jnp.log(l)


def _dil_attn_body(*refs, dils, tile, hd):
    n_groups = len(dils)
    bias_ref, o_ref, og_ref, lse_ref = refs[5 * n_groups:]
    not_first_tile = pl.program_id(1) > 0
    blk = A_BLOCK
    scale = hd ** -0.5

    for g, d in enumerate(dils):
        q_ref, kc_ref, vc_ref, kp_ref, vp_ref = refs[5 * g:5 * g + 5]
        stream_rows = blk * d
        n_blocks = tile // stream_rows

        def rows(start, d=d):
            if d == 1:
                return pl.ds(pl.multiple_of(start, blk), blk)
            return pl.ds(start, blk, stride=d)

        bias2 = bias_ref[g]
        bias2_first = jnp.concatenate(
            [jnp.where(not_first_tile, bias2[:, :blk], -jnp.inf), bias2[:, blk:]], axis=1)

        def run(starts, prev_of, bias, g=g, q_ref=q_ref, kc_ref=kc_ref, vc_ref=vc_ref, rows=rows):
            results = []
            for s in starts:
                kp, vp = prev_of(s)
                k2 = jnp.concatenate([kp, kc_ref[rows(s), :]], axis=0)
                v2 = jnp.concatenate([vp, vc_ref[rows(s), :]], axis=0)
                results.append(_attn_block(q_ref[rows(s), :], k2, v2, bias, scale))
            for s, (o, lse) in zip(starts, results):
                og_ref[g, rows(s), :] = o
                lse_ref[g, rows(s), :] = jnp.broadcast_to(lse, (blk, hd))

        def first(it, carry, run=run, rows=rows, kp_ref=kp_ref, vp_ref=vp_ref, u=_unroll(d)):
            run([it * u + k for k in range(u)], lambda s: (kp_ref[rows(s), :], vp_ref[rows(s), :]), bias2_first)
            return carry

        n_rest = d * (n_blocks - 1)

        def rest(it, carry, run=run, rows=rows, kc_ref=kc_ref, vc_ref=vc_ref, d=d, stream_rows=stream_rows,
                 u=_unroll(n_rest)):
            idx = [it * u + k for k in range(u)]
            run([i % d + (1 + i // d) * stream_rows for i in idx],
                lambda s: (kc_ref[rows(s - stream_rows), :], vc_ref[rows(s - stream_rows), :]), bias2)
            return carry

        lax.fori_loop(0, d // _unroll(d), first, 0)
        if n_blocks > 1:
            lax.fori_loop(0, n_rest // _unroll(n_rest), rest, 0)

    lses = [lse_ref[g] for g in range(n_groups)]
    mx = functools.reduce(jnp.maximum, lses)
    es = [jnp.exp(l - mx) for l in lses]
    den = functools.reduce(lambda a, b: a + b, es)
    acc = (es[0] / den) * og_ref[0]
    for g in range(1, n_groups):
        acc = acc + (es[g] / den) * og_ref[g]
    o_ref[...] = acc


def dilated_attention_prompt(qkv, bias, *, heads, hd, tile):
    B, T, _ = qkv.shape
    dils = A_DILATIONS
    n_groups = len(dils)
    tile = min(tile, T)
    assert hd == LANES and T % tile == 0 and all(tile % (A_BLOCK * d) == 0 for d in dils)
    sec = n_groups * heads

    def cur(which, g):
        return pl.BlockSpec((None, tile, hd), lambda b, n, h: (b, n, which * sec + g * heads + h))

    def prev(which, g):
        rows = A_BLOCK * dils[g]
        per = tile // rows
        return pl.BlockSpec((None, rows, hd),
                            lambda b, n, h: (b, jnp.maximum(n * per - 1, 0), which * sec + g * heads + h))

    in_specs = []
    for g in range(n_groups):
        in_specs += [cur(0, g), cur(1, g), cur(2, g), prev(1, g), prev(2, g)]
    in_specs.append(pl.BlockSpec((n_groups, None, A_BLOCK, 2 * A_BLOCK), lambda b, n, h: (0, h, 0, 0)))
    return pl.pallas_call(
        functools.partial(_dil_attn_body, dils=dils, tile=tile, hd=hd),
        grid=(B, T // tile, heads),
        in_specs=in_specs,
        out_specs=pl.BlockSpec((None, tile, hd), lambda b, n, h: (b, n, h)),
        out_shape=jax.ShapeDtypeStruct((B, T, heads * hd), F32),
        scratch_shapes=[pltpu.VMEM((n_groups, tile, hd), F32), pltpu.VMEM((n_groups, tile, hd), F32)],
        compiler_params=_cparams(3),
        name="dilated_attention_prompt",
    )(*([qkv] * (5 * n_groups)), bias)


def _dec_attn_body(qkv_ref, *refs, groups, span, scale):
    bufs, bias_ref, o_ref = refs[:groups], refs[groups], refs[groups + 1]
    outs, lses = [], []
    for g in range(groups):
        q = _bf_round(qkv_ref[g])
        kn = _bf_round(qkv_ref[groups + g])
        vn = _bf_round(qkv_ref[2 * groups + g])
        k = _bf_round(bufs[g][:, 0])
        v = _bf_round(bufs[g][:, 1])
        s = jnp.sum(k * q[None], axis=-1, keepdims=True) * scale + bias_ref[g, :span]
        s_self = jnp.sum(q * kn, axis=-1, keepdims=True) * scale + bias_ref[g, span]
        m = jnp.maximum(jnp.max(s, axis=0), s_self)
        p = jnp.exp(s - m[None])
        p_self = jnp.exp(s_self - m)
        l = jnp.sum(p, axis=0) + p_self
        outs.append(jnp.sum(_bf_round(p / l[None]) * v, axis=0) + _bf_round(p_self / l) * vn)
        lses.append(m + jnp.log(l))
    mx = functools.reduce(jnp.maximum, lses)
    es = [jnp.exp(l - mx) for l in lses]
    den = functools.reduce(lambda a, b: a + b, es)
    acc = (es[0] / den) * outs[0]
    for g in range(1, groups):
        acc = acc + (es[g] / den) * outs[g]
    o_ref[...] = acc


def dilated_attention_sample(qkv, caches, layer, bias, *, heads, hd, span):
    B = qkv.shape[0]
    groups = len(caches)
    in_specs = [pl.BlockSpec((None, 3 * groups, heads, hd), lambda b: (b, 0, 0, 0))]
    views = []
    for g, c in enumerate(caches):
        d = A_DILATIONS[g]
        assert c.shape[2] == span * d
        views.append(c.reshape(c.shape[0], B, span, d, 2, heads, hd))
        in_specs.append(pl.BlockSpec((None, None, span, None, 2, heads, hd),
                                     lambda b: (layer, b, 0, 0, 0, 0, 0)))
    in_specs.append(pl.BlockSpec((groups, span + 1, heads, hd), lambda b: (0, 0, 0, 0)))
    return pl.pallas_call(
        functools.partial(_dec_attn_body, groups=groups, span=span, scale=hd ** -0.5),
        grid=(B,),
        in_specs=in_specs,
        out_specs=pl.BlockSpec((None, heads, hd), lambda b: (b, 0, 0)),
        out_shape=jax.ShapeDtypeStruct((B, heads, hd), F32),
        compiler_params=_cparams(1),
        name="dilated_attention_sample",
    )(qkv, *views, bias)


def _window_update_body(main_ref, next_ref, new_ref, o_ref):
    rows = main_ref.shape[0]
    last_chunk = pl.program_id(2) == pl.num_programs(2) - 1
    o_ref[0:rows - 1] = main_ref[1:rows]
    o_ref[rows - 1:rows] = jnp.where(last_chunk, new_ref[...], next_ref[...])


def window_update(cache, new, *, chunk):
    n_layers, B, rows = cache.shape[:3]
    tail = cache.shape[3:]
    chunk = _tile(rows, chunk, 1)
    zeros = (0,) * len(tail)
    return pl.pallas_call(
        _window_update_body,
        grid=(n_layers, B, rows // chunk),
        in_specs=[pl.BlockSpec((None, None, chunk) + tail, lambda j, b, c: (j, b, c) + zeros),
                  pl.BlockSpec((None, None, 1) + tail,
                               lambda j, b, c: (j, b, jnp.minimum((c + 1) * chunk, rows - 1)) + zeros),
                  pl.BlockSpec((None, None, 1) + tail, lambda j, b, c: (j, b, 0) + zeros)],
        out_specs=pl.BlockSpec((None, None, chunk) + tail, lambda j, b, c: (j, b, c) + zeros),
        out_shape=jax.ShapeDtypeStruct(cache.shape, cache.dtype),
        compiler_params=_cparams(3),
        name="window_update",
    )(cache, cache, new)


def _gla_gate_body(x_ref, g_ref, wl_ref, wg_ref, b_ref, o_ref):
    xn = _rms(x_ref[...], g_ref[...]).astype(BF16)
    low = _dot(xn, wl_ref[...].astype(BF16))
    z = _dot(low.astype(BF16), wg_ref[...].astype(BF16)) + b_ref[...]
    log_sig = jnp.minimum(z, 0.0) - jnp.log(1.0 + jnp.exp(-jnp.abs(z)))
    o_ref[...] = log_sig / GLA_GATE_NORMALIZER


def gla_gate(x, gain, w_low, w_gate, b_gate, layer, *, bm):
    M, D = x.shape
    g_arr, g_idx = gain
    rank, dk = w_gate.shape[1], w_gate.shape[2]
    bm = _tile(M, bm, SUBLANES)
    return pl.pallas_call(
        _gla_gate_body,
        grid=(M // bm,),
        in_specs=[pl.BlockSpec((bm, D), lambda i: (i, 0)),
                  pl.BlockSpec((None, 1, D), lambda i: (g_idx, 0, 0)),
                  pl.BlockSpec((D, rank), lambda i: (0, 0)),
                  pl.BlockSpec((None, rank, dk), lambda i: (layer, 0, 0)),
                  pl.BlockSpec((None, 1, dk), lambda i: (layer, 0, 0))],
        out_specs=pl.BlockSpec((bm, dk), lambda i: (i, 0)),
        out_shape=jax.ShapeDtypeStruct((M, dk), F32),
        compiler_params=_cparams(1),
        name="gla_gate",
    )(x, g_arr, w_low, w_gate, b_gate)


def _cumsum_rows(x):
    rows = x.shape[0]
    row = lax.broadcasted_iota(jnp.int32, x.shape, 0)
    s = 1
    while s < rows:
        x = x + jnp.where(row >= s, pltpu.roll(x, s, axis=0), 0.0)
        s *= 2
    return x


def _block_refs(bcs, m, pairs):
    rows, dk = bcs.shape
    nb = rows // m
    ends = [bcs[(k + 1) * m - 1:(k + 1) * m, :] for k in range(nb)]
    parts = []
    for k in range(nb):
        if pairs:
            r = ends[k - 1] if k % 2 == 1 else ends[k]
        else:
            r = ends[k - 1] if k > 0 else jnp.zeros((1, dk), F32)
        parts.append(jnp.broadcast_to(r, (m, dk)))
    return parts[0] if nb == 1 else jnp.concatenate(parts, axis=0)


def _la_body(a_ref, b_ref, c_ref, gate_ref, e_ref, gn_ref, s0_ref, y_ref, so_ref, st_ref,
             *, hgrn2, heads, dk, dv, chunk, base, valid, q_scale):
    n = pl.program_id(1)

    @pl.when(n == 0)
    def _():
        for h in range(heads):
            st_ref[h] = s0_ref[h].T

    tq = lax.broadcasted_iota(jnp.int32, (chunk, chunk), 0)
    ts = lax.broadcasted_iota(jnp.int32, (chunk, chunk), 1)
    row = lax.broadcasted_iota(jnp.int32, (chunk, 1), 0)
    sh = int(math.log2(base))
    mask_diag = jnp.logical_and((tq >> sh) == (ts >> sh), ts <= tq)
    levels = []
    m = base
    while m < chunk:
        s1 = int(math.log2(m))
        mk = jnp.logical_and((tq >> (s1 + 1)) == (ts >> (s1 + 1)),
                             jnp.logical_and(((tq >> s1) & 1) == 1, ((ts >> s1) & 1) == 0))
        levels.append((m, mk, ((row >> s1) & 1) == 1))
        m *= 2
    gn = gn_ref[...]

    for h in range(heads):
        ksl = slice(h * dk, (h + 1) * dk)
        vsl = slice(h * dv, (h + 1) * dv)
        if hgrn2:
            a = a_ref[:, ksl]
            lb = e_ref[:, ksl]
            f = lb + (1.0 - lb) * jax.nn.sigmoid(b_ref[:, ksl])
            qa = a * jax.nn.sigmoid(a)
            kk = 1.0 - f
            g = jnp.log(f)
        else:
            qa = a_ref[:, ksl] * q_scale
            kk = b_ref[:, ksl]
            g = e_ref[:, ksl]
        v = c_ref[:, vsl]
        if valid < chunk:
            kk = jnp.where(row < valid, kk, 0.0)
            g = jnp.where(row < valid, g, 0.0)
        bcs = _cumsum_rows(g)
        vb = v.astype(BF16)
        st = st_ref[h]

        o = _dot_nt((qa * jnp.exp(bcs)).astype(BF16), st.astype(BF16))

        ref_d = _block_refs(bcs, base, False)
        a_mat = _dot_nt((qa * jnp.exp(bcs - ref_d)).astype(BF16), (kk * jnp.exp(ref_d - bcs)).astype(BF16))
        a_mat = jnp.where(mask_diag, a_mat, 0.0)
        for (m, mk, odd) in levels:
            ref_m = _block_refs(bcs, m, True)
            qm = qa * jnp.exp(jnp.where(odd, bcs - ref_m, 0.0))
            km = kk * jnp.exp(jnp.where(odd, 0.0, ref_m - bcs))
            a_mat = a_mat + jnp.where(mk, _dot_nt(qm.astype(BF16), km.astype(BF16)), 0.0)
        o = o + _dot(a_mat.astype(BF16), vb)

        b_last = bcs[chunk - 1:chunk, :]
        kd = (kk * jnp.exp(b_last - bcs)).astype(BF16)
        st_ref[h] = jnp.exp(b_last) * st + _dot_tn(vb, kd)

        gate = gate_ref[:, vsl]
        y_ref[:, vsl] = _rms(o, gn) * (gate * jax.nn.sigmoid(gate))

    @pl.when(n == pl.num_programs(1) - 1)
    def _():
        for h in range(heads):
            so_ref[h] = st_ref[h].T


def gated_linear_attention(proj, extra, gnorm, s0, *, hgrn2, chunk, valid=None):
    B, T, _ = proj.shape
    _, H, dk, dv = s0.shape
    chunk = min(chunk, T)
    assert T % chunk == 0
    base = min(LA_BASE, chunk)
    valid = chunk if valid is None else valid
    wk, wv = H * dk, H * dv

    def col(width, start):
        assert start % width == 0
        return pl.BlockSpec((None, chunk, width), lambda b, n: (b, n, start // width))

    if hgrn2:
        specs = [col(wk, 0), col(wk, wk), col(wv, 2 * wk), col(wv, 2 * wk + wv),
                 pl.BlockSpec((1, wk), lambda b, n: (0, 0))]
    else:
        specs = [col(wk, 0), col(wk, wk), col(wv, 2 * wk), col(wv, 2 * wk + wv),
                 pl.BlockSpec((None, chunk, wk), lambda b, n: (b, n, 0))]
    specs += [pl.BlockSpec((1, dv), lambda b, n: (0, 0)),
              pl.BlockSpec((None, H, dk, dv), lambda b, n: (b, 0, 0, 0))]
    return pl.pallas_call(
        functools.partial(_la_body, hgrn2=hgrn2, heads=H, dk=dk, dv=dv, chunk=chunk, base=base,
                          valid=valid, q_scale=1.0 if hgrn2 else dk ** -0.5),
        grid=(B, T // chunk),
        in_specs=specs,
        out_specs=[pl.BlockSpec((None, chunk, wv), lambda b, n: (b, n, 0)),
                   pl.BlockSpec((None, H, dk, dv), lambda b, n: (b, 0, 0, 0))],
        out_shape=[jax.ShapeDtypeStruct((B, T, wv), F32),
                   jax.ShapeDtypeStruct((B, H, dk, dv), F32)],
        scratch_shapes=[pltpu.VMEM((H, dv, dk), F32)],
        compiler_params=_cparams(2),
        name="gated_linear_attention",
    )(proj, proj, proj, proj, extra, gnorm, s0)


def _pad_rows(x, rows):
    return jnp.pad(x, ((0, 0), (0, rows - x.shape[1]), (0, 0)))


def _run_trunk(x, mem_kvs, a_caches, b_states, c_states, P, prompt):
    B, T, D = x.shape
    M = B * T
    depth = P["norm_mix"].shape[0]
    hd_a, hpg = P["a_hd"], P["a_hpg"]
    groups = len(A_DILATIONS)
    Wa = hpg * hd_a
    if prompt:
        tile = dict(bm=2048, bn=512)
        tile_res = dict(bm=1024, bn=512)
        mlp_tile = dict(bm=1024, bf=512)
    else:
        tile = tile_res = dict(bm=SUBLANES, bn=1024)
        mlp_tile = dict(bm=SUBLANES, bf=1024)

    h = x.reshape(M, D)
    new_a, new_b, new_c = [], [], []
    for li in range(depth):
        kind, j = li % N_MIXERS, li // N_MIXERS
        gmix = (P["norm_mix"], li)
        if kind == 0:
            qkv = linear(h, P["a_w_in"], j, gain=gmix, **tile)
            if prompt:
                qkv3 = qkv.reshape(B, T, 3 * groups * Wa)
                o = dilated_attention_prompt(qkv3, P["a_bias_prompt"], heads=hpg, hd=hd_a, tile=16 * A_BLOCK)
                nbufs = []
                for g in range(groups):
                    lw = min(A_WINDOWS[g], T)
                    kg = qkv3[:, T - lw:, (groups + g) * Wa:(groups + g + 1) * Wa].reshape(B, lw, hpg, hd_a)
                    vg = qkv3[:, T - lw:, (2 * groups + g) * Wa:(2 * groups + g + 1) * Wa].reshape(B, lw, hpg, hd_a)
                    nbufs.append(jnp.stack([kg, vg], axis=2))
                new_a.append(nbufs)
            else:
                assert T == 1
                qkv5 = qkv.reshape(B, 3, groups, hpg, hd_a)
                o = dilated_attention_sample(qkv5.reshape(B, 3 * groups, hpg, hd_a), a_caches, j,
                                             P["a_bias_sample"], heads=hpg, hd=hd_a, span=A_BLOCK)
                new_a.append([qkv5[:, 1:3, g][:, None] for g in range(groups)])
            h = linear(o.reshape(M, Wa), P["a_w_out"], j, res=h, **tile_res)
        elif kind == 1:
            proj = linear(h, P["b_w_in"], j, gain=gmix, **tile).reshape(B, T, -1)
            s0 = jnp.zeros((B,) + P["b_state_shape"], F32) if prompt else b_states[j]
            lb = P["lower_bounds"][li][None, :]
            if prompt:
                y, S = gated_linear_attention(proj, lb, P["b_gnorm"][j][None, :], s0, hgrn2=True, chunk=64)
            else:
                y, S = gated_linear_attention(_pad_rows(proj, SUBLANES), lb, P["b_gnorm"][j][None, :], s0,
                                              hgrn2=True, chunk=SUBLANES, valid=T)
                y = y[:, :T]
            h = linear(y.reshape(M, -1), P["b_w_out"], j, res=h, **tile_res)
            new_b.append(S)
        else:
            wk = P["c_wk"]
            wv = P["c_state_shape"][0] * P["c_state_shape"][2]
            n_main = 2 * wk + 2 * wv
            proj = linear(h, P["c_w_in"], j, n_out=n_main, gain=gmix, **tile).reshape(B, T, n_main)
            log_g = gla_gate(h, gmix, P["c_w_low"][j], P["c_w_gate"], P["c_b_gate"], j,
                             bm=1024).reshape(B, T, wk)
            s0 = jnp.zeros((B,) + P["c_state_shape"], F32) if prompt else c_states[j]
            if prompt:
                y, S = gated_linear_attention(proj, log_g, P["c_gnorm"][j][None, :], s0, hgrn2=False, chunk=64)
            else:
                y, S = gated_linear_attention(_pad_rows(proj, SUBLANES), _pad_rows(log_g, SUBLANES),
                                              P["c_gnorm"][j][None, :], s0, hgrn2=False, chunk=SUBLANES, valid=T)
                y = y[:, :T]
            h = linear(y.reshape(M, -1), P["c_w_out"], j, res=h, **tile_res)
            new_c.append(S)

        if prompt:
            h = mem_attend(h.reshape(B, T, D), P["norm_mem"], P["mem_w_q"], mem_kvs, P["mem_w_out"], li,
                           heads=P["m_heads"], hd=P["m_hd"], bm=512).reshape(M, D)
        else:
            hp = _pad_rows(h.reshape(B, T, D), SUBLANES)
            hp = mem_attend(hp, P["norm_mem"], P["mem_w_q"], mem_kvs, P["mem_w_out"], li,
                            heads=P["m_heads"], hd=P["m_hd"], bm=SUBLANES)
            h = hp[:, :T].reshape(M, D)
        final = P["norm_final"] if li == depth - 1 else None
        h = mlp(h, P["norm_mlp"], P["mlp_w_up"], P["mlp_w_down"], li, final_gain=final, **mlp_tile)

    y = h.reshape(B, T, D)
    win = [jnp.stack([nb[g] for nb in new_a], axis=0) for g in range(groups)]
    return y, win, jnp.stack(new_b, axis=0), jnp.stack(new_c, axis=0)


def _bias_tables(rel_bias_table, hpg, span):
    steps = np.arange(A_BLOCK)[:, None] + A_BLOCK - np.arange(2 * A_BLOCK)[None, :]
    eye = np.eye(N_BUCKETS, dtype=np.float32)
    prompt, sample = [], []
    for g, d in enumerate(A_DILATIONS):
        tg_t = rel_bias_table[:, g * hpg:(g + 1) * hpg].T
        onehot = eye[:, t5_bucket(np.clip(steps, 0, span) * d).reshape(-1)]
        bias = jnp.dot(tg_t, jnp.asarray(onehot), precision=lax.Precision.HIGHEST).reshape(hpg, A_BLOCK, 2 * A_BLOCK)
        band = jnp.asarray((steps >= 0) & (steps <= span))
        prompt.append(jnp.where(band[None], bias, -jnp.inf))
        dist = np.concatenate([(span - np.arange(span)) * d, np.zeros(1, np.int64)])
        row = jnp.dot(tg_t, jnp.asarray(eye[:, t5_bucket(dist)]), precision=lax.Precision.HIGHEST)
        sample.append(jnp.broadcast_to(row.T[:, :, None], (span + 1, hpg, LANES)))
    return jnp.stack(prompt, axis=0), jnp.stack(sample, axis=0)


def kernel(x_prompt, x_sample, cache_win0, cache_win1, cache_win2, state_hgrn, state_gla, cache_mem, mem_prompt, rel_bias_table, a_w_in, a_w_out, b_w_in, b_w_out, b_lower_bound, b_gnorm, c_w_in, c_w_gate, c_b_gate, c_w_out, c_gnorm, mem_norm, mem_w_kv, mem_w_q, mem_w_out, norm_mix, norm_mem, norm_mlp, mlp_w_up, mlp_w_down, norm_final):
    depth, D = norm_mix.shape
    hpg, hd_a = cache_win0.shape[-2], cache_win0.shape[-1]
    span = A_BLOCK
    assert all(w // d == span for w, d in zip(A_WINDOWS, A_DILATIONS))
    m_heads, m_hd = cache_mem.shape[-2], cache_mem.shape[-1]
    c_heads, c_dk, c_dv = state_gla.shape[2:]
    wk = c_heads * c_dk
    n_main = 2 * wk + 2 * c_heads * c_dv
    bias_prompt, bias_sample = _bias_tables(rel_bias_table, hpg, span)

    sm = jax.nn.softmax(b_lower_bound.astype(F32), axis=0)
    P = dict(
        a_hd=hd_a, a_hpg=hpg, m_heads=m_heads, m_hd=m_hd,
        a_bias_prompt=bias_prompt, a_bias_sample=bias_sample,
        a_w_in=a_w_in, a_w_out=a_w_out, b_w_in=b_w_in, b_w_out=b_w_out,
        lower_bounds=jnp.cumsum(sm, axis=0) - sm, b_gnorm=b_gnorm, b_state_shape=state_hgrn.shape[2:],
        c_w_in=c_w_in, c_w_low=c_w_in[:, :, n_main:], c_w_gate=c_w_gate,
        c_b_gate=c_b_gate[:, None, :], c_w_out=c_w_out, c_gnorm=c_gnorm, c_state_shape=state_gla.shape[2:],
        c_wk=wk,
        mem_w_q=mem_w_q, mem_w_out=mem_w_out,
        norm_mix=norm_mix[:, None, :], norm_mem=norm_mem[:, None, :], norm_mlp=norm_mlp[:, None, :],
        mlp_w_up=mlp_w_up, mlp_w_down=mlp_w_down, norm_final=norm_final[None, :],
    )

    Bm, Mt, _ = mem_prompt.shape
    mem_flat = mem_prompt.reshape(Bm * Mt, D)
    mem_gain = mem_norm[:, None, :]
    p_mem = jnp.stack([linear(mem_flat, mem_w_kv, li, gain=(mem_gain, li), bm=512, bn=512)
                       for li in range(depth)], axis=0).reshape(depth, Bm, Mt, 2 * m_heads * m_hd)

    caches = (cache_win0, cache_win1, cache_win2)
    y_prompt, p_win, p_hgrn, p_gla = _run_trunk(x_prompt, p_mem, None, None, None, P, True)
    y_sample, new_rows, s_hgrn, s_gla = _run_trunk(
        x_sample, cache_mem.reshape(cache_mem.shape[:3] + (-1,)), caches, state_hgrn, state_gla, P, False)
    s_win = [window_update(c, n, chunk=256) for c, n in zip(caches, new_rows)]
    return (y_prompt, y_sample, p_win[0], p_win[1], p_win[2], p_hgrn, p_gla,
            p_mem.reshape(depth, Bm, Mt, 2, m_heads, m_hd),
            s_win[0], s_win[1], s_win[2], s_hgrn, s_gla)
```

```python
import functools
import math

import numpy as np
import jax
import jax.numpy as jnp
from jax import lax
from jax.experimental import pallas as pl
from jax.experimental.pallas import tpu as pltpu

F32 = jnp.float32
BF16 = jnp.bfloat16

EPS = 1e-6
N_MIXERS = 3
A_WINDOWS = (128, 512, 2048)
A_DILATIONS = (1, 4, 16)
A_BLOCK = 128
N_BUCKETS = 32
MAX_DISTANCE = 2048
GLA_GATE_NORMALIZER = 16.0
LA_BASE = 16
LA_HEAD_BATCH_LANES = 512
LA_CHUNK = 128

V7X_VMEM_BYTES = 64 * 1024 * 1024
VMEM_LIMIT = V7X_VMEM_BYTES - 8 * 1024 * 1024
SUBLANES = 8
LANES = 128


def _cparams(n_axes):
    return pltpu.CompilerParams(dimension_semantics=("arbitrary",) * n_axes,
                                vmem_limit_bytes=VMEM_LIMIT)


def _tile(n, target, align):
    if n <= target:
        return n
    t = (target // align) * align
    while n % t:
        t -= align
    return t


def _rms(x, g):
    r = lax.rsqrt(jnp.mean(x * x, axis=-1, keepdims=True) + EPS)
    return (x * r) * g


def _dot(a, b):
    return jnp.dot(a, b, preferred_element_type=F32)


def _dot_nt(a, b):
    return lax.dot_general(a, b, (((1,), (1,)), ((), ())), preferred_element_type=F32)


def _dot_tn(a, b):
    return lax.dot_general(a, b, (((0,), (0,)), ((), ())), preferred_element_type=F32)


def _bf_round(a):
    return a.astype(BF16).astype(F32)


def _linear_body(x_ref, w_ref, g_ref, o_ref, xs_ref):
    @pl.when(pl.program_id(1) == 0)
    def _():
        xs_ref[...] = _rms(x_ref[...], g_ref[...]).astype(BF16)

    o_ref[...] = _dot(xs_ref[...], w_ref[...].astype(BF16))


def linear(x, w, layer, *, n_out=None, gain, bm, bn):
    M, K = x.shape
    N = w.shape[-1] if n_out is None else n_out
    bm, bn = _tile(M, bm, SUBLANES), _tile(N, bn, LANES)
    in_specs = [pl.BlockSpec((bm, K), lambda i, j: (i, 0), pipeline_mode=pl.Buffered(1)),
                pl.BlockSpec((None, K, bn), lambda i, j: (layer, 0, j))]
    g_arr, g_idx = gain
    in_specs.append(pl.BlockSpec((None, 1, K), lambda i, j: (g_idx, 0, 0)))
    args = [x, w, g_arr]
    return pl.pallas_call(
        _linear_body,
        grid=(M // bm, N // bn),
        in_specs=in_specs,
        out_specs=pl.BlockSpec((bm, bn), lambda i, j: (i, j)),
        out_shape=jax.ShapeDtypeStruct((M, N), F32),
        scratch_shapes=[pltpu.VMEM((bm, K), BF16)],
        compiler_params=_cparams(2),
        name="linear",
    )(*args)


def _mlp_body(*refs, has_final):
    if has_final:
        x_ref, g_ref, wu_ref, wd_ref, fg_ref, o_ref, xs_ref = refs
    else:
        x_ref, g_ref, wu_ref, wd_ref, o_ref, xs_ref = refs
    f = pl.program_id(1)

    @pl.when(f == 0)
    def _():
        x = x_ref[...]
        xs_ref[...] = _rms(x, g_ref[...]).astype(BF16)
        o_ref[...] = x

    hdn = _dot(xs_ref[...], wu_ref[...].astype(BF16))
    hdn = jnp.square(jnp.maximum(hdn, 0.0)).astype(BF16)
    o_ref[...] += _dot(hdn, wd_ref[...].astype(BF16))

    if has_final:
        @pl.when(f == pl.num_programs(1) - 1)
        def _():
            o_ref[...] = _rms(o_ref[...], fg_ref[...])


def mlp(x, gains, w_up, w_down, layer, *, final_gain=None, bm, bf):
    M, D = x.shape
    FF = w_up.shape[-1]
    bm, bf = _tile(M, bm, SUBLANES), _tile(FF, bf, LANES)
    in_specs = [pl.BlockSpec((bm, D), lambda i, f: (i, 0), pipeline_mode=pl.Buffered(1)),
                pl.BlockSpec((None, 1, D), lambda i, f: (layer, 0, 0)),
                pl.BlockSpec((None, D, bf), lambda i, f: (layer, 0, f)),
                pl.BlockSpec((None, bf, D), lambda i, f: (layer, f, 0))]
    args = [x, gains, w_up, w_down]
    if final_gain is not None:
        in_specs.append(pl.BlockSpec((1, D), lambda i, f: (0, 0)))
        args.append(final_gain)
    return pl.pallas_call(
        functools.partial(_mlp_body, has_final=final_gain is not None),
        grid=(M // bm, FF // bf),
        in_specs=in_specs,
        out_specs=pl.BlockSpec((bm, D), lambda i, f: (i, 0), pipeline_mode=pl.Buffered(1)),
        out_shape=jax.ShapeDtypeStruct((M, D), F32),
        scratch_shapes=[pltpu.VMEM((bm, D), BF16)],
        compiler_params=_cparams(2),
        name="mlp",
    )(*args)


def _mem_body(y_ref, wm_ref, h_ref, g_ref, wq_ref, kv_ref, wo_ref, o_ref, *, heads, hd):
    x = h_ref[...] + _dot(y_ref[...].astype(BF16), wm_ref[...])
    xn = _rms(x, g_ref[...]).astype(BF16)
    q = _dot(xn, wq_ref[...])
    width = heads * hd
    scale = hd ** -0.5
    outs = []
    for h in range(heads):
        qh = q[:, h * hd:(h + 1) * hd].astype(BF16)
        kh = kv_ref[:, h * hd:(h + 1) * hd].astype(BF16)
        vh = kv_ref[:, width + h * hd:width + (h + 1) * hd].astype(BF16)
        s = _dot_nt(qh, kh) * scale
        p = jnp.exp(s - jnp.max(s, axis=-1, keepdims=True))
        p = p / jnp.sum(p, axis=-1, keepdims=True)
        outs.append(_dot(p.astype(BF16), vh))
    o = jnp.concatenate(outs, axis=-1).astype(BF16)
    o_ref[...] = x + _dot(o, wo_ref[...])


def mix_out_mem_attend(y, w_mix, mix_layer, h, gains, w_q, kv, w_o, layer, *, heads, hd, bm):
    B, T, D = h.shape
    Ky = y.shape[-1]
    Mt = kv.shape[2]
    width = heads * hd
    bm = _tile(T, bm, SUBLANES)
    return pl.pallas_call(
        functools.partial(_mem_body, heads=heads, hd=hd),
        grid=(B, T // bm),
        in_specs=[pl.BlockSpec((None, bm, Ky), lambda b, i: (b, i, 0)),
                  pl.BlockSpec((None, Ky, D), lambda b, i: (mix_layer, 0, 0), pipeline_mode=pl.Buffered(1)),
                  pl.BlockSpec((None, bm, D), lambda b, i: (b, i, 0)),
                  pl.BlockSpec((None, 1, D), lambda b, i: (layer, 0, 0)),
                  pl.BlockSpec((None, D, width), lambda b, i: (layer, 0, 0), pipeline_mode=pl.Buffered(1)),
                  pl.BlockSpec((None, None, Mt, 2 * width), lambda b, i: (layer, b, 0, 0)),
                  pl.BlockSpec((None, width, D), lambda b, i: (layer, 0, 0), pipeline_mode=pl.Buffered(1))],
        out_specs=pl.BlockSpec((None, bm, D), lambda b, i: (b, i, 0)),
        out_shape=jax.ShapeDtypeStruct((B, T, D), F32),
        compiler_params=_cparams(2),
        name="mix_out_mem_attend",
    )(y, w_mix, h, gains, w_q, kv, w_o)


def t5_bucket(dist):
    dist = np.asarray(dist, np.int64)
    max_exact = N_BUCKETS // 2
    d = np.maximum(dist, 1).astype(np.float64)
    large = max_exact + (np.log(d / max_exact) / math.log(MAX_DISTANCE / max_exact)
                         * (N_BUCKETS - max_exact)).astype(np.int64)
    return np.where(dist < max_exact, dist, np.minimum(large, N_BUCKETS - 1)).astype(np.int32)


ATTN_UNROLL = 4


def _unroll(trip):
    return max(u for u in range(1, ATTN_UNROLL + 2) if trip % u == 0)


def _attn_block(q, k2, v2, bias2, scale):
    blk = q.shape[0]
    s = _dot_nt(q.astype(BF16), k2.astype(BF16)) * scale + bias2
    m = jnp.max(jnp.maximum(s[:, :blk], s[:, blk:]), axis=-1, keepdims=True)
    p = jnp.exp(s - m)
    l = jnp.sum(p[:, :blk] + p[:, blk:], axis=-1, keepdims=True)
    o = _dot(p.astype(BF16), v2.astype(BF16))
    return o / l, m + jnp.log(l)


def _dil_attn_body(*refs, dils, tile, hd):
    n_groups = len(dils)
    bias_ref, o_ref, og_ref, lse_ref = refs[5 * n_groups:]
    not_first_tile = pl.program_id(1) > 0
    blk = A_BLOCK
    scale = hd ** -0.5

    for g, d in enumerate(dils):
        q_ref, kc_ref, vc_ref, kp_ref, vp_ref = refs[5 * g:5 * g + 5]
        stream_rows = blk * d
        n_blocks = tile // stream_rows

        def rows(start, d=d):
            if d == 1:
                return pl.ds(pl.multiple_of(start, blk), blk)
            return pl.ds(start, blk, stride=d)

        bias2 = bias_ref[g]
        bias2_first = jnp.concatenate(
            [jnp.where(not_first_tile, bias2[:, :blk], -jnp.inf), bias2[:, blk:]], axis=1)

        def run(starts, prev_of, bias, g=g, q_ref=q_ref, kc_ref=kc_ref, vc_ref=vc_ref, rows=rows):
            results = []
            for s in starts:
                kp, vp = prev_of(s)
                k2 = jnp.concatenate([kp, kc_ref[rows(s), :]], axis=0)
                v2 = jnp.concatenate([vp, vc_ref[rows(s), :]], axis=0)
                results.append(_attn_block(q_ref[rows(s), :], k2, v2, bias, scale))
            for s, (o, lse) in zip(starts, results):
                og_ref[g, rows(s), :] = o
                lse_ref[g, rows(s), :] = jnp.broadcast_to(lse, (blk, hd))

        def first(it, carry, run=run, rows=rows, kp_ref=kp_ref, vp_ref=vp_ref, u=_unroll(d)):
            run([it * u + k for k in range(u)], lambda s: (kp_ref[rows(s), :], vp_ref[rows(s), :]), bias2_first)
            return carry

        n_rest = d * (n_blocks - 1)

        def rest(it, carry, run=run, rows=rows, kc_ref=kc_ref, vc_ref=vc_ref, d=d, stream_rows=stream_rows,
                 u=_unroll(n_rest)):
            idx = [it * u + k for k in range(u)]
            run([i % d + (1 + i // d) * stream_rows for i in idx],
                lambda s: (kc_ref[rows(s - stream_rows), :], vc_ref[rows(s - stream_rows), :]), bias2)
            return carry

        lax.fori_loop(0, d // _unroll(d), first, 0)
        if n_blocks > 1:
            lax.fori_loop(0, n_rest // _unroll(n_rest), rest, 0)

    lses = [lse_ref[g] for g in range(n_groups)]
    mx = functools.reduce(jnp.maximum, lses)
    es = [jnp.exp(l - mx) for l in lses]
    den = functools.reduce(lambda a, b: a + b, es)
    acc = (es[0] / den) * og_ref[0]
    for g in range(1, n_groups):
        acc = acc + (es[g] / den) * og_ref[g]
    o_ref[...] = acc


def dilated_attention_prompt(qkv, bias, *, heads, hd, tile):
    B, T, _ = qkv.shape
    dils = A_DILATIONS
    n_groups = len(dils)
    tile = min(tile, T)
    assert hd == LANES and T % tile == 0 and all(tile % (A_BLOCK * d) == 0 for d in dils)
    sec = n_groups * heads

    def cur(which, g):
        return pl.BlockSpec((None, tile, hd), lambda b, n, h: (b, n, which * sec + g * heads + h))

    def prev(which, g):
        rows = A_BLOCK * dils[g]
        per = tile // rows
        return pl.BlockSpec((None, rows, hd),
                            lambda b, n, h: (b, jnp.maximum(n * per - 1, 0), which * sec + g * heads + h))

    in_specs = []
    for g in range(n_groups):
        in_specs += [cur(0, g), cur(1, g), cur(2, g), prev(1, g), prev(2, g)]
    in_specs.append(pl.BlockSpec((n_groups, None, A_BLOCK, 2 * A_BLOCK), lambda b, n, h: (0, h, 0, 0)))
    return pl.pallas_call(
        functools.partial(_dil_attn_body, dils=dils, tile=tile, hd=hd),
        grid=(B, T // tile, heads),
        in_specs=in_specs,
        out_specs=pl.BlockSpec((None, tile, hd), lambda b, n, h: (b, n, h)),
        out_shape=jax.ShapeDtypeStruct((B, T, heads * hd), F32),
        scratch_shapes=[pltpu.VMEM((n_groups, tile, hd), F32), pltpu.VMEM((n_groups, tile, hd), F32)],
        compiler_params=_cparams(3),
        name="dilated_attention_prompt",
    )(*([qkv] * (5 * n_groups)), bias)


def _dec_attn_body(qkv_ref, *refs, groups, span, scale):
    bufs, bias_ref, o_ref = refs[:groups], refs[groups], refs[groups + 1]
    outs, lses = [], []
    for g in range(groups):
        q = _bf_round(qkv_ref[g])
        kn = _bf_round(qkv_ref[groups + g])
        vn = _bf_round(qkv_ref[2 * groups + g])
        k = _bf_round(bufs[g][:, 0])
        v = _bf_round(bufs[g][:, 1])
        s = jnp.sum(k * q[None], axis=-1, keepdims=True) * scale + bias_ref[g, :span]
        s_self = jnp.sum(q * kn, axis=-1, keepdims=True) * scale + bias_ref[g, span]
        m = jnp.maximum(jnp.max(s, axis=0), s_self)
        p = jnp.exp(s - m[None])
        p_self = jnp.exp(s_self - m)
        l = jnp.sum(p, axis=0) + p_self
        outs.append(jnp.sum(_bf_round(p / l[None]) * v, axis=0) + _bf_round(p_self / l) * vn)
        lses.append(m + jnp.log(l))
    mx = functools.reduce(jnp.maximum, lses)
    es = [jnp.exp(l - mx) for l in lses]
    den = functools.reduce(lambda a, b: a + b, es)
    acc = (es[0] / den) * outs[0]
    for g in range(1, groups):
        acc = acc + (es[g] / den) * outs[g]
    o_ref[...] = acc


def dilated_attention_sample(qkv, caches, layer, bias, *, heads, hd, span):
    B = qkv.shape[0]
    groups = len(caches)
    in_specs = [pl.BlockSpec((None, 3 * groups, heads, hd), lambda b: (b, 0, 0, 0))]
    views = []
    for g, c in enumerate(caches):
        d = A_DILATIONS[g]
        assert c.shape[2] == span * d
        views.append(c.reshape(c.shape[0], B, span, d, 2, heads, hd))
        in_specs.append(pl.BlockSpec((None, None, span, None, 2, heads, hd),
                                     lambda b: (layer, b, 0, 0, 0, 0, 0)))
    in_specs.append(pl.BlockSpec((groups, span + 1, heads, hd), lambda b: (0, 0, 0, 0)))
    return pl.pallas_call(
        functools.partial(_dec_attn_body, groups=groups, span=span, scale=hd ** -0.5),
        grid=(B,),
        in_specs=in_specs,
        out_specs=pl.BlockSpec((None, heads, hd), lambda b: (b, 0, 0)),
        out_shape=jax.ShapeDtypeStruct((B, heads, hd), F32),
        compiler_params=_cparams(1),
        name="dilated_attention_sample",
    )(qkv, *views, bias)


def _window_update_body(main_ref, next_ref, new_ref, o_ref):
    rows = main_ref.shape[0]
    last_chunk = pl.program_id(2) == pl.num_programs(2) - 1
    o_ref[0:rows - 1] = main_ref[1:rows]
    o_ref[rows - 1:rows] = jnp.where(last_chunk, new_ref[...], next_ref[...])


def window_update(cache, new, *, chunk):
    n_layers, B, rows = cache.shape[:3]
    tail = cache.shape[3:]
    chunk = _tile(rows, chunk, 1)
    zeros = (0,) * len(tail)
    return pl.pallas_call(
        _window_update_body,
        grid=(n_layers, B, rows // chunk),
        in_specs=[pl.BlockSpec((None, None, chunk) + tail, lambda j, b, c: (j, b, c) + zeros),
                  pl.BlockSpec((None, None, 1) + tail,
                               lambda j, b, c: (j, b, jnp.minimum((c + 1) * chunk, rows - 1)) + zeros),
                  pl.BlockSpec((None, None, 1) + tail, lambda j, b, c: (j, b, 0) + zeros)],
        out_specs=pl.BlockSpec((None, None, chunk) + tail, lambda j, b, c: (j, b, c) + zeros),
        out_shape=jax.ShapeDtypeStruct(cache.shape, cache.dtype),
        compiler_params=_cparams(3),
        name="window_update",
    )(cache, cache, new)


def _gla_gate_body(x_ref, g_ref, wl_ref, wg_ref, b_ref, o_ref):
    xn = _rms(x_ref[...], g_ref[...]).astype(BF16)
    low = _dot(xn, wl_ref[...].astype(BF16))
    z = _dot(low.astype(BF16), wg_ref[...].astype(BF16)) + b_ref[...]
    log_sig = jnp.minimum(z, 0.0) - jnp.log(1.0 + jnp.exp(-jnp.abs(z)))
    o_ref[...] = log_sig / GLA_GATE_NORMALIZER


def gla_gate(x, gain, w_low, w_gate, b_gate, layer, *, bm):
    M, D = x.shape
    g_arr, g_idx = gain
    rank, dk = w_gate.shape[1], w_gate.shape[2]
    bm = _tile(M, bm, SUBLANES)
    return pl.pallas_call(
        _gla_gate_body,
        grid=(M // bm,),
        in_specs=[pl.BlockSpec((bm, D), lambda i: (i, 0)),
                  pl.BlockSpec((None, 1, D), lambda i: (g_idx, 0, 0)),
                  pl.BlockSpec((D, rank), lambda i: (0, 0)),
                  pl.BlockSpec((None, rank, dk), lambda i: (layer, 0, 0)),
                  pl.BlockSpec((None, 1, dk), lambda i: (layer, 0, 0))],
        out_specs=pl.BlockSpec((bm, dk), lambda i: (i, 0)),
        out_shape=jax.ShapeDtypeStruct((M, dk), F32),
        compiler_params=_cparams(1),
        name="gla_gate",
    )(x, g_arr, w_low, w_gate, b_gate)


def _cumsum_rows(x):
    rows = x.shape[0]
    row = lax.broadcasted_iota(jnp.int32, x.shape, 0)
    s = 1
    while s < rows:
        x = x + jnp.where(row >= s, pltpu.roll(x, s, axis=0), 0.0)
        s *= 2
    return x


def _block_refs(bcs, m, pairs):
    rows, dk = bcs.shape
    nb = rows // m
    ends = [bcs[(k + 1) * m - 1:(k + 1) * m, :] for k in range(nb)]
    parts = []
    for k in range(nb):
        if pairs:
            r = ends[k - 1] if k % 2 == 1 else ends[k]
        else:
            r = ends[k - 1] if k > 0 else jnp.zeros((1, dk), F32)
        parts.append(jnp.broadcast_to(r, (m, dk)))
    return parts[0] if nb == 1 else jnp.concatenate(parts, axis=0)


def _la_body(a_ref, b_ref, c_ref, gate_ref, e_ref, gn_ref, s0_ref, y_ref, so_ref, st_ref,
             *, hgrn2, heads, dk, dv, chunk, base, valid, q_scale, head_batch):
    n = pl.program_id(1)

    @pl.when(n == 0)
    def _():
        for h in range(heads):
            st_ref[h] = s0_ref[h].T

    tq = lax.broadcasted_iota(jnp.int32, (chunk, chunk), 0)
    ts = lax.broadcasted_iota(jnp.int32, (chunk, chunk), 1)
    row = lax.broadcasted_iota(jnp.int32, (chunk, 1), 0)
    sh = int(math.log2(base))
    mask_diag = jnp.logical_and((tq >> sh) == (ts >> sh), ts <= tq)
    levels = []
    m = base
    while m < chunk:
        s1 = int(math.log2(m))
        mk = jnp.logical_and((tq >> (s1 + 1)) == (ts >> (s1 + 1)),
                             jnp.logical_and(((tq >> s1) & 1) == 1, ((ts >> s1) & 1) == 0))
        levels.append((m, mk))
        m *= 2
    gn = gn_ref[...]

    def one_head(h):
        ksl = slice(h * dk, (h + 1) * dk)
        vsl = slice(h * dv, (h + 1) * dv)
        if hgrn2:
            a = a_ref[:, ksl]
            lb = e_ref[:, ksl]
            f = lb + (1.0 - lb) * jax.nn.sigmoid(b_ref[:, ksl])
            qa = a * jax.nn.sigmoid(a)
            kk = 1.0 - f
            g = jnp.log(f)
        else:
            qa = a_ref[:, ksl] * q_scale
            kk = b_ref[:, ksl]
            g = e_ref[:, ksl]
        v = c_ref[:, vsl]
        if valid < chunk:
            kk = jnp.where(row < valid, kk, 0.0)
            g = jnp.where(row < valid, g, 0.0)
        bcs = _cumsum_rows(g)
        vb = v.astype(BF16)
        st = st_ref[h]

        o = _dot_nt((qa * jnp.exp(bcs)).astype(BF16), st.astype(BF16))

        ref_d = _block_refs(bcs, base, False)
        a_mat = _dot_nt((qa * jnp.exp(bcs - ref_d)).astype(BF16), (kk * jnp.exp(ref_d - bcs)).astype(BF16))
        a_mat = jnp.where(mask_diag, a_mat, 0.0)
        for (m, mk) in levels:
            dec = jnp.exp(-jnp.abs(bcs - _block_refs(bcs, m, True)))
            a_mat = a_mat + jnp.where(mk, _dot_nt((qa * dec).astype(BF16), (kk * dec).astype(BF16)), 0.0)
        o = o + _dot(a_mat.astype(BF16), vb)

        b_last = bcs[chunk - 1:chunk, :]
        kd = (kk * jnp.exp(b_last - bcs)).astype(BF16)
        st_new = jnp.exp(b_last) * st + _dot_tn(vb, kd)

        gate = gate_ref[:, vsl]
        return st_new, _rms(o, gn) * (gate * jax.nn.sigmoid(gate))

    for h0 in range(0, heads, head_batch):
        hs = range(h0, min(h0 + head_batch, heads))
        results = [one_head(h) for h in hs]
        for h, (st_new, y) in zip(hs, results):
            st_ref[h] = st_new
            y_ref[:, h * dv:(h + 1) * dv] = y

    @pl.when(n == pl.num_programs(1) - 1)
    def _():
        for h in range(heads):
            so_ref[h] = st_ref[h].T


def gated_linear_attention(proj, extra, gnorm, s0, *, hgrn2, chunk, valid=None):
    B, T, _ = proj.shape
    _, H, dk, dv = s0.shape
    chunk = min(chunk, T)
    assert T % chunk == 0
    base = min(LA_BASE, chunk)
    valid = chunk if valid is None else valid
    wk, wv = H * dk, H * dv

    def col(width, start):
        assert start % width == 0
        return pl.BlockSpec((None, chunk, width), lambda b, n: (b, n, start // width))

    if hgrn2:
        specs = [col(wk, 0), col(wk, wk), col(wv, 2 * wk), col(wv, 2 * wk + wv),
                 pl.BlockSpec((1, wk), lambda b, n: (0, 0))]
    else:
        specs = [col(wk, 0), col(wk, wk), col(wv, 2 * wk), col(wv, 2 * wk + wv),
                 pl.BlockSpec((None, chunk, wk), lambda b, n: (b, n, 0))]
    specs += [pl.BlockSpec((1, dv), lambda b, n: (0, 0)),
              pl.BlockSpec((None, H, dk, dv), lambda b, n: (b, 0, 0, 0))]
    return pl.pallas_call(
        functools.partial(_la_body, hgrn2=hgrn2, heads=H, dk=dk, dv=dv, chunk=chunk, base=base,
                          valid=valid, q_scale=1.0 if hgrn2 else dk ** -0.5,
                          head_batch=max(1, LA_HEAD_BATCH_LANES // max(dk, dv))),
        grid=(B, T // chunk),
        in_specs=specs,
        out_specs=[pl.BlockSpec((None, chunk, wv), lambda b, n: (b, n, 0)),
                   pl.BlockSpec((None, H, dk, dv), lambda b, n: (b, 0, 0, 0))],
        out_shape=[jax.ShapeDtypeStruct((B, T, wv), F32),
                   jax.ShapeDtypeStruct((B, H, dk, dv), F32)],
        scratch_shapes=[pltpu.VMEM((H, dv, dk), F32)],
        compiler_params=_cparams(2),
        name="gated_linear_attention",
    )(proj, proj, proj, proj, extra, gnorm, s0)


def _pad_rows(x, rows):
    return jnp.pad(x, ((0, 0), (0, rows - x.shape[1]), (0, 0)))


def _run_trunk(x, mem_kvs, a_caches, b_states, c_states, P, prompt):
    B, T, D = x.shape
    M = B * T
    depth = P["norm_mix"].shape[0]
    hd_a, hpg = P["a_hd"], P["a_hpg"]
    groups = len(A_DILATIONS)
    Wa = hpg * hd_a
    if prompt:
        tile = dict(bm=2048, bn=512)
        mlp_tile = dict(bm=1024, bf=512)
    else:
        tile = dict(bm=SUBLANES, bn=1024)
        mlp_tile = dict(bm=SUBLANES, bf=1024)

    h = x.reshape(M, D)
    new_a, new_b, new_c = [], [], []
    for li in range(depth):
        kind, j = li % N_MIXERS, li // N_MIXERS
        gmix = (P["norm_mix"], li)
        if kind == 0:
            qkv = linear(h, P["a_w_in"], j, gain=gmix, **tile)
            if prompt:
                qkv3 = qkv.reshape(B, T, 3 * groups * Wa)
                o = dilated_attention_prompt(qkv3, P["a_bias_prompt"], heads=hpg, hd=hd_a, tile=16 * A_BLOCK)
                nbufs = []
                for g in range(groups):
                    lw = min(A_WINDOWS[g], T)
                    kg = qkv3[:, T - lw:, (groups + g) * Wa:(groups + g + 1) * Wa].reshape(B, lw, hpg, hd_a)
                    vg = qkv3[:, T - lw:, (2 * groups + g) * Wa:(2 * groups + g + 1) * Wa].reshape(B, lw, hpg, hd_a)
                    nbufs.append(jnp.stack([kg, vg], axis=2))
                new_a.append(nbufs)
            else:
                assert T == 1
                qkv5 = qkv.reshape(B, 3, groups, hpg, hd_a)
                o = dilated_attention_sample(qkv5.reshape(B, 3 * groups, hpg, hd_a), a_caches, j,
                                             P["a_bias_sample"], heads=hpg, hd=hd_a, span=A_BLOCK)
                new_a.append([qkv5[:, 1:3, g][:, None] for g in range(groups)])
            y_mix, w_mix = o.reshape(B, T, Wa), P["a_w_out"]
        elif kind == 1:
            proj = linear(h, P["b_w_in"], j, gain=gmix, **tile).reshape(B, T, -1)
            s0 = jnp.zeros((B,) + P["b_state_shape"], F32) if prompt else b_states[j]
            lb = P["lower_bounds"][li][None, :]
            if prompt:
                y, S = gated_linear_attention(proj, lb, P["b_gnorm"][j][None, :], s0, hgrn2=True, chunk=LA_CHUNK)
            else:
                y, S = gated_linear_attention(_pad_rows(proj, SUBLANES), lb, P["b_gnorm"][j][None, :], s0,
                                              hgrn2=True, chunk=SUBLANES, valid=T)
            y_mix, w_mix = y, P["b_w_out"]
            new_b.append(S)
        else:
            wk = P["c_wk"]
            wv = P["c_state_shape"][0] * P["c_state_shape"][2]
            n_main = 2 * wk + 2 * wv
            proj = linear(h, P["c_w_in"], j, n_out=n_main, gain=gmix, **tile).reshape(B, T, n_main)
            log_g = gla_gate(h, gmix, P["c_w_low"][j], P["c_w_gate"], P["c_b_gate"], j,
                             bm=1024).reshape(B, T, wk)
            s0 = jnp.zeros((B,) + P["c_state_shape"], F32) if prompt else c_states[j]
            if prompt:
                y, S = gated_linear_attention(proj, log_g, P["c_gnorm"][j][None, :], s0, hgrn2=False,
                                              chunk=LA_CHUNK)
            else:
                y, S = gated_linear_attention(_pad_rows(proj, SUBLANES), _pad_rows(log_g, SUBLANES),
                                              P["c_gnorm"][j][None, :], s0, hgrn2=False, chunk=SUBLANES, valid=T)
            y_mix, w_mix = y, P["c_w_out"]
            new_c.append(S)

        mem_args = (P["norm_mem"], P["mem_w_q"], mem_kvs, P["mem_w_out"], li)
        if prompt:
            h = mix_out_mem_attend(y_mix, w_mix, j, h.reshape(B, T, D), *mem_args,
                                   heads=P["m_heads"], hd=P["m_hd"], bm=512).reshape(M, D)
        else:
            if y_mix.shape[1] < SUBLANES:
                y_mix = _pad_rows(y_mix, SUBLANES)
            hp = mix_out_mem_attend(y_mix, w_mix, j, _pad_rows(h.reshape(B, T, D), SUBLANES), *mem_args,
                                    heads=P["m_heads"], hd=P["m_hd"], bm=SUBLANES)
            h = hp[:, :T].reshape(M, D)
        final = P["norm_final"] if li == depth - 1 else None
        h = mlp(h, P["norm_mlp"], P["mlp_w_up"], P["mlp_w_down"], li, final_gain=final, **mlp_tile)

    y = h.reshape(B, T, D)
    win = [jnp.stack([nb[g] for nb in new_a], axis=0) for g in range(groups)]
    return y, win, jnp.stack(new_b, axis=0), jnp.stack(new_c, axis=0)


def _bias_tables(rel_bias_table, hpg, span):
    steps = np.arange(A_BLOCK)[:, None] + A_BLOCK - np.arange(2 * A_BLOCK)[None, :]
    eye = np.eye(N_BUCKETS, dtype=np.float32)
    prompt, sample = [], []
    for g, d in enumerate(A_DILATIONS):
        tg_t = rel_bias_table[:, g * hpg:(g + 1) * hpg].T
        onehot = eye[:, t5_bucket(np.clip(steps, 0, span) * d).reshape(-1)]
        bias = jnp.dot(tg_t, jnp.asarray(onehot), precision=lax.Precision.HIGHEST).reshape(hpg, A_BLOCK, 2 * A_BLOCK)
        band = jnp.asarray((steps >= 0) & (steps <= span))
        prompt.append(jnp.where(band[None], bias, -jnp.inf))
        dist = np.concatenate([(span - np.arange(span)) * d, np.zeros(1, np.int64)])
        row = jnp.dot(tg_t, jnp.asarray(eye[:, t5_bucket(dist)]), precision=lax.Precision.HIGHEST)
        sample.append(jnp.broadcast_to(row.T[:, :, None], (span + 1, hpg, LANES)))
    return jnp.stack(prompt, axis=0), jnp.stack(sample, axis=0)


def kernel(x_prompt, x_sample, cache_win0, cache_win1, cache_win2, state_hgrn, state_gla, cache_mem, mem_prompt, rel_bias_table, a_w_in, a_w_out, b_w_in, b_w_out, b_lower_bound, b_gnorm, c_w_in, c_w_gate, c_b_gate, c_w_out, c_gnorm, mem_norm, mem_w_kv, mem_w_q, mem_w_out, norm_mix, norm_mem, norm_mlp, mlp_w_up, mlp_w_down, norm_final):
    depth, D = norm_mix.shape
    hpg, hd_a = cache_win0.shape[-2], cache_win0.shape[-1]
    span = A_BLOCK
    assert all(w // d == span for w, d in zip(A_WINDOWS, A_DILATIONS))
    m_heads, m_hd = cache_mem.shape[-2], cache_mem.shape[-1]
    c_heads, c_dk, c_dv = state_gla.shape[2:]
    wk = c_heads * c_dk
    n_main = 2 * wk + 2 * c_heads * c_dv
    bias_prompt, bias_sample = _bias_tables(rel_bias_table, hpg, span)

    sm = jax.nn.softmax(b_lower_bound.astype(F32), axis=0)
    P = dict(
        a_hd=hd_a, a_hpg=hpg, m_heads=m_heads, m_hd=m_hd,
        a_bias_prompt=bias_prompt, a_bias_sample=bias_sample,
        a_w_in=a_w_in, a_w_out=a_w_out.astype(BF16), b_w_in=b_w_in, b_w_out=b_w_out.astype(BF16),
        lower_bounds=jnp.cumsum(sm, axis=0) - sm, b_gnorm=b_gnorm, b_state_shape=state_hgrn.shape[2:],
        c_w_in=c_w_in, c_w_low=c_w_in[:, :, n_main:], c_w_gate=c_w_gate,
        c_b_gate=c_b_gate[:, None, :], c_w_out=c_w_out.astype(BF16), c_gnorm=c_gnorm,
        c_state_shape=state_gla.shape[2:], c_wk=wk,
        mem_w_q=mem_w_q.astype(BF16), mem_w_out=mem_w_out.astype(BF16),
        norm_mix=norm_mix[:, None, :], norm_mem=norm_mem[:, None, :], norm_mlp=norm_mlp[:, None, :],
        mlp_w_up=mlp_w_up, mlp_w_down=mlp_w_down, norm_final=norm_final[None, :],
    )

    Bm, Mt, _ = mem_prompt.shape
    mem_flat = mem_prompt.reshape(Bm * Mt, D)
    mem_gain = mem_norm[:, None, :]
    p_mem = jnp.stack([linear(mem_flat, mem_w_kv, li, gain=(mem_gain, li), bm=512, bn=512)
                       for li in range(depth)], axis=0).reshape(depth, Bm, Mt, 2 * m_heads * m_hd)

    caches = (cache_win0, cache_win1, cache_win2)
    y_prompt, p_win, p_hgrn, p_gla = _run_trunk(x_prompt, p_mem, None, None, None, P, True)
    y_sample, new_rows, s_hgrn, s_gla = _run_trunk(
        x_sample, cache_mem.reshape(cache_mem.shape[:3] + (-1,)), caches, state_hgrn, state_gla, P, False)
    s_win = [window_update(c, n, chunk=256) for c, n in zip(caches, new_rows)]
    return (y_prompt, y_sample, p_win[0], p_win[1], p_win[2], p_hgrn, p_gla,
            p_mem.reshape(depth, Bm, Mt, 2, m_heads, m_hd),
            s_win[0], s_win[1], s_win[2], s_hgrn, s_gla)
```

```python
import functools
import math

import numpy as np
import jax
import jax.numpy as jnp
from jax import lax
from jax.experimental import pallas as pl
from jax.experimental.pallas import tpu as pltpu

F32 = jnp.float32
BF16 = jnp.bfloat16

EPS = 1e-6
N_MIXERS = 3
A_WINDOWS = (128, 512, 2048)
A_DILATIONS = (1, 4, 16)
A_BLOCK = 128
N_BUCKETS = 32
MAX_DISTANCE = 2048
GLA_GATE_NORMALIZER = 16.0
LA_BASE = 16
LA_HEAD_BATCH_LANES = 512
LA_CHUNK = 128
RIDER_ROWS = 16

V7X_VMEM_BYTES = 64 * 1024 * 1024
VMEM_LIMIT = V7X_VMEM_BYTES - 8 * 1024 * 1024
SUBLANES = 8
LANES = 128


def _cparams(n_axes):
    return pltpu.CompilerParams(dimension_semantics=("arbitrary",) * n_axes,
                                vmem_limit_bytes=VMEM_LIMIT)


def _tile(n, target, align):
    if n <= target:
        return n
    t = (target // align) * align
    while n % t:
        t -= align
    return t


def _rms(x, g):
    r = lax.rsqrt(jnp.mean(x * x, axis=-1, keepdims=True) + EPS)
    return (x * r) * g


def _dot(a, b):
    return jnp.dot(a, b, preferred_element_type=F32)


def _dot_nt(a, b):
    return lax.dot_general(a, b, (((1,), (1,)), ((), ())), preferred_element_type=F32)


def _dot_tn(a, b):
    return lax.dot_general(a, b, (((0,), (0,)), ((), ())), preferred_element_type=F32)


def _bf_round(a):
    return a.astype(BF16).astype(F32)


def _linear_body(*refs, riders):
    if riders:
        x_ref, w_ref, g_ref, r_ref, o_ref, or_ref, xs_ref = refs
    else:
        x_ref, w_ref, g_ref, o_ref, xs_ref = refs
    bm = x_ref.shape[0]

    @pl.when(pl.program_id(1) == 0)
    def _():
        xs_ref[0:bm] = _rms(x_ref[...], g_ref[...]).astype(BF16)
        if riders:
            xs_ref[bm:bm + riders] = _rms(r_ref[...], g_ref[...]).astype(BF16)

    acc = _dot(xs_ref[...], w_ref[...].astype(BF16))
    if riders:
        o_ref[...] = acc[:bm]
        or_ref[...] = acc[bm:]
    else:
        o_ref[...] = acc


def linear(x, w, layer, *, n_out=None, gain, rider=None, bm, bn):
    M, K = x.shape
    N = w.shape[-1] if n_out is None else n_out
    bm, bn = _tile(M, bm, SUBLANES), _tile(N, bn, LANES)
    riders = 0 if rider is None else rider.shape[0]
    g_arr, g_idx = gain
    in_specs = [pl.BlockSpec((bm, K), lambda i, j: (i, 0), pipeline_mode=pl.Buffered(1)),
                pl.BlockSpec((None, K, bn), lambda i, j: (layer, 0, j)),
                pl.BlockSpec((None, 1, K), lambda i, j: (g_idx, 0, 0))]
    args = [x, w, g_arr]
    out_specs = [pl.BlockSpec((bm, bn), lambda i, j: (i, j))]
    out_shape = [jax.ShapeDtypeStruct((M, N), F32)]
    if riders:
        in_specs.append(pl.BlockSpec((riders, K), lambda i, j: (0, 0)))
        args.append(rider)
        out_specs.append(pl.BlockSpec((None, riders, bn), lambda i, j: (i, 0, j)))
        out_shape.append(jax.ShapeDtypeStruct((M // bm, riders, N), F32))
    outs = pl.pallas_call(
        functools.partial(_linear_body, riders=riders),
        grid=(M // bm, N // bn),
        in_specs=in_specs,
        out_specs=out_specs,
        out_shape=out_shape,
        scratch_shapes=[pltpu.VMEM((bm + riders, K), BF16)],
        compiler_params=_cparams(2),
        name="linear",
    )(*args)
    return (outs[0], outs[1][0]) if riders else outs[0]


def _mlp_body(*refs, has_final, riders):
    refs = list(refs)
    x_ref, g_ref, wu_ref, wd_ref = refs[:4]
    pos = 4
    fg_ref = r_ref = or_ref = None
    if has_final:
        fg_ref = refs[pos]
        pos += 1
    if riders:
        r_ref = refs[pos]
        pos += 1
    o_ref = refs[pos]
    if riders:
        or_ref = refs[pos + 1]
    xs_ref = refs[-1]
    bm = x_ref.shape[0]
    f = pl.program_id(1)

    @pl.when(f == 0)
    def _():
        x = x_ref[...]
        xs_ref[0:bm] = _rms(x, g_ref[...]).astype(BF16)
        o_ref[...] = x
        if riders:
            r = r_ref[...]
            xs_ref[bm:bm + riders] = _rms(r, g_ref[...]).astype(BF16)
            or_ref[...] = r

    hdn = _dot(xs_ref[...], wu_ref[...].astype(BF16))
    hdn = jnp.square(jnp.maximum(hdn, 0.0)).astype(BF16)
    down = _dot(hdn, wd_ref[...].astype(BF16))
    if riders:
        o_ref[...] += down[:bm]
        or_ref[...] += down[bm:]
    else:
        o_ref[...] += down

    if has_final:
        @pl.when(f == pl.num_programs(1) - 1)
        def _():
            o_ref[...] = _rms(o_ref[...], fg_ref[...])
            if riders:
                or_ref[...] = _rms(or_ref[...], fg_ref[...])


def mlp(x, gains, w_up, w_down, layer, *, final_gain=None, rider=None, bm, bf):
    M, D = x.shape
    FF = w_up.shape[-1]
    bm, bf = _tile(M, bm, SUBLANES), _tile(FF, bf, LANES)
    riders = 0 if rider is None else rider.shape[0]
    in_specs = [pl.BlockSpec((bm, D), lambda i, f: (i, 0), pipeline_mode=pl.Buffered(1)),
                pl.BlockSpec((None, 1, D), lambda i, f: (layer, 0, 0)),
                pl.BlockSpec((None, D, bf), lambda i, f: (layer, 0, f)),
                pl.BlockSpec((None, bf, D), lambda i, f: (layer, f, 0))]
    args = [x, gains, w_up, w_down]
    if final_gain is not None:
        in_specs.append(pl.BlockSpec((1, D), lambda i, f: (0, 0)))
        args.append(final_gain)
    out_specs = [pl.BlockSpec((bm, D), lambda i, f: (i, 0), pipeline_mode=pl.Buffered(1))]
    out_shape = [jax.ShapeDtypeStruct((M, D), F32)]
    if riders:
        in_specs.append(pl.BlockSpec((riders, D), lambda i, f: (0, 0)))
        args.append(rider)
        out_specs.append(pl.BlockSpec((None, riders, D), lambda i, f: (i, 0, 0)))
        out_shape.append(jax.ShapeDtypeStruct((M // bm, riders, D), F32))
    outs = pl.pallas_call(
        functools.partial(_mlp_body, has_final=final_gain is not None, riders=riders),
        grid=(M // bm, FF // bf),
        in_specs=in_specs,
        out_specs=out_specs,
        out_shape=out_shape,
        scratch_shapes=[pltpu.VMEM((bm + riders, D), BF16)],
        compiler_params=_cparams(2),
        name="mlp",
    )(*args)
    return (outs[0], outs[1][0]) if riders else outs[0]


def _mem_body(y_ref, wm_ref, h_ref, g_ref, wq_ref, kv_ref, wo_ref, o_ref, *, heads, hd):
    x = h_ref[...] + _dot(y_ref[...].astype(BF16), wm_ref[...])
    xn = _rms(x, g_ref[...]).astype(BF16)
    q = _dot(xn, wq_ref[...])
    width = heads * hd
    scale = hd ** -0.5
    outs = []
    for h in range(heads):
        qh = q[:, h * hd:(h + 1) * hd].astype(BF16)
        kh = kv_ref[:, h * hd:(h + 1) * hd].astype(BF16)
        vh = kv_ref[:, width + h * hd:width + (h + 1) * hd].astype(BF16)
        s = _dot_nt(qh, kh) * scale
        p = jnp.exp(s - jnp.max(s, axis=-1, keepdims=True))
        p = p / jnp.sum(p, axis=-1, keepdims=True)
        outs.append(_dot(p.astype(BF16), vh))
    o = jnp.concatenate(outs, axis=-1).astype(BF16)
    o_ref[...] = x + _dot(o, wo_ref[...])


def mix_out_mem_attend(y, w_mix, mix_layer, h, gains, w_q, kv, w_o, layer, *, heads, hd, bm):
    B, T, D = h.shape
    Ky = y.shape[-1]
    Mt = kv.shape[2]
    width = heads * hd
    bm = _tile(T, bm, SUBLANES)
    return pl.pallas_call(
        functools.partial(_mem_body, heads=heads, hd=hd),
        grid=(B, T // bm),
        in_specs=[pl.BlockSpec((None, bm, Ky), lambda b, i: (b, i, 0)),
                  pl.BlockSpec((None, Ky, D), lambda b, i: (mix_layer, 0, 0), pipeline_mode=pl.Buffered(1)),
                  pl.BlockSpec((None, bm, D), lambda b, i: (b, i, 0)),
                  pl.BlockSpec((None, 1, D), lambda b, i: (layer, 0, 0)),
                  pl.BlockSpec((None, D, width), lambda b, i: (layer, 0, 0), pipeline_mode=pl.Buffered(1)),
                  pl.BlockSpec((None, None, Mt, 2 * width), lambda b, i: (layer, b, 0, 0)),
                  pl.BlockSpec((None, width, D), lambda b, i: (layer, 0, 0), pipeline_mode=pl.Buffered(1))],
        out_specs=pl.BlockSpec((None, bm, D), lambda b, i: (b, i, 0)),
        out_shape=jax.ShapeDtypeStruct((B, T, D), F32),
        compiler_params=_cparams(2),
        name="mix_out_mem_attend",
    )(y, w_mix, h, gains, w_q, kv, w_o)


def t5_bucket(dist):
    dist = np.asarray(dist, np.int64)
    max_exact = N_BUCKETS // 2
    d = np.maximum(dist, 1).astype(np.float64)
    large = max_exact + (np.log(d / max_exact) / math.log(MAX_DISTANCE / max_exact)
                         * (N_BUCKETS - max_exact)).astype(np.int64)
    return np.where(dist < max_exact, dist, np.minimum(large, N_BUCKETS - 1)).astype(np.int32)


ATTN_UNROLL = 4


def _unroll(trip):
    return max(u for u in range(1, ATTN_UNROLL + 2) if trip % u == 0)


def _attn_block(q, k2, v2, bias2, scale):
    blk = q.shape[0]
    s = _dot_nt(q.astype(BF16), k2.astype(BF16)) * scale + bias2
    m = jnp.max(jnp.maximum(s[:, :blk], s[:, blk:]), axis=-1, keepdims=True)
    p = jnp.exp(s - m)
    l = jnp.sum(p[:, :blk] + p[:, blk:], axis=-1, keepdims=True)
    o = _dot(p.astype(BF16), v2.astype(BF16))
    return o / l, m + jnp.log(l)


def _dil_attn_body(*refs, dils, tile, hd):
    n_groups = len(dils)
    bias_ref, o_ref, og_ref, lse_ref = refs[5 * n_groups:]
    not_first_tile = pl.program_id(1) > 0
    blk = A_BLOCK
    scale = hd ** -0.5

    for g, d in enumerate(dils):
        q_ref, kc_ref, vc_ref, kp_ref, vp_ref = refs[5 * g:5 * g + 5]
        stream_rows = blk * d
        n_blocks = tile // stream_rows

        def rows(start, d=d):
            if d == 1:
                return pl.ds(pl.multiple_of(start, blk), blk)
            return pl.ds(start, blk, stride=d)

        bias2 = bias_ref[g]
        bias2_first = jnp.concatenate(
            [jnp.where(not_first_tile, bias2[:, :blk], -jnp.inf), bias2[:, blk:]], axis=1)

        def run(starts, prev_of, bias, g=g, q_ref=q_ref, kc_ref=kc_ref, vc_ref=vc_ref, rows=rows):
            results = []
            for s in starts:
                kp, vp = prev_of(s)
                k2 = jnp.concatenate([kp, kc_ref[rows(s), :]], axis=0)
                v2 = jnp.concatenate([vp, vc_ref[rows(s), :]], axis=0)
                results.append(_attn_block(q_ref[rows(s), :], k2, v2, bias, scale))
            for s, (o, lse) in zip(starts, results):
                og_ref[g, rows(s), :] = o
                lse_ref[g, rows(s), :] = jnp.broadcast_to(lse, (blk, hd))

        def first(it, carry, run=run, rows=rows, kp_ref=kp_ref, vp_ref=vp_ref, u=_unroll(d)):
            run([it * u + k for k in range(u)], lambda s: (kp_ref[rows(s), :], vp_ref[rows(s), :]), bias2_first)
            return carry

        n_rest = d * (n_blocks - 1)

        def rest(it, carry, run=run, rows=rows, kc_ref=kc_ref, vc_ref=vc_ref, d=d, stream_rows=stream_rows,
                 u=_unroll(n_rest)):
            idx = [it * u + k for k in range(u)]
            run([i % d + (1 + i // d) * stream_rows for i in idx],
                lambda s: (kc_ref[rows(s - stream_rows), :], vc_ref[rows(s - stream_rows), :]), bias2)
            return carry

        lax.fori_loop(0, d // _unroll(d), first, 0)
        if n_blocks > 1:
            lax.fori_loop(0, n_rest // _unroll(n_rest), rest, 0)

    lses = [lse_ref[g] for g in range(n_groups)]
    mx = functools.reduce(jnp.maximum, lses)
    es = [jnp.exp(l - mx) for l in lses]
    den = functools.reduce(lambda a, b: a + b, es)
    acc = (es[0] / den) * og_ref[0]
    for g in range(1, n_groups):
        acc = acc + (es[g] / den) * og_ref[g]
    o_ref[...] = acc


def dilated_attention_prompt(qkv, bias, *, heads, hd, tile):
    B, T, _ = qkv.shape
    dils = A_DILATIONS
    n_groups = len(dils)
    tile = min(tile, T)
    assert hd == LANES and T % tile == 0 and all(tile % (A_BLOCK * d) == 0 for d in dils)
    sec = n_groups * heads

    def cur(which, g):
        return pl.BlockSpec((None, tile, hd), lambda b, n, h: (b, n, which * sec + g * heads + h))

    def prev(which, g):
        rows = A_BLOCK * dils[g]
        per = tile // rows
        return pl.BlockSpec((None, rows, hd),
                            lambda b, n, h: (b, jnp.maximum(n * per - 1, 0), which * sec + g * heads + h))

    in_specs = []
    for g in range(n_groups):
        in_specs += [cur(0, g), cur(1, g), cur(2, g), prev(1, g), prev(2, g)]
    in_specs.append(pl.BlockSpec((n_groups, None, A_BLOCK, 2 * A_BLOCK), lambda b, n, h: (0, h, 0, 0)))
    return pl.pallas_call(
        functools.partial(_dil_attn_body, dils=dils, tile=tile, hd=hd),
        grid=(B, T // tile, heads),
        in_specs=in_specs,
        out_specs=pl.BlockSpec((None, tile, hd), lambda b, n, h: (b, n, h)),
        out_shape=jax.ShapeDtypeStruct((B, T, heads * hd), F32),
        scratch_shapes=[pltpu.VMEM((n_groups, tile, hd), F32), pltpu.VMEM((n_groups, tile, hd), F32)],
        compiler_params=_cparams(3),
        name="dilated_attention_prompt",
    )(*([qkv] * (5 * n_groups)), bias)


def _dec_attn_body(qkv_ref, *refs, groups, span, scale):
    bufs, bias_ref, o_ref = refs[:groups], refs[groups], refs[groups + 1]
    outs, lses = [], []
    for g in range(groups):
        q = _bf_round(qkv_ref[g])
        kn = _bf_round(qkv_ref[groups + g])
        vn = _bf_round(qkv_ref[2 * groups + g])
        k = _bf_round(bufs[g][:, 0])
        v = _bf_round(bufs[g][:, 1])
        s = jnp.sum(k * q[None], axis=-1, keepdims=True) * scale + bias_ref[g, :span]
        s_self = jnp.sum(q * kn, axis=-1, keepdims=True) * scale + bias_ref[g, span]
        m = jnp.maximum(jnp.max(s, axis=0), s_self)
        p = jnp.exp(s - m[None])
        p_self = jnp.exp(s_self - m)
        l = jnp.sum(p, axis=0) + p_self
        outs.append(jnp.sum(_bf_round(p / l[None]) * v, axis=0) + _bf_round(p_self / l) * vn)
        lses.append(m + jnp.log(l))
    mx = functools.reduce(jnp.maximum, lses)
    es = [jnp.exp(l - mx) for l in lses]
    den = functools.reduce(lambda a, b: a + b, es)
    acc = (es[0] / den) * outs[0]
    for g in range(1, groups):
        acc = acc + (es[g] / den) * outs[g]
    o_ref[...] = acc


def dilated_attention_sample(qkv, caches, layer, bias, *, heads, hd, span):
    B = qkv.shape[0]
    groups = len(caches)
    in_specs = [pl.BlockSpec((None, 3 * groups, heads, hd), lambda b: (b, 0, 0, 0))]
    views = []
    for g, c in enumerate(caches):
        d = A_DILATIONS[g]
        assert c.shape[2] == span * d
        views.append(c.reshape(c.shape[0], B, span, d, 2, heads, hd))
        in_specs.append(pl.BlockSpec((None, None, span, None, 2, heads, hd),
                                     lambda b: (layer, b, 0, 0, 0, 0, 0)))
    in_specs.append(pl.BlockSpec((groups, span + 1, heads, hd), lambda b: (0, 0, 0, 0)))
    return pl.pallas_call(
        functools.partial(_dec_attn_body, groups=groups, span=span, scale=hd ** -0.5),
        grid=(B,),
        in_specs=in_specs,
        out_specs=pl.BlockSpec((None, heads, hd), lambda b: (b, 0, 0)),
        out_shape=jax.ShapeDtypeStruct((B, heads, hd), F32),
        compiler_params=_cparams(1),
        name="dilated_attention_sample",
    )(qkv, *views, bias)


def _to_window_body(*refs, heads):
    k_ref, v_ref, o_ref = refs[0], refs[1], refs[-1]
    o_ref[:, 0] = pltpu.einshape("r(hd)->rhd", k_ref[...], h=heads)
    o_ref[:, 1] = pltpu.einshape("r(hd)->rhd", v_ref[...], h=heads)


def to_window(qkv, g, layer, n_layers, acc, *, rows, groups, heads, hd):
    B, T, _ = qkv.shape
    W = heads * hd
    chunk = _tile(rows, 256, SUBLANES)
    assert (T - rows) % chunk == 0
    first = (T - rows) // chunk
    in_specs = [pl.BlockSpec((None, chunk, W), lambda b, c: (b, first + c, groups + g)),
                pl.BlockSpec((None, chunk, W), lambda b, c: (b, first + c, 2 * groups + g))]
    args = [qkv, qkv]
    aliases = {}
    if acc is not None:
        in_specs.append(pl.BlockSpec(memory_space=pl.ANY))
        args.append(acc)
        aliases = {2: 0}
    return pl.pallas_call(
        functools.partial(_to_window_body, heads=heads),
        grid=(B, rows // chunk),
        in_specs=in_specs,
        out_specs=pl.BlockSpec((None, None, chunk, 2, heads, hd), lambda b, c: (layer, b, c, 0, 0, 0)),
        out_shape=jax.ShapeDtypeStruct((n_layers, B, rows, 2, heads, hd), F32),
        input_output_aliases=aliases,
        compiler_params=_cparams(2),
        name="to_window",
    )(*args)


def _window_update_body(main_ref, next_ref, new_ref, o_ref):
    rows = main_ref.shape[0]
    last_chunk = pl.program_id(2) == pl.num_programs(2) - 1
    o_ref[0:rows - 1] = main_ref[1:rows]
    o_ref[rows - 1:rows] = jnp.where(last_chunk, new_ref[...], next_ref[...])


def window_update(cache, new, *, chunk):
    n_layers, B, rows = cache.shape[:3]
    tail = cache.shape[3:]
    chunk = _tile(rows, chunk, 1)
    zeros = (0,) * len(tail)
    return pl.pallas_call(
        _window_update_body,
        grid=(n_layers, B, rows // chunk),
        in_specs=[pl.BlockSpec((None, None, chunk) + tail, lambda j, b, c: (j, b, c) + zeros),
                  pl.BlockSpec((None, None, 1) + tail,
                               lambda j, b, c: (j, b, jnp.minimum((c + 1) * chunk, rows - 1)) + zeros),
                  pl.BlockSpec((None, None, 1) + tail, lambda j, b, c: (j, b, 0) + zeros)],
        out_specs=pl.BlockSpec((None, None, chunk) + tail, lambda j, b, c: (j, b, c) + zeros),
        out_shape=jax.ShapeDtypeStruct(cache.shape, cache.dtype),
        compiler_params=_cparams(3),
        name="window_update",
    )(cache, cache, new)


def _gla_gate_body(x_ref, g_ref, wl_ref, wg_ref, b_ref, o_ref):
    xn = _rms(x_ref[...], g_ref[...]).astype(BF16)
    low = _dot(xn, wl_ref[...].astype(BF16))
    z = _dot(low.astype(BF16), wg_ref[...].astype(BF16)) + b_ref[...]
    log_sig = jnp.minimum(z, 0.0) - jnp.log(1.0 + jnp.exp(-jnp.abs(z)))
    o_ref[...] = log_sig / GLA_GATE_NORMALIZER


def gla_gate(x, gain, w_low, w_gate, b_gate, layer, *, bm):
    M, D = x.shape
    g_arr, g_idx = gain
    rank, dk = w_gate.shape[1], w_gate.shape[2]
    bm = _tile(M, bm, SUBLANES)
    return pl.pallas_call(
        _gla_gate_body,
        grid=(M // bm,),
        in_specs=[pl.BlockSpec((bm, D), lambda i: (i, 0)),
                  pl.BlockSpec((None, 1, D), lambda i: (g_idx, 0, 0)),
                  pl.BlockSpec((D, rank), lambda i: (0, 0)),
                  pl.BlockSpec((None, rank, dk), lambda i: (layer, 0, 0)),
                  pl.BlockSpec((None, 1, dk), lambda i: (layer, 0, 0))],
        out_specs=pl.BlockSpec((bm, dk), lambda i: (i, 0)),
        out_shape=jax.ShapeDtypeStruct((M, dk), F32),
        compiler_params=_cparams(1),
        name="gla_gate",
    )(x, g_arr, w_low, w_gate, b_gate)


def _cumsum_rows(x):
    rows = x.shape[0]
    row = lax.broadcasted_iota(jnp.int32, x.shape, 0)
    s = 1
    while s < rows:
        x = x + jnp.where(row >= s, pltpu.roll(x, s, axis=0), 0.0)
        s *= 2
    return x


def _block_refs(bcs, m, pairs):
    rows, dk = bcs.shape
    nb = rows // m
    ends = [bcs[(k + 1) * m - 1:(k + 1) * m, :] for k in range(nb)]
    parts = []
    for k in range(nb):
        if pairs:
            r = ends[k - 1] if k % 2 == 1 else ends[k]
        else:
            r = ends[k - 1] if k > 0 else jnp.zeros((1, dk), F32)
        parts.append(jnp.broadcast_to(r, (m, dk)))
    return parts[0] if nb == 1 else jnp.concatenate(parts, axis=0)


def _la_body(a_ref, b_ref, c_ref, gate_ref, e_ref, gn_ref, s0_ref, y_ref, so_ref, st_ref,
             *, hgrn2, heads, dk, dv, chunk, base, valid, q_scale, head_batch):
    n = pl.program_id(1)

    @pl.when(n == 0)
    def _():
        for h in range(heads):
            st_ref[h] = s0_ref[h].T

    tq = lax.broadcasted_iota(jnp.int32, (chunk, chunk), 0)
    ts = lax.broadcasted_iota(jnp.int32, (chunk, chunk), 1)
    row = lax.broadcasted_iota(jnp.int32, (chunk, 1), 0)
    sh = int(math.log2(base))
    mask_diag = jnp.logical_and((tq >> sh) == (ts >> sh), ts <= tq)
    levels = []
    m = base
    while m < chunk:
        s1 = int(math.log2(m))
        mk = jnp.logical_and((tq >> (s1 + 1)) == (ts >> (s1 + 1)),
                             jnp.logical_and(((tq >> s1) & 1) == 1, ((ts >> s1) & 1) == 0))
        levels.append((m, mk))
        m *= 2
    gn = gn_ref[...]

    def one_head(h):
        ksl = slice(h * dk, (h + 1) * dk)
        vsl = slice(h * dv, (h + 1) * dv)
        if hgrn2:
            a = a_ref[:, ksl]
            lb = e_ref[:, ksl]
            f = lb + (1.0 - lb) * jax.nn.sigmoid(b_ref[:, ksl])
            qa = a * jax.nn.sigmoid(a)
            kk = 1.0 - f
            g = jnp.log(f)
        else:
            qa = a_ref[:, ksl] * q_scale
            kk = b_ref[:, ksl]
            g = e_ref[:, ksl]
        v = c_ref[:, vsl]
        if valid < chunk:
            kk = jnp.where(row < valid, kk, 0.0)
            g = jnp.where(row < valid, g, 0.0)
        bcs = _cumsum_rows(g)
        vb = v.astype(BF16)
        st = st_ref[h]

        o = _dot_nt((qa * jnp.exp(bcs)).astype(BF16), st.astype(BF16))

        ref_d = _block_refs(bcs, base, False)
        a_mat = _dot_nt((qa * jnp.exp(bcs - ref_d)).astype(BF16), (kk * jnp.exp(ref_d - bcs)).astype(BF16))
        a_mat = jnp.where(mask_diag, a_mat, 0.0)
        for (m, mk) in levels:
            dec = jnp.exp(-jnp.abs(bcs - _block_refs(bcs, m, True)))
            a_mat = a_mat + jnp.where(mk, _dot_nt((qa * dec).astype(BF16), (kk * dec).astype(BF16)), 0.0)
        o = o + _dot(a_mat.astype(BF16), vb)

        b_last = bcs[chunk - 1:chunk, :]
        kd = (kk * jnp.exp(b_last - bcs)).astype(BF16)
        st_new = jnp.exp(b_last) * st + _dot_tn(vb, kd)

        gate = gate_ref[:, vsl]
        return st_new, _rms(o, gn) * (gate * jax.nn.sigmoid(gate))

    for h0 in range(0, heads, head_batch):
        hs = range(h0, min(h0 + head_batch, heads))
        results = [one_head(h) for h in hs]
        for h, (st_new, y) in zip(hs, results):
            st_ref[h] = st_new
            y_ref[:, h * dv:(h + 1) * dv] = y

    @pl.when(n == pl.num_programs(1) - 1)
    def _():
        for h in range(heads):
            so_ref[h] = st_ref[h].T


def gated_linear_attention(proj, extra, gnorm, s0, *, hgrn2, chunk, valid=None):
    B, T, _ = proj.shape
    _, H, dk, dv = s0.shape
    chunk = min(chunk, T)
    assert T % chunk == 0
    base = min(LA_BASE, chunk)
    valid = chunk if valid is None else valid
    wk, wv = H * dk, H * dv

    def col(width, start):
        assert start % width == 0
        return pl.BlockSpec((None, chunk, width), lambda b, n: (b, n, start // width))

    if hgrn2:
        specs = [col(wk, 0), col(wk, wk), col(wv, 2 * wk), col(wv, 2 * wk + wv),
                 pl.BlockSpec((1, wk), lambda b, n: (0, 0))]
    else:
        specs = [col(wk, 0), col(wk, wk), col(wv, 2 * wk), col(wv, 2 * wk + wv),
                 pl.BlockSpec((None, chunk, wk), lambda b, n: (b, n, 0))]
    specs += [pl.BlockSpec((1, dv), lambda b, n: (0, 0)),
              pl.BlockSpec((None, H, dk, dv), lambda b, n: (b, 0, 0, 0))]
    return pl.pallas_call(
        functools.partial(_la_body, hgrn2=hgrn2, heads=H, dk=dk, dv=dv, chunk=chunk, base=base,
                          valid=valid, q_scale=1.0 if hgrn2 else dk ** -0.5,
                          head_batch=max(1, LA_HEAD_BATCH_LANES // max(dk, dv))),
        grid=(B, T // chunk),
        in_specs=specs,
        out_specs=[pl.BlockSpec((None, chunk, wv), lambda b, n: (b, n, 0)),
                   pl.BlockSpec((None, H, dk, dv), lambda b, n: (b, 0, 0, 0))],
        out_shape=[jax.ShapeDtypeStruct((B, T, wv), F32),
                   jax.ShapeDtypeStruct((B, H, dk, dv), F32)],
        scratch_shapes=[pltpu.VMEM((H, dv, dk), F32)],
        compiler_params=_cparams(2),
        name="gated_linear_attention",
    )(proj, proj, proj, proj, extra, gnorm, s0)


def _pad_rows(x, rows):
    return jnp.pad(x, ((0, 0), (0, rows - x.shape[1]), (0, 0)))


def _trunk(xp, xs, p_mem, s_mem, caches, b_states, c_states, P):
    Bp, T, D = xp.shape
    Bs = xs.shape[0]
    assert xs.shape[1] == 1 and Bs <= RIDER_ROWS
    Mp = Bp * T
    depth = P["norm_mix"].shape[0]
    hd_a, hpg = P["a_hd"], P["a_hpg"]
    groups = len(A_DILATIONS)
    Wa = hpg * hd_a
    tile = dict(bm=2048, bn=512)
    mlp_tile = dict(bm=1024, bf=512)

    def ride(h):
        return jnp.pad(h, ((0, RIDER_ROWS - Bs), (0, 0)))

    hp, hs = xp.reshape(Mp, D), xs.reshape(Bs, D)
    p_win = [None] * groups
    new_rows, p_b, s_b, p_c, s_c = [], [], [], [], []
    n_a = sum(1 for li in range(depth) if li % N_MIXERS == 0)
    for li in range(depth):
        kind, j = li % N_MIXERS, li // N_MIXERS
        gmix = (P["norm_mix"], li)
        if kind == 0:
            qkv_p, qkv_s = linear(hp, P["a_w_in"], j, gain=gmix, rider=ride(hs), **tile)
            qkv3 = qkv_p.reshape(Bp, T, 3 * groups * Wa)
            yp = dilated_attention_prompt(qkv3, P["a_bias_prompt"], heads=hpg, hd=hd_a, tile=16 * A_BLOCK)
            for g in range(groups):
                p_win[g] = to_window(qkv3, g, j, n_a, p_win[g], rows=min(A_WINDOWS[g], T),
                                     groups=groups, heads=hpg, hd=hd_a)
            qkv5 = qkv_s[:Bs].reshape(Bs, 3, groups, hpg, hd_a)
            ys = dilated_attention_sample(qkv5.reshape(Bs, 3 * groups, hpg, hd_a), caches, j,
                                          P["a_bias_sample"], heads=hpg, hd=hd_a, span=A_BLOCK)
            new_rows.append([qkv5[:, 1:3, g][:, None] for g in range(groups)])
            ys = _pad_rows(ys.reshape(Bs, 1, Wa), SUBLANES)
            w_mix = P["a_w_out"]
        elif kind == 1:
            proj_p, proj_s = linear(hp, P["b_w_in"], j, gain=gmix, rider=ride(hs), **tile)
            lb = P["lower_bounds"][li][None, :]
            gn = P["b_gnorm"][j][None, :]
            yp, S = gated_linear_attention(proj_p.reshape(Bp, T, -1), lb, gn,
                                           jnp.zeros((Bp,) + b_states.shape[2:], F32), hgrn2=True, chunk=LA_CHUNK)
            p_b.append(S)
            ys, S = gated_linear_attention(_pad_rows(proj_s[:Bs, None], SUBLANES), lb, gn, b_states[j],
                                           hgrn2=True, chunk=SUBLANES, valid=1)
            s_b.append(S)
            w_mix = P["b_w_out"]
        else:
            wk = P["c_wk"]
            n_main = P["c_n_main"]
            proj_p, proj_s = linear(hp, P["c_w_in"], j, n_out=n_main, gain=gmix, rider=ride(hs), **tile)
            gate_args = (gmix, P["c_w_low"][j], P["c_w_gate"], P["c_b_gate"], j)
            gn = P["c_gnorm"][j][None, :]
            yp, S = gated_linear_attention(proj_p.reshape(Bp, T, n_main),
                                           gla_gate(hp, *gate_args, bm=1024).reshape(Bp, T, wk), gn,
                                           jnp.zeros((Bp,) + c_states.shape[2:], F32), hgrn2=False, chunk=LA_CHUNK)
            p_c.append(S)
            ys, S = gated_linear_attention(_pad_rows(proj_s[:Bs, None], SUBLANES),
                                           _pad_rows(gla_gate(hs, *gate_args, bm=SUBLANES)[:, None], SUBLANES), gn,
                                           c_states[j], hgrn2=False, chunk=SUBLANES, valid=1)
            s_c.append(S)
            w_mix = P["c_w_out"]

        mem_kw = dict(heads=P["m_heads"], hd=P["m_hd"])
        hp = mix_out_mem_attend(yp.reshape(Bp, T, -1), w_mix, j, hp.reshape(Bp, T, D), P["norm_mem"], P["mem_w_q"],
                                p_mem, P["mem_w_out"], li, bm=512, **mem_kw).reshape(Mp, D)
        hs = mix_out_mem_attend(ys, w_mix, j, _pad_rows(hs[:, None], SUBLANES), P["norm_mem"], P["mem_w_q"],
                                s_mem, P["mem_w_out"], li, bm=SUBLANES, **mem_kw)[:, 0]
        final = P["norm_final"] if li == depth - 1 else None
        hp, hs = mlp(hp, P["norm_mlp"], P["mlp_w_up"], P["mlp_w_down"], li, final_gain=final,
                     rider=ride(hs), **mlp_tile)
        hs = hs[:Bs]

    new_rows = [jnp.stack([rows[g] for rows in new_rows], axis=0) for g in range(groups)]
    return (hp.reshape(Bp, T, D), hs.reshape(Bs, 1, D), p_win, new_rows,
            jnp.stack(p_b, axis=0), jnp.stack(s_b, axis=0), jnp.stack(p_c, axis=0), jnp.stack(s_c, axis=0))


def _bias_tables(rel_bias_table, hpg, span):
    steps = np.arange(A_BLOCK)[:, None] + A_BLOCK - np.arange(2 * A_BLOCK)[None, :]
    eye = np.eye(N_BUCKETS, dtype=np.float32)
    prompt, sample = [], []
    for g, d in enumerate(A_DILATIONS):
        tg_t = rel_bias_table[:, g * hpg:(g + 1) * hpg].T
        onehot = eye[:, t5_bucket(np.clip(steps, 0, span) * d).reshape(-1)]
        bias = jnp.dot(tg_t, jnp.asarray(onehot), precision=lax.Precision.HIGHEST).reshape(hpg, A_BLOCK, 2 * A_BLOCK)
        band = jnp.asarray((steps >= 0) & (steps <= span))
        prompt.append(jnp.where(band[None], bias, -jnp.inf))
        dist = np.concatenate([(span - np.arange(span)) * d, np.zeros(1, np.int64)])
        row = jnp.dot(tg_t, jnp.asarray(eye[:, t5_bucket(dist)]), precision=lax.Precision.HIGHEST)
        sample.append(jnp.broadcast_to(row.T[:, :, None], (span + 1, hpg, LANES)))
    return jnp.stack(prompt, axis=0), jnp.stack(sample, axis=0)


def kernel(x_prompt, x_sample, cache_win0, cache_win1, cache_win2, state_hgrn, state_gla, cache_mem, mem_prompt, rel_bias_table, a_w_in, a_w_out, b_w_in, b_w_out, b_lower_bound, b_gnorm, c_w_in, c_w_gate, c_b_gate, c_w_out, c_gnorm, mem_norm, mem_w_kv, mem_w_q, mem_w_out, norm_mix, norm_mem, norm_mlp, mlp_w_up, mlp_w_down, norm_final):
    depth, D = norm_mix.shape
    hpg, hd_a = cache_win0.shape[-2], cache_win0.shape[-1]
    span = A_BLOCK
    assert all(w // d == span for w, d in zip(A_WINDOWS, A_DILATIONS))
    m_heads, m_hd = cache_mem.shape[-2], cache_mem.shape[-1]
    c_heads, c_dk, c_dv = state_gla.shape[2:]
    wk = c_heads * c_dk
    n_main = 2 * wk + 2 * c_heads * c_dv
    bias_prompt, bias_sample = _bias_tables(rel_bias_table, hpg, span)

    sm = jax.nn.softmax(b_lower_bound.astype(F32), axis=0)
    P = dict(
        a_hd=hd_a, a_hpg=hpg, m_heads=m_heads, m_hd=m_hd,
        a_bias_prompt=bias_prompt, a_bias_sample=bias_sample,
        a_w_in=a_w_in, a_w_out=a_w_out.astype(BF16), b_w_in=b_w_in, b_w_out=b_w_out.astype(BF16),
        lower_bounds=jnp.cumsum(sm, axis=0) - sm, b_gnorm=b_gnorm,
        c_w_in=c_w_in, c_w_low=c_w_in[:, :, n_main:], c_w_gate=c_w_gate,
        c_b_gate=c_b_gate[:, None, :], c_w_out=c_w_out.astype(BF16), c_gnorm=c_gnorm,
        c_wk=wk, c_n_main=n_main,
        mem_w_q=mem_w_q.astype(BF16), mem_w_out=mem_w_out.astype(BF16),
        norm_mix=norm_mix[:, None, :], norm_mem=norm_mem[:, None, :], norm_mlp=norm_mlp[:, None, :],
        mlp_w_up=mlp_w_up, mlp_w_down=mlp_w_down, norm_final=norm_final[None, :],
    )

    Bm, Mt, _ = mem_prompt.shape
    mem_flat = mem_prompt.reshape(Bm * Mt, D)
    mem_gain = mem_norm[:, None, :]
    p_mem = jnp.stack([linear(mem_flat, mem_w_kv, li, gain=(mem_gain, li), bm=512, bn=512)
                       for li in range(depth)], axis=0).reshape(depth, Bm, Mt, 2 * m_heads * m_hd)

    caches = (cache_win0, cache_win1, cache_win2)
    y_prompt, y_sample, p_win, new_rows, p_hgrn, s_hgrn, p_gla, s_gla = _trunk(
        x_prompt, x_sample, p_mem, cache_mem.reshape(cache_mem.shape[:3] + (-1,)), caches, state_hgrn, state_gla, P)
    s_win = [window_update(c, n, chunk=256) for c, n in zip(caches, new_rows)]
    return (y_prompt, y_sample, p_win[0], p_win[1], p_win[2], p_hgrn, p_gla,
            p_mem.reshape(depth, Bm, Mt, 2, m_heads, m_hd),
            s_win[0], s_win[1], s_win[2], s_hgrn, s_gla)
```

```python
import functools
import math

import numpy as np
import jax
import jax.numpy as jnp
from jax import lax
from jax.experimental import pallas as pl
from jax.experimental.pallas import tpu as pltpu

F32 = jnp.float32
BF16 = jnp.bfloat16

EPS = 1e-6
N_MIXERS = 3
A_WINDOWS = (128, 512, 2048)
A_DILATIONS = (1, 4, 16)
A_BLOCK = 128
N_BUCKETS = 32
MAX_DISTANCE = 2048
GLA_GATE_NORMALIZER = 16.0
LA_BASE = 16
LA_HEAD_BATCH_LANES = 512
LA_CHUNK = 128
RIDER_ROWS = 16

V7X_VMEM_BYTES = 64 * 1024 * 1024
VMEM_LIMIT = V7X_VMEM_BYTES - 8 * 1024 * 1024
SUBLANES = 8
LANES = 128


def _cparams(n_axes):
    return pltpu.CompilerParams(dimension_semantics=("arbitrary",) * n_axes,
                                vmem_limit_bytes=VMEM_LIMIT)


def _tile(n, target, align):
    if n <= target:
        return n
    t = (target // align) * align
    while n % t:
        t -= align
    return t


def _rms(x, g):
    r = lax.rsqrt(jnp.mean(x * x, axis=-1, keepdims=True) + EPS)
    return (x * r) * g


def _dot(a, b):
    return jnp.dot(a, b, preferred_element_type=F32)


def _dot_nt(a, b):
    return lax.dot_general(a, b, (((1,), (1,)), ((), ())), preferred_element_type=F32)


def _dot_tn(a, b):
    return lax.dot_general(a, b, (((0,), (0,)), ((), ())), preferred_element_type=F32)


def _bf_round(a):
    return a.astype(BF16).astype(F32)


def _linear_body(*refs, riders, transposed):
    if riders:
        x_ref, w_ref, g_ref, r_ref, o_ref, or_ref, xs_ref = refs
    else:
        x_ref, w_ref, g_ref, o_ref, xs_ref = refs
    bm = x_ref.shape[0]

    @pl.when(pl.program_id(1) == 0)
    def _():
        xs_ref[0:bm] = _rms(x_ref[...], g_ref[...]).astype(BF16)
        if riders:
            xs_ref[bm:bm + riders] = _rms(r_ref[...], g_ref[...]).astype(BF16)

    w = w_ref[...].astype(BF16)
    acc = _dot_nt(xs_ref[...], w) if transposed else _dot(xs_ref[...], w)
    if riders:
        o_ref[...] = acc[:bm]
        or_ref[...] = acc[bm:]
    else:
        o_ref[...] = acc


def linear(x, w, layer, *, n_out=None, gain, rider=None, transposed=False, bm, bn):
    M, K = x.shape
    N = (w.shape[1] if transposed else w.shape[2]) if n_out is None else n_out
    bm, bn = _tile(M, bm, SUBLANES), _tile(N, bn, LANES)
    riders = 0 if rider is None else rider.shape[0]
    g_arr, g_idx = gain
    w_spec = (pl.BlockSpec((None, bn, K), lambda i, j: (layer, j, 0)) if transposed
              else pl.BlockSpec((None, K, bn), lambda i, j: (layer, 0, j)))
    in_specs = [pl.BlockSpec((bm, K), lambda i, j: (i, 0), pipeline_mode=pl.Buffered(1)),
                w_spec,
                pl.BlockSpec((None, 1, K), lambda i, j: (g_idx, 0, 0))]
    args = [x, w, g_arr]
    out_specs = [pl.BlockSpec((bm, bn), lambda i, j: (i, j))]
    out_shape = [jax.ShapeDtypeStruct((M, N), F32)]
    if riders:
        in_specs.append(pl.BlockSpec((riders, K), lambda i, j: (0, 0)))
        args.append(rider)
        out_specs.append(pl.BlockSpec((None, riders, bn), lambda i, j: (i, 0, j)))
        out_shape.append(jax.ShapeDtypeStruct((M // bm, riders, N), F32))
    outs = pl.pallas_call(
        functools.partial(_linear_body, riders=riders, transposed=transposed),
        grid=(M // bm, N // bn),
        in_specs=in_specs,
        out_specs=out_specs,
        out_shape=out_shape,
        scratch_shapes=[pltpu.VMEM((bm + riders, K), BF16)],
        compiler_params=_cparams(2),
        name="linear",
    )(*args)
    return (outs[0], outs[1][0]) if riders else outs[0]


def _mlp_body(*refs, has_final, riders):
    refs = list(refs)
    x_ref, g_ref, wu_ref, wd_ref = refs[:4]
    pos = 4
    fg_ref = r_ref = or_ref = None
    if has_final:
        fg_ref = refs[pos]
        pos += 1
    if riders:
        r_ref = refs[pos]
        pos += 1
    o_ref = refs[pos]
    if riders:
        or_ref = refs[pos + 1]
    xs_ref = refs[-1]
    bm = x_ref.shape[0]
    f = pl.program_id(1)

    @pl.when(f == 0)
    def _():
        x = x_ref[...]
        xs_ref[0:bm] = _rms(x, g_ref[...]).astype(BF16)
        o_ref[...] = x
        if riders:
            r = r_ref[...]
            xs_ref[bm:bm + riders] = _rms(r, g_ref[...]).astype(BF16)
            or_ref[...] = r

    hdn = _dot(xs_ref[...], wu_ref[...].astype(BF16))
    hdn = jnp.square(jnp.maximum(hdn, 0.0)).astype(BF16)
    down = _dot(hdn, wd_ref[...].astype(BF16))
    if riders:
        o_ref[...] += down[:bm]
        or_ref[...] += down[bm:]
    else:
        o_ref[...] += down

    if has_final:
        @pl.when(f == pl.num_programs(1) - 1)
        def _():
            o_ref[...] = _rms(o_ref[...], fg_ref[...])
            if riders:
                or_ref[...] = _rms(or_ref[...], fg_ref[...])


def mlp(x, gains, w_up, w_down, layer, *, final_gain=None, rider=None, bm, bf):
    M, D = x.shape
    FF = w_up.shape[-1]
    bm, bf = _tile(M, bm, SUBLANES), _tile(FF, bf, LANES)
    riders = 0 if rider is None else rider.shape[0]
    in_specs = [pl.BlockSpec((bm, D), lambda i, f: (i, 0), pipeline_mode=pl.Buffered(1)),
                pl.BlockSpec((None, 1, D), lambda i, f: (layer, 0, 0)),
                pl.BlockSpec((None, D, bf), lambda i, f: (layer, 0, f)),
                pl.BlockSpec((None, bf, D), lambda i, f: (layer, f, 0))]
    args = [x, gains, w_up, w_down]
    if final_gain is not None:
        in_specs.append(pl.BlockSpec((1, D), lambda i, f: (0, 0)))
        args.append(final_gain)
    out_specs = [pl.BlockSpec((bm, D), lambda i, f: (i, 0), pipeline_mode=pl.Buffered(1))]
    out_shape = [jax.ShapeDtypeStruct((M, D), F32)]
    if riders:
        in_specs.append(pl.BlockSpec((riders, D), lambda i, f: (0, 0)))
        args.append(rider)
        out_specs.append(pl.BlockSpec((None, riders, D), lambda i, f: (i, 0, 0)))
        out_shape.append(jax.ShapeDtypeStruct((M // bm, riders, D), F32))
    outs = pl.pallas_call(
        functools.partial(_mlp_body, has_final=final_gain is not None, riders=riders),
        grid=(M // bm, FF // bf),
        in_specs=in_specs,
        out_specs=out_specs,
        out_shape=out_shape,
        scratch_shapes=[pltpu.VMEM((bm + riders, D), BF16)],
        compiler_params=_cparams(2),
        name="mlp",
    )(*args)
    return (outs[0], outs[1][0]) if riders else outs[0]


def _mem_body(y_ref, wm_ref, h_ref, g_ref, wq_ref, kv_ref, wo_ref, o_ref, *, heads, hd):
    x = h_ref[...] + _dot(y_ref[...].astype(BF16), wm_ref[...])
    xn = _rms(x, g_ref[...]).astype(BF16)
    q = _dot(xn, wq_ref[...])
    width = heads * hd
    scale = hd ** -0.5
    outs = []
    for h in range(heads):
        qh = q[:, h * hd:(h + 1) * hd].astype(BF16)
        kh = kv_ref[:, h * hd:(h + 1) * hd].astype(BF16)
        vh = kv_ref[:, width + h * hd:width + (h + 1) * hd].astype(BF16)
        s = _dot_nt(qh, kh) * scale
        p = jnp.exp(s - jnp.max(s, axis=-1, keepdims=True))
        p = p / jnp.sum(p, axis=-1, keepdims=True)
        outs.append(_dot(p.astype(BF16), vh))
    o = jnp.concatenate(outs, axis=-1).astype(BF16)
    o_ref[...] = x + _dot(o, wo_ref[...])


def mix_out_mem_attend(y, w_mix, mix_layer, h, gains, w_q, kv, w_o, layer, *, heads, hd, bm):
    B, T, D = h.shape
    Ky = y.shape[-1]
    Mt = kv.shape[2]
    width = heads * hd
    bm = _tile(T, bm, SUBLANES)
    return pl.pallas_call(
        functools.partial(_mem_body, heads=heads, hd=hd),
        grid=(B, T // bm),
        in_specs=[pl.BlockSpec((None, bm, Ky), lambda b, i: (b, i, 0)),
                  pl.BlockSpec((None, Ky, D), lambda b, i: (mix_layer, 0, 0), pipeline_mode=pl.Buffered(1)),
                  pl.BlockSpec((None, bm, D), lambda b, i: (b, i, 0)),
                  pl.BlockSpec((None, 1, D), lambda b, i: (layer, 0, 0)),
                  pl.BlockSpec((None, D, width), lambda b, i: (layer, 0, 0), pipeline_mode=pl.Buffered(1)),
                  pl.BlockSpec((None, None, Mt, 2 * width), lambda b, i: (layer, b, 0, 0)),
                  pl.BlockSpec((None, width, D), lambda b, i: (layer, 0, 0), pipeline_mode=pl.Buffered(1))],
        out_specs=pl.BlockSpec((None, bm, D), lambda b, i: (b, i, 0)),
        out_shape=jax.ShapeDtypeStruct((B, T, D), F32),
        compiler_params=_cparams(2),
        name="mix_out_mem_attend",
    )(y, w_mix, h, gains, w_q, kv, w_o)


def _mem_sample_body(y_ref, wm_ref, h_ref, g_ref, wq_ref, kv_ref, wo_ref, o_ref, *, heads, hd):
    batch = h_ref.shape[0]
    width = heads * hd
    scale = hd ** -0.5
    x = h_ref[...] + _dot(y_ref[...].astype(BF16), wm_ref[...])
    q = _dot(_rms(x, g_ref[...]).astype(BF16), wq_ref[...])
    row = lax.broadcasted_iota(jnp.int32, (SUBLANES, width), 0)
    lane = lax.broadcasted_iota(jnp.int32, (SUBLANES, width), 1)
    own_head = jnp.logical_and(lane >= row * hd, lane < (row + 1) * hd)
    outs = []
    for b in range(batch):
        qb = jnp.where(own_head, jnp.broadcast_to(q[b:b + 1], (SUBLANES, width)), 0.0).astype(BF16)
        s = _dot_nt(qb, kv_ref[b, :, :width].astype(BF16)) * scale
        p = jnp.exp(s - jnp.max(s, axis=-1, keepdims=True))
        p = p / jnp.sum(p, axis=-1, keepdims=True)
        ob = _dot(p.astype(BF16), kv_ref[b, :, width:].astype(BF16))
        outs.append(jnp.sum(jnp.where(own_head, ob, 0.0), axis=0, keepdims=True))
    o = jnp.concatenate(outs, axis=0).astype(BF16)
    o_ref[...] = x + _dot(o, wo_ref[...])


def mix_out_mem_attend_sample(y, w_mix, mix_layer, h, gains, w_q, kv, w_o, layer, *, heads, hd):
    B, D = h.shape
    Ky = y.shape[-1]
    Mt = kv.shape[2]
    width = heads * hd
    assert heads <= SUBLANES
    one = pl.Buffered(1)
    return pl.pallas_call(
        functools.partial(_mem_sample_body, heads=heads, hd=hd),
        grid=(1,),
        in_specs=[pl.BlockSpec((B, Ky), lambda i: (0, 0)),
                  pl.BlockSpec((None, Ky, D), lambda i: (mix_layer, 0, 0), pipeline_mode=one),
                  pl.BlockSpec((B, D), lambda i: (0, 0)),
                  pl.BlockSpec((None, 1, D), lambda i: (layer, 0, 0)),
                  pl.BlockSpec((None, D, width), lambda i: (layer, 0, 0), pipeline_mode=one),
                  pl.BlockSpec((None, B, Mt, 2 * width), lambda i: (layer, 0, 0, 0), pipeline_mode=one),
                  pl.BlockSpec((None, width, D), lambda i: (layer, 0, 0), pipeline_mode=one)],
        out_specs=pl.BlockSpec((B, D), lambda i: (0, 0)),
        out_shape=jax.ShapeDtypeStruct((B, D), F32),
        compiler_params=_cparams(1),
        name="mix_out_mem_attend_sample",
    )(y, w_mix, h, gains, w_q, kv, w_o)


def t5_bucket(dist):
    dist = np.asarray(dist, np.int64)
    max_exact = N_BUCKETS // 2
    d = np.maximum(dist, 1).astype(np.float64)
    large = max_exact + (np.log(d / max_exact) / math.log(MAX_DISTANCE / max_exact)
                         * (N_BUCKETS - max_exact)).astype(np.int64)
    return np.where(dist < max_exact, dist, np.minimum(large, N_BUCKETS - 1)).astype(np.int32)


ATTN_UNROLL = 4


def _unroll(trip):
    return max(u for u in range(1, ATTN_UNROLL + 2) if trip % u == 0)


def _attn_block(q, k2, v2, bias2, scale):
    blk = q.shape[0]
    s = _dot_nt(q.astype(BF16), k2.astype(BF16)) * scale + bias2
    m = jnp.max(jnp.maximum(s[:, :blk], s[:, blk:]), axis=-1, keepdims=True)
    p = jnp.exp(s - m)
    l = jnp.sum(p[:, :blk] + p[:, blk:], axis=-1, keepdims=True)
    o = _dot(p.astype(BF16), v2.astype(BF16))
    return o / l, m + jnp.log(l)


def _dil_attn_body(*refs, dils, tile, hd):
    n_groups = len(dils)
    bias_ref, o_ref, og_ref, lse_ref = refs[5 * n_groups:]
    not_first_tile = pl.program_id(1) > 0
    blk = A_BLOCK
    scale = hd ** -0.5

    for g, d in enumerate(dils):
        q_ref, kc_ref, vc_ref, kp_ref, vp_ref = refs[5 * g:5 * g + 5]
        stream_rows = blk * d
        n_blocks = tile // stream_rows

        def rows(start, d=d):
            if d == 1:
                return pl.ds(pl.multiple_of(start, blk), blk)
            return pl.ds(start, blk, stride=d)

        bias2 = bias_ref[g]
        bias2_first = jnp.concatenate(
            [jnp.where(not_first_tile, bias2[:, :blk], -jnp.inf), bias2[:, blk:]], axis=1)

        def run(starts, prev_of, bias, g=g, q_ref=q_ref, kc_ref=kc_ref, vc_ref=vc_ref, rows=rows):
            results = []
            for s in starts:
                kp, vp = prev_of(s)
                k2 = jnp.concatenate([kp, kc_ref[rows(s), :]], axis=0)
                v2 = jnp.concatenate([vp, vc_ref[rows(s), :]], axis=0)
                results.append(_attn_block(q_ref[rows(s), :], k2, v2, bias, scale))
            for s, (o, lse) in zip(starts, results):
                og_ref[g, rows(s), :] = o
                lse_ref[g, rows(s), :] = jnp.broadcast_to(lse, (blk, hd))

        def first(it, carry, run=run, rows=rows, kp_ref=kp_ref, vp_ref=vp_ref, u=_unroll(d)):
            run([it * u + k for k in range(u)], lambda s: (kp_ref[rows(s), :], vp_ref[rows(s), :]), bias2_first)
            return carry

        n_rest = d * (n_blocks - 1)

        def rest(it, carry, run=run, rows=rows, kc_ref=kc_ref, vc_ref=vc_ref, d=d, stream_rows=stream_rows,
                 u=_unroll(n_rest)):
            idx = [it * u + k for k in range(u)]
            run([i % d + (1 + i // d) * stream_rows for i in idx],
                lambda s: (kc_ref[rows(s - stream_rows), :], vc_ref[rows(s - stream_rows), :]), bias2)
            return carry

        lax.fori_loop(0, d // _unroll(d), first, 0)
        if n_blocks > 1:
            lax.fori_loop(0, n_rest // _unroll(n_rest), rest, 0)

    lses = [lse_ref[g] for g in range(n_groups)]
    mx = functools.reduce(jnp.maximum, lses)
    es = [jnp.exp(l - mx) for l in lses]
    den = functools.reduce(lambda a, b: a + b, es)
    acc = (es[0] / den) * og_ref[0]
    for g in range(1, n_groups):
        acc = acc + (es[g] / den) * og_ref[g]
    o_ref[...] = acc


def dilated_attention_prompt(qkv, bias, *, heads, hd, tile):
    B, T, _ = qkv.shape
    dils = A_DILATIONS
    n_groups = len(dils)
    tile = min(tile, T)
    assert hd == LANES and T % tile == 0 and all(tile % (A_BLOCK * d) == 0 for d in dils)
    sec = n_groups * heads

    def cur(which, g):
        return pl.BlockSpec((None, tile, hd), lambda b, n, h: (b, n, which * sec + g * heads + h))

    def prev(which, g):
        rows = A_BLOCK * dils[g]
        per = tile // rows
        return pl.BlockSpec((None, rows, hd),
                            lambda b, n, h: (b, jnp.maximum(n * per - 1, 0), which * sec + g * heads + h))

    in_specs = []
    for g in range(n_groups):
        in_specs += [cur(0, g), cur(1, g), cur(2, g), prev(1, g), prev(2, g)]
    in_specs.append(pl.BlockSpec((n_groups, None, A_BLOCK, 2 * A_BLOCK), lambda b, n, h: (0, h, 0, 0)))
    return pl.pallas_call(
        functools.partial(_dil_attn_body, dils=dils, tile=tile, hd=hd),
        grid=(B, T // tile, heads),
        in_specs=in_specs,
        out_specs=pl.BlockSpec((None, tile, hd), lambda b, n, h: (b, n, h)),
        out_shape=jax.ShapeDtypeStruct((B, T, heads * hd), F32),
        scratch_shapes=[pltpu.VMEM((n_groups, tile, hd), F32), pltpu.VMEM((n_groups, tile, hd), F32)],
        compiler_params=_cparams(3),
        name="dilated_attention_prompt",
    )(*([qkv] * (5 * n_groups)), bias)


def _dec_attn_body(qkv_ref, *refs, groups, span, scale):
    bufs, bias_ref, o_ref = refs[:groups], refs[groups], refs[groups + 1]
    outs, lses = [], []
    for g in range(groups):
        q = _bf_round(qkv_ref[g])
        kn = _bf_round(qkv_ref[groups + g])
        vn = _bf_round(qkv_ref[2 * groups + g])
        k = _bf_round(bufs[g][:, 0])
        v = _bf_round(bufs[g][:, 1])
        s = jnp.sum(k * q[None], axis=-1, keepdims=True) * scale + bias_ref[g, :span]
        s_self = jnp.sum(q * kn, axis=-1, keepdims=True) * scale + bias_ref[g, span]
        m = jnp.maximum(jnp.max(s, axis=0), s_self)
        p = jnp.exp(s - m[None])
        p_self = jnp.exp(s_self - m)
        l = jnp.sum(p, axis=0) + p_self
        outs.append(jnp.sum(_bf_round(p / l[None]) * v, axis=0) + _bf_round(p_self / l) * vn)
        lses.append(m + jnp.log(l))
    mx = functools.reduce(jnp.maximum, lses)
    es = [jnp.exp(l - mx) for l in lses]
    den = functools.reduce(lambda a, b: a + b, es)
    acc = (es[0] / den) * outs[0]
    for g in range(1, groups):
        acc = acc + (es[g] / den) * outs[g]
    o_ref[...] = acc


def dilated_attention_sample(qkv, caches, layer, bias, *, heads, hd, span):
    B = qkv.shape[0]
    groups = len(caches)
    in_specs = [pl.BlockSpec((None, 3 * groups, heads, hd), lambda b: (b, 0, 0, 0))]
    views = []
    for g, c in enumerate(caches):
        d = A_DILATIONS[g]
        assert c.shape[2] == span * d
        views.append(c.reshape(c.shape[0], B, span, d, 2, heads, hd))
        in_specs.append(pl.BlockSpec((None, None, span, None, 2, heads, hd),
                                     lambda b: (layer, b, 0, 0, 0, 0, 0)))
    in_specs.append(pl.BlockSpec((groups, span + 1, heads, hd), lambda b: (0, 0, 0, 0)))
    return pl.pallas_call(
        functools.partial(_dec_attn_body, groups=groups, span=span, scale=hd ** -0.5),
        grid=(B,),
        in_specs=in_specs,
        out_specs=pl.BlockSpec((None, heads, hd), lambda b: (b, 0, 0)),
        out_shape=jax.ShapeDtypeStruct((B, heads, hd), F32),
        compiler_params=_cparams(1),
        name="dilated_attention_sample",
    )(qkv, *views, bias)


def _to_window_body(*refs, heads, hd):
    o_ref = refs[-1]
    chunk = refs[0].shape[0]
    for layer in range(o_ref.shape[0]):
        for s in range(2):
            src = refs[2 * layer + s]
            for h in range(heads):
                o_ref[layer, pl.ds(s * heads + h, chunk, stride=2 * heads), :] = src[:, h * hd:(h + 1) * hd]


def to_window(qkvs, g, *, rows, groups, heads, hd):
    B, T, _ = qkvs[0].shape
    W = heads * hd
    chunk = _tile(rows, 256, SUBLANES)
    assert (T - rows) % chunk == 0
    first = (T - rows) // chunk
    in_specs, args = [], []
    for qkv in qkvs:
        in_specs += [pl.BlockSpec((None, chunk, W), lambda b, c: (b, first + c, groups + g)),
                     pl.BlockSpec((None, chunk, W), lambda b, c: (b, first + c, 2 * groups + g))]
        args += [qkv, qkv]
    per_token = 2 * heads
    out = pl.pallas_call(
        functools.partial(_to_window_body, heads=heads, hd=hd),
        grid=(B, rows // chunk),
        in_specs=in_specs,
        out_specs=pl.BlockSpec((len(qkvs), None, chunk * per_token, hd), lambda b, c: (0, b, c, 0)),
        out_shape=jax.ShapeDtypeStruct((len(qkvs), B, rows * per_token, hd), F32),
        compiler_params=_cparams(2),
        name="to_window",
    )(*args)
    return out.reshape(len(qkvs), B, rows, 2, heads, hd)


def _window_update_body(main_ref, next_ref, new_ref, o_ref):
    rows = main_ref.shape[0]
    last_chunk = pl.program_id(2) == pl.num_programs(2) - 1
    o_ref[0:rows - 1] = main_ref[1:rows]
    o_ref[rows - 1:rows] = jnp.where(last_chunk, new_ref[...], next_ref[...])


def window_update(cache, new, *, chunk):
    n_layers, B, rows = cache.shape[:3]
    tail = cache.shape[3:]
    chunk = _tile(rows, chunk, 1)
    zeros = (0,) * len(tail)
    return pl.pallas_call(
        _window_update_body,
        grid=(n_layers, B, rows // chunk),
        in_specs=[pl.BlockSpec((None, None, chunk) + tail, lambda j, b, c: (j, b, c) + zeros),
                  pl.BlockSpec((None, None, 1) + tail,
                               lambda j, b, c: (j, b, jnp.minimum((c + 1) * chunk, rows - 1)) + zeros),
                  pl.BlockSpec((None, None, 1) + tail, lambda j, b, c: (j, b, 0) + zeros)],
        out_specs=pl.BlockSpec((None, None, chunk) + tail, lambda j, b, c: (j, b, c) + zeros),
        out_shape=jax.ShapeDtypeStruct(cache.shape, cache.dtype),
        compiler_params=_cparams(3),
        name="window_update",
    )(cache, cache, new)


def _gla_gate_body(x_ref, g_ref, wl_ref, wg_ref, b_ref, o_ref):
    xn = _rms(x_ref[...], g_ref[...]).astype(BF16)
    low = _dot(xn, wl_ref[...].astype(BF16))
    z = _dot(low.astype(BF16), wg_ref[...].astype(BF16)) + b_ref[...]
    log_sig = jnp.minimum(z, 0.0) - jnp.log(1.0 + jnp.exp(-jnp.abs(z)))
    o_ref[...] = log_sig / GLA_GATE_NORMALIZER


def gla_gate(x, gain, w_low, w_gate, b_gate, layer, *, bm):
    M, D = x.shape
    g_arr, g_idx = gain
    rank, dk = w_gate.shape[1], w_gate.shape[2]
    bm = _tile(M, bm, SUBLANES)
    return pl.pallas_call(
        _gla_gate_body,
        grid=(M // bm,),
        in_specs=[pl.BlockSpec((bm, D), lambda i: (i, 0)),
                  pl.BlockSpec((None, 1, D), lambda i: (g_idx, 0, 0)),
                  pl.BlockSpec((D, rank), lambda i: (0, 0)),
                  pl.BlockSpec((None, rank, dk), lambda i: (layer, 0, 0)),
                  pl.BlockSpec((None, 1, dk), lambda i: (layer, 0, 0))],
        out_specs=pl.BlockSpec((bm, dk), lambda i: (i, 0)),
        out_shape=jax.ShapeDtypeStruct((M, dk), F32),
        compiler_params=_cparams(1),
        name="gla_gate",
    )(x, g_arr, w_low, w_gate, b_gate)


def _cumsum_rows(x):
    rows = x.shape[0]
    row = lax.broadcasted_iota(jnp.int32, x.shape, 0)
    s = 1
    while s < rows:
        x = x + jnp.where(row >= s, pltpu.roll(x, s, axis=0), 0.0)
        s *= 2
    return x


def _block_refs(bcs, m, pairs):
    rows, dk = bcs.shape
    nb = rows // m
    ends = [bcs[(k + 1) * m - 1:(k + 1) * m, :] for k in range(nb)]
    parts = []
    for k in range(nb):
        if pairs:
            r = ends[k - 1] if k % 2 == 1 else ends[k]
        else:
            r = ends[k - 1] if k > 0 else jnp.zeros((1, dk), F32)
        parts.append(jnp.broadcast_to(r, (m, dk)))
    return parts[0] if nb == 1 else jnp.concatenate(parts, axis=0)


def _la_body(a_ref, b_ref, c_ref, gate_ref, e_ref, gn_ref, s0_ref, y_ref, so_ref, st_ref,
             *, hgrn2, heads, dk, dv, chunk, base, valid, q_scale, head_batch):
    n = pl.program_id(1)

    @pl.when(n == 0)
    def _():
        for h in range(heads):
            st_ref[h] = s0_ref[h].T

    tq = lax.broadcasted_iota(jnp.int32, (chunk, chunk), 0)
    ts = lax.broadcasted_iota(jnp.int32, (chunk, chunk), 1)
    row = lax.broadcasted_iota(jnp.int32, (chunk, 1), 0)
    sh = int(math.log2(base))
    mask_diag = jnp.logical_and((tq >> sh) == (ts >> sh), ts <= tq)
    levels = []
    m = base
    while m < chunk:
        s1 = int(math.log2(m))
        mk = jnp.logical_and((tq >> (s1 + 1)) == (ts >> (s1 + 1)),
                             jnp.logical_and(((tq >> s1) & 1) == 1, ((ts >> s1) & 1) == 0))
        levels.append((m, mk))
        m *= 2
    gn = gn_ref[...]

    def one_head(h):
        ksl = slice(h * dk, (h + 1) * dk)
        vsl = slice(h * dv, (h + 1) * dv)
        if hgrn2:
            a = a_ref[:, ksl]
            lb = e_ref[:, ksl]
            f = lb + (1.0 - lb) * jax.nn.sigmoid(b_ref[:, ksl])
            qa = a * jax.nn.sigmoid(a)
            kk = 1.0 - f
            g = jnp.log(f)
        else:
            qa = a_ref[:, ksl] * q_scale
            kk = b_ref[:, ksl]
            g = e_ref[:, ksl]
        v = c_ref[:, vsl]
        if valid < chunk:
            kk = jnp.where(row < valid, kk, 0.0)
            g = jnp.where(row < valid, g, 0.0)
        bcs = _cumsum_rows(g)
        vb = v.astype(BF16)
        st = st_ref[h]

        o = _dot_nt((qa * jnp.exp(bcs)).astype(BF16), st.astype(BF16))

        ref_d = _block_refs(bcs, base, False)
        a_mat = _dot_nt((qa * jnp.exp(bcs - ref_d)).astype(BF16), (kk * jnp.exp(ref_d - bcs)).astype(BF16))
        a_mat = jnp.where(mask_diag, a_mat, 0.0)
        for (m, mk) in levels:
            dec = jnp.exp(-jnp.abs(bcs - _block_refs(bcs, m, True)))
            a_mat = a_mat + jnp.where(mk, _dot_nt((qa * dec).astype(BF16), (kk * dec).astype(BF16)), 0.0)
        o = o + _dot(a_mat.astype(BF16), vb)

        b_last = bcs[chunk - 1:chunk, :]
        kd = (kk * jnp.exp(b_last - bcs)).astype(BF16)
        st_new = jnp.exp(b_last) * st + _dot_tn(vb, kd)

        gate = gate_ref[:, vsl]
        return st_new, _rms(o, gn) * (gate * jax.nn.sigmoid(gate))

    for h0 in range(0, heads, head_batch):
        hs = range(h0, min(h0 + head_batch, heads))
        results = [one_head(h) for h in hs]
        for h, (st_new, y) in zip(hs, results):
            st_ref[h] = st_new
            y_ref[:, h * dv:(h + 1) * dv] = y

    @pl.when(n == pl.num_programs(1) - 1)
    def _():
        for h in range(heads):
            so_ref[h] = st_ref[h].T


def gated_linear_attention(proj, extra, gnorm, s0, *, hgrn2, chunk, valid=None):
    B, T, _ = proj.shape
    _, H, dk, dv = s0.shape
    chunk = min(chunk, T)
    assert T % chunk == 0
    base = min(LA_BASE, chunk)
    valid = chunk if valid is None else valid
    wk, wv = H * dk, H * dv

    def col(width, start):
        assert start % width == 0
        return pl.BlockSpec((None, chunk, width), lambda b, n: (b, n, start // width))

    if hgrn2:
        specs = [col(wk, 0), col(wk, wk), col(wv, 2 * wk), col(wv, 2 * wk + wv),
                 pl.BlockSpec((1, wk), lambda b, n: (0, 0))]
    else:
        specs = [col(wk, 0), col(wk, wk), col(wv, 2 * wk), col(wv, 2 * wk + wv),
                 pl.BlockSpec((None, chunk, wk), lambda b, n: (b, n, 0))]
    specs += [pl.BlockSpec((1, dv), lambda b, n: (0, 0)),
              pl.BlockSpec((None, H, dk, dv), lambda b, n: (b, 0, 0, 0))]
    return pl.pallas_call(
        functools.partial(_la_body, hgrn2=hgrn2, heads=H, dk=dk, dv=dv, chunk=chunk, base=base,
                          valid=valid, q_scale=1.0 if hgrn2 else dk ** -0.5,
                          head_batch=max(1, LA_HEAD_BATCH_LANES // max(dk, dv))),
        grid=(B, T // chunk),
        in_specs=specs,
        out_specs=[pl.BlockSpec((None, chunk, wv), lambda b, n: (b, n, 0)),
                   pl.BlockSpec((None, H, dk, dv), lambda b, n: (b, 0, 0, 0))],
        out_shape=[jax.ShapeDtypeStruct((B, T, wv), F32),
                   jax.ShapeDtypeStruct((B, H, dk, dv), F32)],
        scratch_shapes=[pltpu.VMEM((H, dv, dk), F32)],
        compiler_params=_cparams(2),
        name="gated_linear_attention",
    )(proj, proj, proj, proj, extra, gnorm, s0)


def _pad_rows(x, rows):
    return jnp.pad(x, ((0, 0), (0, rows - x.shape[1]), (0, 0)))


def _trunk(xp, xs, p_mem, s_mem, caches, b_states, c_states, P):
    Bp, T, D = xp.shape
    Bs = xs.shape[0]
    assert xs.shape[1] == 1 and Bs <= RIDER_ROWS
    Mp = Bp * T
    depth = P["norm_mix"].shape[0]
    hd_a, hpg = P["a_hd"], P["a_hpg"]
    groups = len(A_DILATIONS)
    Wa = hpg * hd_a
    tile = dict(bm=2048, bn=512)
    mlp_tile = dict(bm=1024, bf=512)

    def ride(h):
        return jnp.pad(h, ((0, RIDER_ROWS - Bs), (0, 0)))

    hp, hs = xp.reshape(Mp, D), xs.reshape(Bs, D)
    qkv_prompt, new_rows, p_b, s_b, p_c, s_c = [], [], [], [], [], []
    for li in range(depth):
        kind, j = li % N_MIXERS, li // N_MIXERS
        gmix = (P["norm_mix"], li)
        if kind == 0:
            qkv_p, qkv_s = linear(hp, P["a_w_in"], j, gain=gmix, rider=ride(hs), **tile)
            qkv3 = qkv_p.reshape(Bp, T, 3 * groups * Wa)
            yp = dilated_attention_prompt(qkv3, P["a_bias_prompt"], heads=hpg, hd=hd_a, tile=16 * A_BLOCK)
            qkv_prompt.append(qkv3)
            qkv5 = qkv_s[:Bs].reshape(Bs, 3, groups, hpg, hd_a)
            ys = dilated_attention_sample(qkv5.reshape(Bs, 3 * groups, hpg, hd_a), caches, j,
                                          P["a_bias_sample"], heads=hpg, hd=hd_a, span=A_BLOCK)
            new_rows.append([qkv5[:, 1:3, g][:, None] for g in range(groups)])
            ys = ys.reshape(Bs, Wa)
            w_mix = P["a_w_out"]
        elif kind == 1:
            proj_p, proj_s = linear(hp, P["b_w_in"], j, gain=gmix, rider=ride(hs), **tile)
            lb = P["lower_bounds"][li][None, :]
            gn = P["b_gnorm"][j][None, :]
            yp, S = gated_linear_attention(proj_p.reshape(Bp, T, -1), lb, gn,
                                           jnp.zeros((Bp,) + b_states.shape[2:], F32), hgrn2=True, chunk=LA_CHUNK)
            p_b.append(S)
            ys, S = gated_linear_attention(_pad_rows(proj_s[:Bs, None], SUBLANES), lb, gn, b_states[j],
                                           hgrn2=True, chunk=SUBLANES, valid=1)
            s_b.append(S)
            w_mix = P["b_w_out"]
        else:
            wk = P["c_wk"]
            n_main = P["c_n_main"]
            proj_p, proj_s = linear(hp, P["c_w_in_t"], j, n_out=n_main, gain=gmix, rider=ride(hs),
                                    transposed=True, **tile)
            gate_args = (gmix, P["c_w_low"][j], P["c_w_gate"], P["c_b_gate"], j)
            gn = P["c_gnorm"][j][None, :]
            yp, S = gated_linear_attention(proj_p.reshape(Bp, T, n_main),
                                           gla_gate(hp, *gate_args, bm=1024).reshape(Bp, T, wk), gn,
                                           jnp.zeros((Bp,) + c_states.shape[2:], F32), hgrn2=False, chunk=LA_CHUNK)
            p_c.append(S)
            ys, S = gated_linear_attention(_pad_rows(proj_s[:Bs, None], SUBLANES),
                                           _pad_rows(gla_gate(hs, *gate_args, bm=SUBLANES)[:, None], SUBLANES), gn,
                                           c_states[j], hgrn2=False, chunk=SUBLANES, valid=1)
            s_c.append(S)
            w_mix = P["c_w_out"]

        mem_kw = dict(heads=P["m_heads"], hd=P["m_hd"])
        hp = mix_out_mem_attend(yp.reshape(Bp, T, -1), w_mix, j, hp.reshape(Bp, T, D), P["norm_mem"], P["mem_w_q"],
                                p_mem, P["mem_w_out"], li, bm=512, **mem_kw).reshape(Mp, D)
        if ys.ndim == 3:
            ys = ys[:, 0]
        hs = mix_out_mem_attend_sample(ys, w_mix, j, hs, P["norm_mem"], P["mem_w_q"], s_mem, P["mem_w_out"], li,
                                       **mem_kw)
        final = P["norm_final"] if li == depth - 1 else None
        hp, hs = mlp(hp, P["norm_mlp"], P["mlp_w_up"], P["mlp_w_down"], li, final_gain=final,
                     rider=ride(hs), **mlp_tile)
        hs = hs[:Bs]

    new_rows = [jnp.stack([rows[g] for rows in new_rows], axis=0) for g in range(groups)]
    p_win = [to_window(qkv_prompt, g, rows=min(A_WINDOWS[g], T), groups=groups, heads=hpg, hd=hd_a)
             for g in range(groups)]
    return (hp.reshape(Bp, T, D), hs.reshape(Bs, 1, D), p_win, new_rows,
            jnp.stack(p_b, axis=0), jnp.stack(s_b, axis=0), jnp.stack(p_c, axis=0), jnp.stack(s_c, axis=0))


def _bias_tables(rel_bias_table, hpg, span):
    steps = np.arange(A_BLOCK)[:, None] + A_BLOCK - np.arange(2 * A_BLOCK)[None, :]
    eye = np.eye(N_BUCKETS, dtype=np.float32)
    prompt, sample = [], []
    for g, d in enumerate(A_DILATIONS):
        tg_t = rel_bias_table[:, g * hpg:(g + 1) * hpg].T
        onehot = eye[:, t5_bucket(np.clip(steps, 0, span) * d).reshape(-1)]
        bias = jnp.dot(tg_t, jnp.asarray(onehot), precision=lax.Precision.HIGHEST).reshape(hpg, A_BLOCK, 2 * A_BLOCK)
        band = jnp.asarray((steps >= 0) & (steps <= span))
        prompt.append(jnp.where(band[None], bias, -jnp.inf))
        dist = np.concatenate([(span - np.arange(span)) * d, np.zeros(1, np.int64)])
        row = jnp.dot(tg_t, jnp.asarray(eye[:, t5_bucket(dist)]), precision=lax.Precision.HIGHEST)
        sample.append(jnp.broadcast_to(row.T[:, :, None], (span + 1, hpg, LANES)))
    return jnp.stack(prompt, axis=0), jnp.stack(sample, axis=0)


def kernel(x_prompt, x_sample, cache_win0, cache_win1, cache_win2, state_hgrn, state_gla, cache_mem, mem_prompt, rel_bias_table, a_w_in, a_w_out, b_w_in, b_w_out, b_lower_bound, b_gnorm, c_w_in, c_w_gate, c_b_gate, c_w_out, c_gnorm, mem_norm, mem_w_kv, mem_w_q, mem_w_out, norm_mix, norm_mem, norm_mlp, mlp_w_up, mlp_w_down, norm_final):
    depth, D = norm_mix.shape
    hpg, hd_a = cache_win0.shape[-2], cache_win0.shape[-1]
    span = A_BLOCK
    assert all(w // d == span for w, d in zip(A_WINDOWS, A_DILATIONS))
    m_heads, m_hd = cache_mem.shape[-2], cache_mem.shape[-1]
    c_heads, c_dk, c_dv = state_gla.shape[2:]
    wk = c_heads * c_dk
    n_main = 2 * wk + 2 * c_heads * c_dv
    bias_prompt, bias_sample = _bias_tables(rel_bias_table, hpg, span)

    sm = jax.nn.softmax(b_lower_bound.astype(F32), axis=0)
    P = dict(
        a_hd=hd_a, a_hpg=hpg, m_heads=m_heads, m_hd=m_hd,
        a_bias_prompt=bias_prompt, a_bias_sample=bias_sample,
        a_w_in=a_w_in, a_w_out=a_w_out.astype(BF16), b_w_in=b_w_in, b_w_out=b_w_out.astype(BF16),
        lower_bounds=jnp.cumsum(sm, axis=0) - sm, b_gnorm=b_gnorm,
        c_w_in_t=jnp.swapaxes(c_w_in, 1, 2), c_w_low=c_w_in[:, :, n_main:], c_w_gate=c_w_gate,
        c_b_gate=c_b_gate[:, None, :], c_w_out=c_w_out.astype(BF16), c_gnorm=c_gnorm,
        c_wk=wk, c_n_main=n_main,
        mem_w_q=mem_w_q.astype(BF16), mem_w_out=mem_w_out.astype(BF16),
        norm_mix=norm_mix[:, None, :], norm_mem=norm_mem[:, None, :], norm_mlp=norm_mlp[:, None, :],
        mlp_w_up=mlp_w_up, mlp_w_down=mlp_w_down, norm_final=norm_final[None, :],
    )

    Bm, Mt, _ = mem_prompt.shape
    mem_flat = mem_prompt.reshape(Bm * Mt, D)
    mem_gain = mem_norm[:, None, :]
    p_mem = jnp.stack([linear(mem_flat, mem_w_kv, li, gain=(mem_gain, li), bm=512, bn=512)
                       for li in range(depth)], axis=0).reshape(depth, Bm, Mt, 2 * m_heads * m_hd)

    caches = (cache_win0, cache_win1, cache_win2)
    y_prompt, y_sample, p_win, new_rows, p_hgrn, s_hgrn, p_gla, s_gla = _trunk(
        x_prompt, x_sample, p_mem, cache_mem.reshape(cache_mem.shape[:3] + (-1,)), caches, state_hgrn, state_gla, P)
    s_win = [window_update(c, n, chunk=256) for c, n in zip(caches, new_rows)]
    return (y_prompt, y_sample, p_win[0], p_win[1], p_win[2], p_hgrn, p_gla,
            p_mem.reshape(depth, Bm, Mt, 2, m_heads, m_hd),
            s_win[0], s_win[1], s_win[2], s_hgrn, s_gla)
```

```python
import functools
import math

import numpy as np
import jax
import jax.numpy as jnp
from jax import lax
from jax.experimental import pallas as pl
from jax.experimental.pallas import tpu as pltpu

F32 = jnp.float32
BF16 = jnp.bfloat16

EPS = 1e-6
N_MIXERS = 3
A_WINDOWS = (128, 512, 2048)
A_DILATIONS = (1, 4, 16)
A_BLOCK = 128
N_BUCKETS = 32
MAX_DISTANCE = 2048
GLA_GATE_NORMALIZER = 16.0
LA_BASE = 16
LA_HEAD_BATCH_LANES = 512
LA_CHUNK = 128
RIDER_ROWS = 16

V7X_VMEM_BYTES = 64 * 1024 * 1024
VMEM_LIMIT = V7X_VMEM_BYTES - 8 * 1024 * 1024
SUBLANES = 8
LANES = 128


def _cparams(n_axes):
    return pltpu.CompilerParams(dimension_semantics=("arbitrary",) * n_axes,
                                vmem_limit_bytes=VMEM_LIMIT)


def _tile(n, target, align):
    if n <= target:
        return n
    t = (target // align) * align
    while n % t:
        t -= align
    return t


def _rms(x, g):
    r = lax.rsqrt(jnp.mean(x * x, axis=-1, keepdims=True) + EPS)
    return (x * r) * g


def _dot(a, b):
    return jnp.dot(a, b, preferred_element_type=F32)


def _dot_nt(a, b):
    return lax.dot_general(a, b, (((1,), (1,)), ((), ())), preferred_element_type=F32)


def _dot_tn(a, b):
    return lax.dot_general(a, b, (((0,), (0,)), ((), ())), preferred_element_type=F32)


def _bf_round(a):
    return a.astype(BF16).astype(F32)


def _linear_body(*refs, riders, transposed):
    if riders:
        x_ref, w_ref, g_ref, r_ref, o_ref, or_ref, xs_ref = refs
    else:
        x_ref, w_ref, g_ref, o_ref, xs_ref = refs
    bm = x_ref.shape[0]

    @pl.when(pl.program_id(1) == 0)
    def _():
        xs_ref[0:bm] = _rms(x_ref[...], g_ref[...]).astype(BF16)
        if riders:
            xs_ref[bm:bm + riders] = _rms(r_ref[...], g_ref[...]).astype(BF16)

    w = w_ref[...].astype(BF16)
    acc = _dot_nt(xs_ref[...], w) if transposed else _dot(xs_ref[...], w)
    if riders:
        o_ref[...] = acc[:bm]
        or_ref[...] = acc[bm:]
    else:
        o_ref[...] = acc


def linear(x, w, layer, *, n_out=None, gain, rider=None, transposed=False, bm, bn):
    M, K = x.shape
    N = (w.shape[1] if transposed else w.shape[2]) if n_out is None else n_out
    bm, bn = _tile(M, bm, SUBLANES), _tile(N, bn, LANES)
    riders = 0 if rider is None else rider.shape[0]
    g_arr, g_idx = gain
    w_spec = (pl.BlockSpec((None, bn, K), lambda i, j: (layer, j, 0)) if transposed
              else pl.BlockSpec((None, K, bn), lambda i, j: (layer, 0, j)))
    in_specs = [pl.BlockSpec((bm, K), lambda i, j: (i, 0), pipeline_mode=pl.Buffered(1)),
                w_spec,
                pl.BlockSpec((None, 1, K), lambda i, j: (g_idx, 0, 0))]
    args = [x, w, g_arr]
    out_specs = [pl.BlockSpec((bm, bn), lambda i, j: (i, j))]
    out_shape = [jax.ShapeDtypeStruct((M, N), F32)]
    if riders:
        in_specs.append(pl.BlockSpec((riders, K), lambda i, j: (0, 0)))
        args.append(rider)
        out_specs.append(pl.BlockSpec((None, riders, bn), lambda i, j: (i, 0, j)))
        out_shape.append(jax.ShapeDtypeStruct((M // bm, riders, N), F32))
    outs = pl.pallas_call(
        functools.partial(_linear_body, riders=riders, transposed=transposed),
        grid=(M // bm, N // bn),
        in_specs=in_specs,
        out_specs=out_specs,
        out_shape=out_shape,
        scratch_shapes=[pltpu.VMEM((bm + riders, K), BF16)],
        compiler_params=_cparams(2),
        name="linear",
    )(*args)
    return (outs[0], outs[1][0]) if riders else outs[0]


def _mlp_body(*refs, has_final, riders):
    refs = list(refs)
    x_ref, g_ref, wu_ref, wd_ref = refs[:4]
    pos = 4
    fg_ref = r_ref = or_ref = None
    if has_final:
        fg_ref = refs[pos]
        pos += 1
    if riders:
        r_ref = refs[pos]
        pos += 1
    o_ref = refs[pos]
    if riders:
        or_ref = refs[pos + 1]
    xs_ref = refs[-1]
    bm = x_ref.shape[0]
    f = pl.program_id(1)

    @pl.when(f == 0)
    def _():
        x = x_ref[...]
        xs_ref[0:bm] = _rms(x, g_ref[...]).astype(BF16)
        o_ref[...] = x
        if riders:
            r = r_ref[...]
            xs_ref[bm:bm + riders] = _rms(r, g_ref[...]).astype(BF16)
            or_ref[...] = r

    def up_down(xs):
        hdn = _dot(xs, wu_ref[...].astype(BF16))
        hdn = jnp.square(jnp.maximum(hdn, 0.0)).astype(BF16)
        return _dot(hdn, wd_ref[...].astype(BF16))

    if riders:
        @pl.when(pl.program_id(0) == 0)
        def _():
            down = up_down(xs_ref[...])
            o_ref[...] += down[:bm]
            or_ref[...] += down[bm:]

        @pl.when(pl.program_id(0) != 0)
        def _():
            o_ref[...] += up_down(xs_ref[0:bm])
    else:
        o_ref[...] += up_down(xs_ref[...])

    if has_final:
        @pl.when(f == pl.num_programs(1) - 1)
        def _():
            o_ref[...] = _rms(o_ref[...], fg_ref[...])
            if riders:
                or_ref[...] = _rms(or_ref[...], fg_ref[...])


def mlp(x, gains, w_up, w_down, layer, *, final_gain=None, rider=None, bm, bf):
    M, D = x.shape
    FF = w_up.shape[-1]
    bm, bf = _tile(M, bm, SUBLANES), _tile(FF, bf, LANES)
    riders = 0 if rider is None else rider.shape[0]
    in_specs = [pl.BlockSpec((bm, D), lambda i, f: (i, 0), pipeline_mode=pl.Buffered(1)),
                pl.BlockSpec((None, 1, D), lambda i, f: (layer, 0, 0)),
                pl.BlockSpec((None, D, bf), lambda i, f: (layer, 0, f)),
                pl.BlockSpec((None, bf, D), lambda i, f: (layer, f, 0))]
    args = [x, gains, w_up, w_down]
    if final_gain is not None:
        in_specs.append(pl.BlockSpec((1, D), lambda i, f: (0, 0)))
        args.append(final_gain)
    out_specs = [pl.BlockSpec((bm, D), lambda i, f: (i, 0), pipeline_mode=pl.Buffered(1))]
    out_shape = [jax.ShapeDtypeStruct((M, D), F32)]
    if riders:
        in_specs.append(pl.BlockSpec((riders, D), lambda i, f: (0, 0)))
        args.append(rider)
        out_specs.append(pl.BlockSpec((None, riders, D), lambda i, f: (i, 0, 0)))
        out_shape.append(jax.ShapeDtypeStruct((M // bm, riders, D), F32))
    outs = pl.pallas_call(
        functools.partial(_mlp_body, has_final=final_gain is not None, riders=riders),
        grid=(M // bm, FF // bf),
        in_specs=in_specs,
        out_specs=out_specs,
        out_shape=out_shape,
        scratch_shapes=[pltpu.VMEM((bm + riders, D), BF16)],
        compiler_params=_cparams(2),
        name="mlp",
    )(*args)
    return (outs[0], outs[1][0]) if riders else outs[0]


def _mem_body(y_ref, wm_ref, h_ref, g_ref, wq_ref, kv_ref, wo_ref, o_ref, *, heads, hd):
    x = h_ref[...] + _dot(y_ref[...].astype(BF16), wm_ref[...])
    xn = _rms(x, g_ref[...]).astype(BF16)
    q = _dot(xn, wq_ref[...])
    width = heads * hd
    scale = hd ** -0.5
    outs = []
    for h in range(heads):
        qh = q[:, h * hd:(h + 1) * hd].astype(BF16)
        kh = kv_ref[:, h * hd:(h + 1) * hd].astype(BF16)
        vh = kv_ref[:, width + h * hd:width + (h + 1) * hd].astype(BF16)
        s = _dot_nt(qh, kh) * scale
        p = jnp.exp(s - jnp.max(s, axis=-1, keepdims=True))
        p = p / jnp.sum(p, axis=-1, keepdims=True)
        outs.append(_dot(p.astype(BF16), vh))
    o = jnp.concatenate(outs, axis=-1).astype(BF16)
    o_ref[...] = x + _dot(o, wo_ref[...])


def mix_out_mem_attend(y, w_mix, mix_layer, h, gains, w_q, kv, w_o, layer, *, heads, hd, bm):
    B, T, D = h.shape
    Ky = y.shape[-1]
    Mt = kv.shape[2]
    width = heads * hd
    bm = _tile(T, bm, SUBLANES)
    return pl.pallas_call(
        functools.partial(_mem_body, heads=heads, hd=hd),
        grid=(B, T // bm),
        in_specs=[pl.BlockSpec((None, bm, Ky), lambda b, i: (b, i, 0)),
                  pl.BlockSpec((None, Ky, D), lambda b, i: (mix_layer, 0, 0), pipeline_mode=pl.Buffered(1)),
                  pl.BlockSpec((None, bm, D), lambda b, i: (b, i, 0)),
                  pl.BlockSpec((None, 1, D), lambda b, i: (layer, 0, 0)),
                  pl.BlockSpec((None, D, width), lambda b, i: (layer, 0, 0), pipeline_mode=pl.Buffered(1)),
                  pl.BlockSpec((None, None, Mt, 2 * width), lambda b, i: (layer, b, 0, 0)),
                  pl.BlockSpec((None, width, D), lambda b, i: (layer, 0, 0), pipeline_mode=pl.Buffered(1))],
        out_specs=pl.BlockSpec((None, bm, D), lambda b, i: (b, i, 0)),
        out_shape=jax.ShapeDtypeStruct((B, T, D), F32),
        compiler_params=_cparams(2),
        name="mix_out_mem_attend",
    )(y, w_mix, h, gains, w_q, kv, w_o)


def _mem_sample_body(y_ref, wm_ref, h_ref, g_ref, wq_ref, kv_ref, wo_ref, o_ref, *, heads, hd):
    batch = h_ref.shape[0]
    scale = hd ** -0.5
    x = h_ref[...] + _dot(y_ref[...].astype(BF16), wm_ref[...])
    q = _bf_round(_dot(_rms(x, g_ref[...]).astype(BF16), wq_ref[...]))
    outs = []
    for b in range(batch):
        qb = jnp.concatenate([q[b:b + 1, h * hd:(h + 1) * hd] for h in range(heads)], axis=0)
        k = _bf_round(kv_ref[b, :, 0])
        v = _bf_round(kv_ref[b, :, 1])
        s = jnp.sum(k * qb[None], axis=-1, keepdims=True) * scale
        p = jnp.exp(s - jnp.max(s, axis=0))
        p = _bf_round(p / jnp.sum(p, axis=0))
        ob = jnp.sum(p * v, axis=0)
        outs.append(jnp.concatenate([ob[h:h + 1] for h in range(heads)], axis=1))
    o = jnp.concatenate(outs, axis=0).astype(BF16)
    o_ref[...] = x + _dot(o, wo_ref[...])


def mix_out_mem_attend_sample(y, w_mix, mix_layer, h, gains, w_q, kv, w_o, layer, *, heads, hd):
    B, D = h.shape
    Ky = y.shape[-1]
    Mt = kv.shape[2]
    width = heads * hd
    one = pl.Buffered(1)
    return pl.pallas_call(
        functools.partial(_mem_sample_body, heads=heads, hd=hd),
        grid=(1,),
        in_specs=[pl.BlockSpec((B, Ky), lambda i: (0, 0)),
                  pl.BlockSpec((None, Ky, D), lambda i: (mix_layer, 0, 0), pipeline_mode=one),
                  pl.BlockSpec((B, D), lambda i: (0, 0)),
                  pl.BlockSpec((None, 1, D), lambda i: (layer, 0, 0)),
                  pl.BlockSpec((None, D, width), lambda i: (layer, 0, 0), pipeline_mode=one),
                  pl.BlockSpec((None, B, Mt, 2, heads, hd), lambda i: (layer, 0, 0, 0, 0, 0), pipeline_mode=one),
                  pl.BlockSpec((None, width, D), lambda i: (layer, 0, 0), pipeline_mode=one)],
        out_specs=pl.BlockSpec((B, D), lambda i: (0, 0)),
        out_shape=jax.ShapeDtypeStruct((B, D), F32),
        compiler_params=_cparams(1),
        name="mix_out_mem_attend_sample",
    )(y, w_mix, h, gains, w_q, kv, w_o)


def t5_bucket(dist):
    dist = np.asarray(dist, np.int64)
    max_exact = N_BUCKETS // 2
    d = np.maximum(dist, 1).astype(np.float64)
    large = max_exact + (np.log(d / max_exact) / math.log(MAX_DISTANCE / max_exact)
                         * (N_BUCKETS - max_exact)).astype(np.int64)
    return np.where(dist < max_exact, dist, np.minimum(large, N_BUCKETS - 1)).astype(np.int32)


ATTN_UNROLL = 7


def _unroll(trip):
    return max(u for u in range(1, ATTN_UNROLL + 2) if trip % u == 0)


def _attn_block(q, k2, v2, bias2, scale):
    blk = q.shape[0]
    s = _dot_nt(q.astype(BF16), k2.astype(BF16)) * scale + bias2
    m = jnp.max(jnp.maximum(s[:, :blk], s[:, blk:]), axis=-1, keepdims=True)
    p = jnp.exp(s - m)
    l = jnp.sum(p[:, :blk] + p[:, blk:], axis=-1, keepdims=True)
    o = _dot(p.astype(BF16), v2.astype(BF16))
    return o / l, m + jnp.log(l)


def _dil_attn_body(*refs, dils, tile, hd):
    n_groups = len(dils)
    bias_ref, o_ref, og_ref, lse_ref = refs[5 * n_groups:]
    not_first_tile = pl.program_id(1) > 0
    blk = A_BLOCK
    scale = hd ** -0.5

    for g, d in enumerate(dils):
        q_ref, kc_ref, vc_ref, kp_ref, vp_ref = refs[5 * g:5 * g + 5]
        stream_rows = blk * d
        n_blocks = tile // stream_rows

        def rows(start, d=d):
            if d == 1:
                return pl.ds(pl.multiple_of(start, blk), blk)
            return pl.ds(start, blk, stride=d)

        bias2 = bias_ref[g]
        bias2_first = jnp.concatenate(
            [jnp.where(not_first_tile, bias2[:, :blk], -jnp.inf), bias2[:, blk:]], axis=1)

        def run(starts, prev_of, bias, g=g, q_ref=q_ref, kc_ref=kc_ref, vc_ref=vc_ref, rows=rows):
            results = []
            for s in starts:
                kp, vp = prev_of(s)
                k2 = jnp.concatenate([kp, kc_ref[rows(s), :]], axis=0)
                v2 = jnp.concatenate([vp, vc_ref[rows(s), :]], axis=0)
                results.append(_attn_block(q_ref[rows(s), :], k2, v2, bias, scale))
            for s, (o, lse) in zip(starts, results):
                og_ref[g, rows(s), :] = o
                lse_ref[g, rows(s), :] = jnp.broadcast_to(lse, (blk, hd))

        def first(it, carry, run=run, rows=rows, kp_ref=kp_ref, vp_ref=vp_ref, u=_unroll(d)):
            run([it * u + k for k in range(u)], lambda s: (kp_ref[rows(s), :], vp_ref[rows(s), :]), bias2_first)
            return carry

        n_rest = d * (n_blocks - 1)

        def rest(it, carry, run=run, rows=rows, kc_ref=kc_ref, vc_ref=vc_ref, d=d, stream_rows=stream_rows,
                 u=_unroll(n_rest)):
            idx = [it * u + k for k in range(u)]
            run([i % d + (1 + i // d) * stream_rows for i in idx],
                lambda s: (kc_ref[rows(s - stream_rows), :], vc_ref[rows(s - stream_rows), :]), bias2)
            return carry

        lax.fori_loop(0, d // _unroll(d), first, 0)
        if n_blocks > 1:
            lax.fori_loop(0, n_rest // _unroll(n_rest), rest, 0)

    lses = [lse_ref[g] for g in range(n_groups)]
    mx = functools.reduce(jnp.maximum, lses)
    es = [jnp.exp(l - mx) for l in lses]
    den = functools.reduce(lambda a, b: a + b, es)
    acc = (es[0] / den) * og_ref[0]
    for g in range(1, n_groups):
        acc = acc + (es[g] / den) * og_ref[g]
    o_ref[...] = acc


def dilated_attention_prompt(qkv, bias, *, heads, hd, tile):
    B, T, _ = qkv.shape
    dils = A_DILATIONS
    n_groups = len(dils)
    tile = min(tile, T)
    assert hd == LANES and T % tile == 0 and all(tile % (A_BLOCK * d) == 0 for d in dils)
    sec = n_groups * heads

    def cur(which, g):
        return pl.BlockSpec((None, tile, hd), lambda b, n, h: (b, n, which * sec + g * heads + h))

    def prev(which, g):
        rows = A_BLOCK * dils[g]
        per = tile // rows
        return pl.BlockSpec((None, rows, hd),
                            lambda b, n, h: (b, jnp.maximum(n * per - 1, 0), which * sec + g * heads + h))

    in_specs = []
    for g in range(n_groups):
        in_specs += [cur(0, g), cur(1, g), cur(2, g), prev(1, g), prev(2, g)]
    in_specs.append(pl.BlockSpec((n_groups, None, A_BLOCK, 2 * A_BLOCK), lambda b, n, h: (0, h, 0, 0)))
    return pl.pallas_call(
        functools.partial(_dil_attn_body, dils=dils, tile=tile, hd=hd),
        grid=(B, T // tile, heads),
        in_specs=in_specs,
        out_specs=pl.BlockSpec((None, tile, hd), lambda b, n, h: (b, n, h)),
        out_shape=jax.ShapeDtypeStruct((B, T, heads * hd), F32),
        scratch_shapes=[pltpu.VMEM((n_groups, tile, hd), F32), pltpu.VMEM((n_groups, tile, hd), F32)],
        compiler_params=_cparams(3),
        name="dilated_attention_prompt",
    )(*([qkv] * (5 * n_groups)), bias)


def _dec_attn_body(qkv_ref, *refs, groups, span, scale):
    bufs, bias_ref, o_ref = refs[:groups], refs[groups], refs[groups + 1]
    outs, lses = [], []
    for g in range(groups):
        q = _bf_round(qkv_ref[g])
        kn = _bf_round(qkv_ref[groups + g])
        vn = _bf_round(qkv_ref[2 * groups + g])
        k = _bf_round(bufs[g][:, 0])
        v = _bf_round(bufs[g][:, 1])
        s = jnp.sum(k * q[None], axis=-1, keepdims=True) * scale + bias_ref[g, :span]
        s_self = jnp.sum(q * kn, axis=-1, keepdims=True) * scale + bias_ref[g, span]
        m = jnp.maximum(jnp.max(s, axis=0), s_self)
        p = jnp.exp(s - m[None])
        p_self = jnp.exp(s_self - m)
        l = jnp.sum(p, axis=0) + p_self
        outs.append(jnp.sum(_bf_round(p / l[None]) * v, axis=0) + _bf_round(p_self / l) * vn)
        lses.append(m + jnp.log(l))
    mx = functools.reduce(jnp.maximum, lses)
    es = [jnp.exp(l - mx) for l in lses]
    den = functools.reduce(lambda a, b: a + b, es)
    acc = (es[0] / den) * outs[0]
    for g in range(1, groups):
        acc = acc + (es[g] / den) * outs[g]
    o_ref[...] = acc


def dilated_attention_sample(qkv, caches, layer, bias, *, heads, hd, span):
    B = qkv.shape[0]
    groups = len(caches)
    in_specs = [pl.BlockSpec((None, 3 * groups, heads, hd), lambda b: (b, 0, 0, 0))]
    views = []
    for g, c in enumerate(caches):
        d = A_DILATIONS[g]
        assert c.shape[2] == span * d
        views.append(c.reshape(c.shape[0], B, span, d, 2, heads, hd))
        in_specs.append(pl.BlockSpec((None, None, span, None, 2, heads, hd),
                                     lambda b: (layer, b, 0, 0, 0, 0, 0)))
    in_specs.append(pl.BlockSpec((groups, span + 1, heads, hd), lambda b: (0, 0, 0, 0)))
    return pl.pallas_call(
        functools.partial(_dec_attn_body, groups=groups, span=span, scale=hd ** -0.5),
        grid=(B,),
        in_specs=in_specs,
        out_specs=pl.BlockSpec((None, heads, hd), lambda b: (b, 0, 0)),
        out_shape=jax.ShapeDtypeStruct((B, heads, hd), F32),
        compiler_params=_cparams(1),
        name="dilated_attention_sample",
    )(qkv, *views, bias)


def _to_window_body(*refs, heads, hd):
    o_ref = refs[-1]
    chunk = refs[0].shape[0]
    for layer in range(o_ref.shape[0]):
        for s in range(2):
            src = refs[2 * layer + s]
            for h in range(heads):
                o_ref[layer, pl.ds(s * heads + h, chunk, stride=2 * heads), :] = src[:, h * hd:(h + 1) * hd]


def to_window(qkvs, g, *, rows, groups, heads, hd):
    B, T, _ = qkvs[0].shape
    W = heads * hd
    chunk = _tile(rows, 256, SUBLANES)
    assert (T - rows) % chunk == 0
    first = (T - rows) // chunk
    in_specs, args = [], []
    for qkv in qkvs:
        in_specs += [pl.BlockSpec((None, chunk, W), lambda b, c: (b, first + c, groups + g)),
                     pl.BlockSpec((None, chunk, W), lambda b, c: (b, first + c, 2 * groups + g))]
        args += [qkv, qkv]
    per_token = 2 * heads
    out = pl.pallas_call(
        functools.partial(_to_window_body, heads=heads, hd=hd),
        grid=(B, rows // chunk),
        in_specs=in_specs,
        out_specs=pl.BlockSpec((len(qkvs), None, chunk * per_token, hd), lambda b, c: (0, b, c, 0)),
        out_shape=jax.ShapeDtypeStruct((len(qkvs), B, rows * per_token, hd), F32),
        compiler_params=_cparams(2),
        name="to_window",
    )(*args)
    return out.reshape(len(qkvs), B, rows, 2, heads, hd)


def _window_update_body(main_ref, next_ref, new_ref, o_ref):
    rows = main_ref.shape[0]
    last_chunk = pl.program_id(2) == pl.num_programs(2) - 1
    o_ref[0:rows - 1] = main_ref[1:rows]
    o_ref[rows - 1:rows] = jnp.where(last_chunk, new_ref[...], next_ref[...])


def window_update(cache, new, *, chunk):
    n_layers, B, rows = cache.shape[:3]
    tail = cache.shape[3:]
    chunk = _tile(rows, chunk, 1)
    zeros = (0,) * len(tail)
    return pl.pallas_call(
        _window_update_body,
        grid=(n_layers, B, rows // chunk),
        in_specs=[pl.BlockSpec((None, None, chunk) + tail, lambda j, b, c: (j, b, c) + zeros),
                  pl.BlockSpec((None, None, 1) + tail,
                               lambda j, b, c: (j, b, jnp.minimum((c + 1) * chunk, rows - 1)) + zeros),
                  pl.BlockSpec((None, None, 1) + tail, lambda j, b, c: (j, b, 0) + zeros)],
        out_specs=pl.BlockSpec((None, None, chunk) + tail, lambda j, b, c: (j, b, c) + zeros),
        out_shape=jax.ShapeDtypeStruct(cache.shape, cache.dtype),
        compiler_params=_cparams(3),
        name="window_update",
    )(cache, cache, new)


def _gla_gate_body(x_ref, g_ref, wl_ref, wg_ref, b_ref, o_ref):
    xn = _rms(x_ref[...], g_ref[...]).astype(BF16)
    low = _dot(xn, wl_ref[...].astype(BF16))
    z = _dot(low.astype(BF16), wg_ref[...].astype(BF16)) + b_ref[...]
    log_sig = jnp.minimum(z, 0.0) - jnp.log(1.0 + jnp.exp(-jnp.abs(z)))
    o_ref[...] = log_sig / GLA_GATE_NORMALIZER


def gla_gate(x, gain, w_low, w_gate, b_gate, layer, *, bm):
    M, D = x.shape
    g_arr, g_idx = gain
    rank, dk = w_gate.shape[1], w_gate.shape[2]
    bm = _tile(M, bm, SUBLANES)
    return pl.pallas_call(
        _gla_gate_body,
        grid=(M // bm,),
        in_specs=[pl.BlockSpec((bm, D), lambda i: (i, 0)),
                  pl.BlockSpec((None, 1, D), lambda i: (g_idx, 0, 0)),
                  pl.BlockSpec((D, rank), lambda i: (0, 0)),
                  pl.BlockSpec((None, rank, dk), lambda i: (layer, 0, 0)),
                  pl.BlockSpec((None, 1, dk), lambda i: (layer, 0, 0))],
        out_specs=pl.BlockSpec((bm, dk), lambda i: (i, 0)),
        out_shape=jax.ShapeDtypeStruct((M, dk), F32),
        compiler_params=_cparams(1),
        name="gla_gate",
    )(x, g_arr, w_low, w_gate, b_gate)


def _cumsum_rows(x):
    rows = x.shape[0]
    row = lax.broadcasted_iota(jnp.int32, x.shape, 0)
    s = 1
    while s < rows:
        x = x + jnp.where(row >= s, pltpu.roll(x, s, axis=0), 0.0)
        s *= 2
    return x


def _block_refs(bcs, m, pairs):
    rows, dk = bcs.shape
    nb = rows // m
    ends = [bcs[(k + 1) * m - 1:(k + 1) * m, :] for k in range(nb)]
    parts = []
    for k in range(nb):
        if pairs:
            r = ends[k - 1] if k % 2 == 1 else ends[k]
        else:
            r = ends[k - 1] if k > 0 else jnp.zeros((1, dk), F32)
        parts.append(jnp.broadcast_to(r, (m, dk)))
    return parts[0] if nb == 1 else jnp.concatenate(parts, axis=0)


def _la_body(a_ref, b_ref, c_ref, gate_ref, e_ref, gn_ref, s0_ref, y_ref, so_ref, st_ref,
             *, hgrn2, heads, dk, dv, chunk, base, valid, q_scale, head_batch):
    n = pl.program_id(1)

    @pl.when(n == 0)
    def _():
        for h in range(heads):
            st_ref[h] = s0_ref[h].T

    tq = lax.broadcasted_iota(jnp.int32, (chunk, chunk), 0)
    ts = lax.broadcasted_iota(jnp.int32, (chunk, chunk), 1)
    row = lax.broadcasted_iota(jnp.int32, (chunk, 1), 0)
    sh = int(math.log2(base))
    mask_diag = jnp.logical_and((tq >> sh) == (ts >> sh), ts <= tq)
    levels = []
    m = base
    while m < chunk:
        s1 = int(math.log2(m))
        mk = jnp.logical_and((tq >> (s1 + 1)) == (ts >> (s1 + 1)),
                             jnp.logical_and(((tq >> s1) & 1) == 1, ((ts >> s1) & 1) == 0))
        levels.append((m, mk))
        m *= 2
    gn = gn_ref[...]

    def one_head(h):
        ksl = slice(h * dk, (h + 1) * dk)
        vsl = slice(h * dv, (h + 1) * dv)
        if hgrn2:
            a = a_ref[:, ksl]
            lb = e_ref[:, ksl]
            f = lb + (1.0 - lb) * jax.nn.sigmoid(b_ref[:, ksl])
            qa = a * jax.nn.sigmoid(a)
            kk = 1.0 - f
            g = jnp.log(f)
        else:
            qa = a_ref[:, ksl] * q_scale
            kk = b_ref[:, ksl]
            g = e_ref[:, ksl]
        v = c_ref[:, vsl]
        if valid < chunk:
            kk = jnp.where(row < valid, kk, 0.0)
            g = jnp.where(row < valid, g, 0.0)
        bcs = _cumsum_rows(g)
        vb = v.astype(BF16)
        st = st_ref[h]

        o = _dot_nt((qa * jnp.exp(bcs)).astype(BF16), st.astype(BF16))

        ref_d = _block_refs(bcs, base, False)
        a_mat = _dot_nt((qa * jnp.exp(bcs - ref_d)).astype(BF16), (kk * jnp.exp(ref_d - bcs)).astype(BF16))
        a_mat = jnp.where(mask_diag, a_mat, 0.0)
        for (m, mk) in levels:
            dec = jnp.exp(-jnp.abs(bcs - _block_refs(bcs, m, True)))
            a_mat = a_mat + jnp.where(mk, _dot_nt((qa * dec).astype(BF16), (kk * dec).astype(BF16)), 0.0)
        o = o + _dot(a_mat.astype(BF16), vb)

        b_last = bcs[chunk - 1:chunk, :]
        kd = (kk * jnp.exp(b_last - bcs)).astype(BF16)
        st_new = jnp.exp(b_last) * st + _dot_tn(vb, kd)

        gate = gate_ref[:, vsl]
        return st_new, _rms(o, gn) * (gate * jax.nn.sigmoid(gate))

    for h0 in range(0, heads, head_batch):
        hs = range(h0, min(h0 + head_batch, heads))
        results = [one_head(h) for h in hs]
        for h, (st_new, y) in zip(hs, results):
            st_ref[h] = st_new
            y_ref[:, h * dv:(h + 1) * dv] = y

    @pl.when(n == pl.num_programs(1) - 1)
    def _():
        for h in range(heads):
            so_ref[h] = st_ref[h].T


def gated_linear_attention(proj, extra, gnorm, s0, *, hgrn2, chunk, valid=None):
    B, T, _ = proj.shape
    _, H, dk, dv = s0.shape
    chunk = min(chunk, T)
    assert T % chunk == 0
    base = min(LA_BASE, chunk)
    valid = chunk if valid is None else valid
    wk, wv = H * dk, H * dv

    def col(width, start):
        assert start % width == 0
        return pl.BlockSpec((None, chunk, width), lambda b, n: (b, n, start // width))

    if hgrn2:
        specs = [col(wk, 0), col(wk, wk), col(wv, 2 * wk), col(wv, 2 * wk + wv),
                 pl.BlockSpec((1, wk), lambda b, n: (0, 0))]
    else:
        specs = [col(wk, 0), col(wk, wk), col(wv, 2 * wk), col(wv, 2 * wk + wv),
                 pl.BlockSpec((None, chunk, wk), lambda b, n: (b, n, 0))]
    specs += [pl.BlockSpec((1, dv), lambda b, n: (0, 0)),
              pl.BlockSpec((None, H, dk, dv), lambda b, n: (b, 0, 0, 0))]
    return pl.pallas_call(
        functools.partial(_la_body, hgrn2=hgrn2, heads=H, dk=dk, dv=dv, chunk=chunk, base=base,
                          valid=valid, q_scale=1.0 if hgrn2 else dk ** -0.5,
                          head_batch=max(1, LA_HEAD_BATCH_LANES // max(dk, dv))),
        grid=(B, T // chunk),
        in_specs=specs,
        out_specs=[pl.BlockSpec((None, chunk, wv), lambda b, n: (b, n, 0)),
                   pl.BlockSpec((None, H, dk, dv), lambda b, n: (b, 0, 0, 0))],
        out_shape=[jax.ShapeDtypeStruct((B, T, wv), F32),
                   jax.ShapeDtypeStruct((B, H, dk, dv), F32)],
        scratch_shapes=[pltpu.VMEM((H, dv, dk), F32)],
        compiler_params=_cparams(2),
        name="gated_linear_attention",
    )(proj, proj, proj, proj, extra, gnorm, s0)


def _pad_rows(x, rows):
    return jnp.pad(x, ((0, 0), (0, rows - x.shape[1]), (0, 0)))


def _trunk(xp, xs, p_mem, s_mem, caches, b_states, c_states, P):
    Bp, T, D = xp.shape
    Bs = xs.shape[0]
    assert xs.shape[1] == 1 and Bs <= RIDER_ROWS
    Mp = Bp * T
    depth = P["norm_mix"].shape[0]
    hd_a, hpg = P["a_hd"], P["a_hpg"]
    groups = len(A_DILATIONS)
    Wa = hpg * hd_a
    tile = dict(bm=2048, bn=512)
    mlp_tile = dict(bm=1024, bf=512)

    def ride(h):
        return jnp.pad(h, ((0, RIDER_ROWS - Bs), (0, 0)))

    hp, hs = xp.reshape(Mp, D), xs.reshape(Bs, D)
    qkv_prompt, new_rows, p_b, s_b, p_c, s_c = [], [], [], [], [], []
    for li in range(depth):
        kind, j = li % N_MIXERS, li // N_MIXERS
        gmix = (P["norm_mix"], li)
        if kind == 0:
            qkv_p, qkv_s = linear(hp, P["a_w_in"], j, gain=gmix, rider=ride(hs), **tile)
            qkv3 = qkv_p.reshape(Bp, T, 3 * groups * Wa)
            yp = dilated_attention_prompt(qkv3, P["a_bias_prompt"], heads=hpg, hd=hd_a, tile=16 * A_BLOCK)
            qkv_prompt.append(qkv3)
            qkv5 = qkv_s[:Bs].reshape(Bs, 3, groups, hpg, hd_a)
            ys = dilated_attention_sample(qkv5.reshape(Bs, 3 * groups, hpg, hd_a), caches, j,
                                          P["a_bias_sample"], heads=hpg, hd=hd_a, span=A_BLOCK)
            new_rows.append([qkv5[:, 1:3, g][:, None] for g in range(groups)])
            ys = ys.reshape(Bs, Wa)
            w_mix = P["a_w_out"]
        elif kind == 1:
            proj_p, proj_s = linear(hp, P["b_w_in"], j, gain=gmix, rider=ride(hs), **tile)
            lb = P["lower_bounds"][li][None, :]
            gn = P["b_gnorm"][j][None, :]
            yp, S = gated_linear_attention(proj_p.reshape(Bp, T, -1), lb, gn,
                                           jnp.zeros((Bp,) + b_states.shape[2:], F32), hgrn2=True, chunk=LA_CHUNK)
            p_b.append(S)
            ys, S = gated_linear_attention(_pad_rows(proj_s[:Bs, None], SUBLANES), lb, gn, b_states[j],
                                           hgrn2=True, chunk=SUBLANES, valid=1)
            s_b.append(S)
            w_mix = P["b_w_out"]
        else:
            wk = P["c_wk"]
            n_main = P["c_n_main"]
            proj_p, proj_s = linear(hp, P["c_w_in_t"], j, n_out=n_main, gain=gmix, rider=ride(hs),
                                    transposed=True, **tile)
            gate_args = (gmix, P["c_w_low"][j], P["c_w_gate"], P["c_b_gate"], j)
            gn = P["c_gnorm"][j][None, :]
            yp, S = gated_linear_attention(proj_p.reshape(Bp, T, n_main),
                                           gla_gate(hp, *gate_args, bm=1024).reshape(Bp, T, wk), gn,
                                           jnp.zeros((Bp,) + c_states.shape[2:], F32), hgrn2=False, chunk=LA_CHUNK)
            p_c.append(S)
            ys, S = gated_linear_attention(_pad_rows(proj_s[:Bs, None], SUBLANES),
                                           _pad_rows(gla_gate(hs, *gate_args, bm=SUBLANES)[:, None], SUBLANES), gn,
                                           c_states[j], hgrn2=False, chunk=SUBLANES, valid=1)
            s_c.append(S)
            w_mix = P["c_w_out"]

        mem_kw = dict(heads=P["m_heads"], hd=P["m_hd"])
        hp = mix_out_mem_attend(yp.reshape(Bp, T, -1), w_mix, j, hp.reshape(Bp, T, D), P["norm_mem"], P["mem_w_q"],
                                p_mem, P["mem_w_out"], li, bm=512, **mem_kw).reshape(Mp, D)
        if ys.ndim == 3:
            ys = ys[:, 0]
        hs = mix_out_mem_attend_sample(ys, w_mix, j, hs, P["norm_mem"], P["mem_w_q"], s_mem, P["mem_w_out"], li,
                                       **mem_kw)
        final = P["norm_final"] if li == depth - 1 else None
        hp, hs = mlp(hp, P["norm_mlp"], P["mlp_w_up"], P["mlp_w_down"], li, final_gain=final,
                     rider=ride(hs), **mlp_tile)
        hs = hs[:Bs]

    new_rows = [jnp.stack([rows[g] for rows in new_rows], axis=0) for g in range(groups)]
    p_win = [to_window(qkv_prompt, g, rows=min(A_WINDOWS[g], T), groups=groups, heads=hpg, hd=hd_a)
             for g in range(groups)]
    return (hp.reshape(Bp, T, D), hs.reshape(Bs, 1, D), p_win, new_rows,
            jnp.stack(p_b, axis=0), jnp.stack(s_b, axis=0), jnp.stack(p_c, axis=0), jnp.stack(s_c, axis=0))


def _bias_tables(rel_bias_table, hpg, span):
    steps = np.arange(A_BLOCK)[:, None] + A_BLOCK - np.arange(2 * A_BLOCK)[None, :]
    eye = np.eye(N_BUCKETS, dtype=np.float32)
    prompt, sample = [], []
    for g, d in enumerate(A_DILATIONS):
        tg_t = rel_bias_table[:, g * hpg:(g + 1) * hpg].T
        onehot = eye[:, t5_bucket(np.clip(steps, 0, span) * d).reshape(-1)]
        bias = jnp.dot(tg_t, jnp.asarray(onehot), precision=lax.Precision.HIGHEST).reshape(hpg, A_BLOCK, 2 * A_BLOCK)
        band = jnp.asarray((steps >= 0) & (steps <= span))
        prompt.append(jnp.where(band[None], bias, -jnp.inf))
        dist = np.concatenate([(span - np.arange(span)) * d, np.zeros(1, np.int64)])
        row = jnp.dot(tg_t, jnp.asarray(eye[:, t5_bucket(dist)]), precision=lax.Precision.HIGHEST)
        sample.append(jnp.broadcast_to(row.T[:, :, None], (span + 1, hpg, LANES)))
    return jnp.stack(prompt, axis=0), jnp.stack(sample, axis=0)


def kernel(x_prompt, x_sample, cache_win0, cache_win1, cache_win2, state_hgrn, state_gla, cache_mem, mem_prompt, rel_bias_table, a_w_in, a_w_out, b_w_in, b_w_out, b_lower_bound, b_gnorm, c_w_in, c_w_gate, c_b_gate, c_w_out, c_gnorm, mem_norm, mem_w_kv, mem_w_q, mem_w_out, norm_mix, norm_mem, norm_mlp, mlp_w_up, mlp_w_down, norm_final):
    depth, D = norm_mix.shape
    hpg, hd_a = cache_win0.shape[-2], cache_win0.shape[-1]
    span = A_BLOCK
    assert all(w // d == span for w, d in zip(A_WINDOWS, A_DILATIONS))
    m_heads, m_hd = cache_mem.shape[-2], cache_mem.shape[-1]
    c_heads, c_dk, c_dv = state_gla.shape[2:]
    wk = c_heads * c_dk
    n_main = 2 * wk + 2 * c_heads * c_dv
    bias_prompt, bias_sample = _bias_tables(rel_bias_table, hpg, span)

    sm = jax.nn.softmax(b_lower_bound.astype(F32), axis=0)
    P = dict(
        a_hd=hd_a, a_hpg=hpg, m_heads=m_heads, m_hd=m_hd,
        a_bias_prompt=bias_prompt, a_bias_sample=bias_sample,
        a_w_in=a_w_in, a_w_out=a_w_out.astype(BF16), b_w_in=b_w_in, b_w_out=b_w_out.astype(BF16),
        lower_bounds=jnp.cumsum(sm, axis=0) - sm, b_gnorm=b_gnorm,
        c_w_in_t=jnp.swapaxes(c_w_in, 1, 2), c_w_low=c_w_in[:, :, n_main:], c_w_gate=c_w_gate,
        c_b_gate=c_b_gate[:, None, :], c_w_out=c_w_out.astype(BF16), c_gnorm=c_gnorm,
        c_wk=wk, c_n_main=n_main,
        mem_w_q=mem_w_q.astype(BF16), mem_w_out=mem_w_out.astype(BF16),
        norm_mix=norm_mix[:, None, :], norm_mem=norm_mem[:, None, :], norm_mlp=norm_mlp[:, None, :],
        mlp_w_up=mlp_w_up, mlp_w_down=mlp_w_down, norm_final=norm_final[None, :],
    )

    Bm, Mt, _ = mem_prompt.shape
    mem_flat = mem_prompt.reshape(Bm * Mt, D)
    mem_gain = mem_norm[:, None, :]
    p_mem = jnp.stack([linear(mem_flat, mem_w_kv, li, gain=(mem_gain, li), bm=512, bn=512)
                       for li in range(depth)], axis=0).reshape(depth, Bm, Mt, 2 * m_heads * m_hd)

    caches = (cache_win0, cache_win1, cache_win2)
    y_prompt, y_sample, p_win, new_rows, p_hgrn, s_hgrn, p_gla, s_gla = _trunk(
        x_prompt, x_sample, p_mem, cache_mem, caches, state_hgrn, state_gla, P)
    s_win = [window_update(c, n, chunk=512) for c, n in zip(caches, new_rows)]
    return (y_prompt, y_sample, p_win[0], p_win[1], p_win[2], p_hgrn, p_gla,
            p_mem.reshape(depth, Bm, Mt, 2, m_heads, m_hd),
            s_win[0], s_win[1], s_win[2], s_hgrn, s_gla)
```

```python
import functools
import math

import numpy as np
import jax
import jax.numpy as jnp
from jax import lax
from jax.experimental import pallas as pl
from jax.experimental.pallas import tpu as pltpu

F32 = jnp.float32
BF16 = jnp.bfloat16

EPS = 1e-6
N_MIXERS = 3
A_WINDOWS = (128, 512, 2048)
A_DILATIONS = (1, 4, 16)
A_BLOCK = 128
N_BUCKETS = 32
MAX_DISTANCE = 2048
GLA_GATE_NORMALIZER = 16.0
LA_BASE = 16
LA_CHUNK = 128
RIDER_ROWS = 16

TILES = dict(
    in_proj=dict(bm=2048, bn=512),
    mlp=dict(bm=1024, bf=512),
    mix_mem_rows=512,
    gate_rows=1024,
    attention_rows=16 * A_BLOCK,
    mem_kv=dict(bm=512, bn=512),
    window_copy_rows=512,
    window_relayout_rows=256,
)

V7X_VMEM_BYTES = 64 * 1024 * 1024
VMEM_LIMIT = V7X_VMEM_BYTES - 8 * 1024 * 1024
SUBLANES = 8
LANES = 128


def _cparams(n_axes):
    return pltpu.CompilerParams(dimension_semantics=("arbitrary",) * n_axes,
                                vmem_limit_bytes=VMEM_LIMIT)


def _tile(n, target, align):
    if n <= target:
        return n
    t = (target // align) * align
    while n % t:
        t -= align
    return t


def _rms(x, g):
    r = lax.rsqrt(jnp.mean(x * x, axis=-1, keepdims=True) + EPS)
    return (x * r) * g


def _dot(a, b):
    return jnp.dot(a, b, preferred_element_type=F32)


def _dot_nt(a, b):
    return lax.dot_general(a, b, (((1,), (1,)), ((), ())), preferred_element_type=F32)


def _dot_tn(a, b):
    return lax.dot_general(a, b, (((0,), (0,)), ((), ())), preferred_element_type=F32)


def _bf_round(a):
    return a.astype(BF16).astype(F32)


def _linear_body(*refs, riders, transposed):
    if riders:
        x_ref, w_ref, g_ref, r_ref, o_ref, or_ref, xs_ref = refs
    else:
        x_ref, w_ref, g_ref, o_ref, xs_ref = refs
    bm = x_ref.shape[0]

    @pl.when(pl.program_id(1) == 0)
    def _():
        xs_ref[0:bm] = _rms(x_ref[...], g_ref[...]).astype(BF16)
        if riders:
            xs_ref[bm:bm + riders] = _rms(r_ref[...], g_ref[...]).astype(BF16)

    w = w_ref[...].astype(BF16)
    acc = _dot_nt(xs_ref[...], w) if transposed else _dot(xs_ref[...], w)
    if riders:
        o_ref[...] = acc[:bm]
        or_ref[...] = acc[bm:]
    else:
        o_ref[...] = acc


def linear(x, w, layer, *, n_out=None, gain, rider=None, transposed=False, bm, bn):
    M, K = x.shape
    N = (w.shape[1] if transposed else w.shape[2]) if n_out is None else n_out
    bm, bn = _tile(M, bm, SUBLANES), _tile(N, bn, LANES)
    riders = 0 if rider is None else rider.shape[0]
    g_arr, g_idx = gain
    w_spec = (pl.BlockSpec((None, bn, K), lambda i, j: (layer, j, 0)) if transposed
              else pl.BlockSpec((None, K, bn), lambda i, j: (layer, 0, j)))
    in_specs = [pl.BlockSpec((bm, K), lambda i, j: (i, 0), pipeline_mode=pl.Buffered(1)),
                w_spec,
                pl.BlockSpec((None, 1, K), lambda i, j: (g_idx, 0, 0))]
    args = [x, w, g_arr]
    out_specs = [pl.BlockSpec((bm, bn), lambda i, j: (i, j))]
    out_shape = [jax.ShapeDtypeStruct((M, N), F32)]
    if riders:
        in_specs.append(pl.BlockSpec((riders, K), lambda i, j: (0, 0)))
        args.append(rider)
        out_specs.append(pl.BlockSpec((None, riders, bn), lambda i, j: (i, 0, j)))
        out_shape.append(jax.ShapeDtypeStruct((M // bm, riders, N), F32))
    outs = pl.pallas_call(
        functools.partial(_linear_body, riders=riders, transposed=transposed),
        grid=(M // bm, N // bn),
        in_specs=in_specs,
        out_specs=out_specs,
        out_shape=out_shape,
        scratch_shapes=[pltpu.VMEM((bm + riders, K), BF16)],
        compiler_params=_cparams(2),
        name="linear",
    )(*args)
    return (outs[0], outs[1][0]) if riders else outs[0]


def _mlp_body(*refs, has_final, riders):
    refs = list(refs)
    x_ref, g_ref, wu_ref, wd_ref = refs[:4]
    pos = 4
    fg_ref = r_ref = or_ref = None
    if has_final:
        fg_ref = refs[pos]
        pos += 1
    if riders:
        r_ref = refs[pos]
        pos += 1
    o_ref = refs[pos]
    if riders:
        or_ref = refs[pos + 1]
    xs_ref = refs[-1]
    bm = x_ref.shape[0]
    f = pl.program_id(1)

    @pl.when(f == 0)
    def _():
        x = x_ref[...]
        xs_ref[0:bm] = _rms(x, g_ref[...]).astype(BF16)
        o_ref[...] = x
        if riders:
            r = r_ref[...]
            xs_ref[bm:bm + riders] = _rms(r, g_ref[...]).astype(BF16)
            or_ref[...] = r

    def up_down(xs):
        hdn = _dot(xs, wu_ref[...].astype(BF16))
        hdn = jnp.square(jnp.maximum(hdn, 0.0)).astype(BF16)
        return _dot(hdn, wd_ref[...].astype(BF16))

    if riders:
        @pl.when(pl.program_id(0) == 0)
        def _():
            down = up_down(xs_ref[...])
            o_ref[...] += down[:bm]
            or_ref[...] += down[bm:]

        @pl.when(pl.program_id(0) != 0)
        def _():
            o_ref[...] += up_down(xs_ref[0:bm])
    else:
        o_ref[...] += up_down(xs_ref[...])

    if has_final:
        @pl.when(f == pl.num_programs(1) - 1)
        def _():
            o_ref[...] = _rms(o_ref[...], fg_ref[...])
            if riders:
                or_ref[...] = _rms(or_ref[...], fg_ref[...])


def mlp(x, gains, w_up, w_down, layer, *, final_gain=None, rider=None, bm, bf):
    M, D = x.shape
    FF = w_up.shape[-1]
    bm, bf = _tile(M, bm, SUBLANES), _tile(FF, bf, LANES)
    riders = 0 if rider is None else rider.shape[0]
    in_specs = [pl.BlockSpec((bm, D), lambda i, f: (i, 0), pipeline_mode=pl.Buffered(1)),
                pl.BlockSpec((None, 1, D), lambda i, f: (layer, 0, 0)),
                pl.BlockSpec((None, D, bf), lambda i, f: (layer, 0, f)),
                pl.BlockSpec((None, bf, D), lambda i, f: (layer, f, 0))]
    args = [x, gains, w_up, w_down]
    if final_gain is not None:
        in_specs.append(pl.BlockSpec((1, D), lambda i, f: (0, 0)))
        args.append(final_gain)
    out_specs = [pl.BlockSpec((bm, D), lambda i, f: (i, 0), pipeline_mode=pl.Buffered(1))]
    out_shape = [jax.ShapeDtypeStruct((M, D), F32)]
    if riders:
        in_specs.append(pl.BlockSpec((riders, D), lambda i, f: (0, 0)))
        args.append(rider)
        out_specs.append(pl.BlockSpec((None, riders, D), lambda i, f: (i, 0, 0)))
        out_shape.append(jax.ShapeDtypeStruct((M // bm, riders, D), F32))
    outs = pl.pallas_call(
        functools.partial(_mlp_body, has_final=final_gain is not None, riders=riders),
        grid=(M // bm, FF // bf),
        in_specs=in_specs,
        out_specs=out_specs,
        out_shape=out_shape,
        scratch_shapes=[pltpu.VMEM((bm + riders, D), BF16)],
        compiler_params=_cparams(2),
        name="mlp",
    )(*args)
    return (outs[0], outs[1][0]) if riders else outs[0]


def _mem_body(y_ref, wm_ref, h_ref, g_ref, wq_ref, kv_ref, wo_ref, o_ref, *, heads, hd):
    x = h_ref[...] + _dot(y_ref[...].astype(BF16), wm_ref[...])
    xn = _rms(x, g_ref[...]).astype(BF16)
    q = _dot(xn, wq_ref[...])
    width = heads * hd
    scale = hd ** -0.5
    outs = []
    for h in range(heads):
        qh = q[:, h * hd:(h + 1) * hd].astype(BF16)
        kh = kv_ref[:, h * hd:(h + 1) * hd].astype(BF16)
        vh = kv_ref[:, width + h * hd:width + (h + 1) * hd].astype(BF16)
        s = _dot_nt(qh, kh) * scale
        p = jnp.exp(s - jnp.max(s, axis=-1, keepdims=True))
        p = p / jnp.sum(p, axis=-1, keepdims=True)
        outs.append(_dot(p.astype(BF16), vh))
    o = jnp.concatenate(outs, axis=-1).astype(BF16)
    o_ref[...] = x + _dot(o, wo_ref[...])


def mix_out_mem_attend(y, w_mix, mix_layer, h, gains, w_q, kv, w_o, layer, *, heads, hd, bm):
    B, T, D = h.shape
    Ky = y.shape[-1]
    Mt = kv.shape[2]
    width = heads * hd
    bm = _tile(T, bm, SUBLANES)
    return pl.pallas_call(
        functools.partial(_mem_body, heads=heads, hd=hd),
        grid=(B, T // bm),
        in_specs=[pl.BlockSpec((None, bm, Ky), lambda b, i: (b, i, 0)),
                  pl.BlockSpec((None, Ky, D), lambda b, i: (mix_layer, 0, 0), pipeline_mode=pl.Buffered(1)),
                  pl.BlockSpec((None, bm, D), lambda b, i: (b, i, 0)),
                  pl.BlockSpec((None, 1, D), lambda b, i: (layer, 0, 0)),
                  pl.BlockSpec((None, D, width), lambda b, i: (layer, 0, 0), pipeline_mode=pl.Buffered(1)),
                  pl.BlockSpec((None, None, Mt, 2 * width), lambda b, i: (layer, b, 0, 0)),
                  pl.BlockSpec((None, width, D), lambda b, i: (layer, 0, 0), pipeline_mode=pl.Buffered(1))],
        out_specs=pl.BlockSpec((None, bm, D), lambda b, i: (b, i, 0)),
        out_shape=jax.ShapeDtypeStruct((B, T, D), F32),
        compiler_params=_cparams(2),
        name="mix_out_mem_attend",
    )(y, w_mix, h, gains, w_q, kv, w_o)


def _mem_sample_body(y_ref, wm_ref, h_ref, g_ref, wq_ref, kv_ref, wo_ref, o_ref, *, heads, hd):
    batch = h_ref.shape[0]
    scale = hd ** -0.5
    x = h_ref[...] + _dot(y_ref[...].astype(BF16), wm_ref[...])
    q = _bf_round(_dot(_rms(x, g_ref[...]).astype(BF16), wq_ref[...]))
    outs = []
    for b in range(batch):
        qb = jnp.concatenate([q[b:b + 1, h * hd:(h + 1) * hd] for h in range(heads)], axis=0)
        k = _bf_round(kv_ref[b, :, 0])
        v = _bf_round(kv_ref[b, :, 1])
        s = jnp.sum(k * qb[None], axis=-1, keepdims=True) * scale
        p = jnp.exp(s - jnp.max(s, axis=0))
        p = _bf_round(p / jnp.sum(p, axis=0))
        ob = jnp.sum(p * v, axis=0)
        outs.append(jnp.concatenate([ob[h:h + 1] for h in range(heads)], axis=1))
    o = jnp.concatenate(outs, axis=0).astype(BF16)
    o_ref[...] = x + _dot(o, wo_ref[...])


def mix_out_mem_attend_sample(y, w_mix, mix_layer, h, gains, w_q, kv, w_o, layer, *, heads, hd):
    B, D = h.shape
    Ky = y.shape[-1]
    Mt = kv.shape[2]
    width = heads * hd
    one = pl.Buffered(1)
    return pl.pallas_call(
        functools.partial(_mem_sample_body, heads=heads, hd=hd),
        grid=(1,),
        in_specs=[pl.BlockSpec((B, Ky), lambda i: (0, 0)),
                  pl.BlockSpec((None, Ky, D), lambda i: (mix_layer, 0, 0), pipeline_mode=one),
                  pl.BlockSpec((B, D), lambda i: (0, 0)),
                  pl.BlockSpec((None, 1, D), lambda i: (layer, 0, 0)),
                  pl.BlockSpec((None, D, width), lambda i: (layer, 0, 0), pipeline_mode=one),
                  pl.BlockSpec((None, B, Mt, 2, heads, hd), lambda i: (layer, 0, 0, 0, 0, 0), pipeline_mode=one),
                  pl.BlockSpec((None, width, D), lambda i: (layer, 0, 0), pipeline_mode=one)],
        out_specs=pl.BlockSpec((B, D), lambda i: (0, 0)),
        out_shape=jax.ShapeDtypeStruct((B, D), F32),
        compiler_params=_cparams(1),
        name="mix_out_mem_attend_sample",
    )(y, w_mix, h, gains, w_q, kv, w_o)


def t5_bucket(dist):
    dist = np.asarray(dist, np.int64)
    max_exact = N_BUCKETS // 2
    d = np.maximum(dist, 1).astype(np.float64)
    large = max_exact + (np.log(d / max_exact) / math.log(MAX_DISTANCE / max_exact)
                         * (N_BUCKETS - max_exact)).astype(np.int64)
    return np.where(dist < max_exact, dist, np.minimum(large, N_BUCKETS - 1)).astype(np.int32)


ATTN_UNROLL = 7


def _unroll(trip):
    return max(u for u in range(1, ATTN_UNROLL + 2) if trip % u == 0)


def _attn_block(q, k2, v2, bias2, scale):
    blk = q.shape[0]
    s = _dot_nt(q.astype(BF16), k2.astype(BF16)) * scale + bias2
    m = jnp.max(jnp.maximum(s[:, :blk], s[:, blk:]), axis=-1, keepdims=True)
    p = jnp.exp(s - m)
    l = jnp.sum(p[:, :blk] + p[:, blk:], axis=-1, keepdims=True)
    o = _dot(p.astype(BF16), v2.astype(BF16))
    return o / l, m + jnp.log(l)


def _dil_attn_body(*refs, dils, tile, hd):
    n_groups = len(dils)
    bias_ref, o_ref, og_ref, lse_ref = refs[5 * n_groups:]
    not_first_tile = pl.program_id(1) > 0
    blk = A_BLOCK
    scale = hd ** -0.5

    for g, d in enumerate(dils):
        q_ref, kc_ref, vc_ref, kp_ref, vp_ref = refs[5 * g:5 * g + 5]
        stream_rows = blk * d
        n_blocks = tile // stream_rows

        def rows(start, d=d):
            if d == 1:
                return pl.ds(pl.multiple_of(start, blk), blk)
            return pl.ds(start, blk, stride=d)

        bias2 = bias_ref[g]
        bias2_first = jnp.concatenate(
            [jnp.where(not_first_tile, bias2[:, :blk], -jnp.inf), bias2[:, blk:]], axis=1)

        def run(starts, prev_of, bias, g=g, q_ref=q_ref, kc_ref=kc_ref, vc_ref=vc_ref, rows=rows):
            results = []
            for s in starts:
                kp, vp = prev_of(s)
                k2 = jnp.concatenate([kp, kc_ref[rows(s), :]], axis=0)
                v2 = jnp.concatenate([vp, vc_ref[rows(s), :]], axis=0)
                results.append(_attn_block(q_ref[rows(s), :], k2, v2, bias, scale))
            for s, (o, lse) in zip(starts, results):
                og_ref[g, rows(s), :] = o
                lse_ref[g, rows(s), :] = jnp.broadcast_to(lse, (blk, hd))

        def first(it, carry, run=run, rows=rows, kp_ref=kp_ref, vp_ref=vp_ref, u=_unroll(d)):
            run([it * u + k for k in range(u)], lambda s: (kp_ref[rows(s), :], vp_ref[rows(s), :]), bias2_first)
            return carry

        n_rest = d * (n_blocks - 1)

        def rest(it, carry, run=run, rows=rows, kc_ref=kc_ref, vc_ref=vc_ref, d=d, stream_rows=stream_rows,
                 u=_unroll(n_rest)):
            idx = [it * u + k for k in range(u)]
            run([i % d + (1 + i // d) * stream_rows for i in idx],
                lambda s: (kc_ref[rows(s - stream_rows), :], vc_ref[rows(s - stream_rows), :]), bias2)
            return carry

        lax.fori_loop(0, d // _unroll(d), first, 0)
        if n_blocks > 1:
            lax.fori_loop(0, n_rest // _unroll(n_rest), rest, 0)

    lses = [lse_ref[g] for g in range(n_groups)]
    mx = functools.reduce(jnp.maximum, lses)
    es = [jnp.exp(l - mx) for l in lses]
    den = functools.reduce(lambda a, b: a + b, es)
    acc = (es[0] / den) * og_ref[0]
    for g in range(1, n_groups):
        acc = acc + (es[g] / den) * og_ref[g]
    o_ref[...] = acc


def dilated_attention_prompt(qkv, bias, *, heads, hd, tile):
    B, T, _ = qkv.shape
    dils = A_DILATIONS
    n_groups = len(dils)
    tile = min(tile, T)
    assert hd == LANES and T % tile == 0 and all(tile % (A_BLOCK * d) == 0 for d in dils)
    sec = n_groups * heads

    def cur(which, g):
        return pl.BlockSpec((None, tile, hd), lambda b, n, h: (b, n, which * sec + g * heads + h))

    def prev(which, g):
        rows = A_BLOCK * dils[g]
        per = tile // rows
        return pl.BlockSpec((None, rows, hd),
                            lambda b, n, h: (b, jnp.maximum(n * per - 1, 0), which * sec + g * heads + h))

    in_specs = []
    for g in range(n_groups):
        in_specs += [cur(0, g), cur(1, g), cur(2, g), prev(1, g), prev(2, g)]
    in_specs.append(pl.BlockSpec((n_groups, None, A_BLOCK, 2 * A_BLOCK), lambda b, n, h: (0, h, 0, 0)))
    return pl.pallas_call(
        functools.partial(_dil_attn_body, dils=dils, tile=tile, hd=hd),
        grid=(B, T // tile, heads),
        in_specs=in_specs,
        out_specs=pl.BlockSpec((None, tile, hd), lambda b, n, h: (b, n, h)),
        out_shape=jax.ShapeDtypeStruct((B, T, heads * hd), F32),
        scratch_shapes=[pltpu.VMEM((n_groups, tile, hd), F32), pltpu.VMEM((n_groups, tile, hd), F32)],
        compiler_params=_cparams(3),
        name="dilated_attention_prompt",
    )(*([qkv] * (5 * n_groups)), bias)


def _dec_attn_body(qkv_ref, *refs, groups, span, scale):
    bufs, bias_ref, o_ref = refs[:groups], refs[groups], refs[groups + 1]
    outs, lses = [], []
    for g in range(groups):
        q = _bf_round(qkv_ref[g])
        kn = _bf_round(qkv_ref[groups + g])
        vn = _bf_round(qkv_ref[2 * groups + g])
        k = _bf_round(bufs[g][:, 0])
        v = _bf_round(bufs[g][:, 1])
        s = jnp.sum(k * q[None], axis=-1, keepdims=True) * scale + bias_ref[g, :span]
        s_self = jnp.sum(q * kn, axis=-1, keepdims=True) * scale + bias_ref[g, span]
        m = jnp.maximum(jnp.max(s, axis=0), s_self)
        p = jnp.exp(s - m[None])
        p_self = jnp.exp(s_self - m)
        l = jnp.sum(p, axis=0) + p_self
        outs.append(jnp.sum(_bf_round(p / l[None]) * v, axis=0) + _bf_round(p_self / l) * vn)
        lses.append(m + jnp.log(l))
    mx = functools.reduce(jnp.maximum, lses)
    es = [jnp.exp(l - mx) for l in lses]
    den = functools.reduce(lambda a, b: a + b, es)
    acc = (es[0] / den) * outs[0]
    for g in range(1, groups):
        acc = acc + (es[g] / den) * outs[g]
    o_ref[...] = acc


def dilated_attention_sample(qkv, caches, layer, bias, *, heads, hd, span):
    B = qkv.shape[0]
    groups = len(caches)
    in_specs = [pl.BlockSpec((None, 3 * groups, heads, hd), lambda b: (b, 0, 0, 0))]
    views = []
    for g, c in enumerate(caches):
        d = A_DILATIONS[g]
        assert c.shape[2] == span * d
        views.append(c.reshape(c.shape[0], B, span, d, 2, heads, hd))
        in_specs.append(pl.BlockSpec((None, None, span, None, 2, heads, hd),
                                     lambda b: (layer, b, 0, 0, 0, 0, 0)))
    in_specs.append(pl.BlockSpec((groups, span + 1, heads, hd), lambda b: (0, 0, 0, 0)))
    return pl.pallas_call(
        functools.partial(_dec_attn_body, groups=groups, span=span, scale=hd ** -0.5),
        grid=(B,),
        in_specs=in_specs,
        out_specs=pl.BlockSpec((None, heads, hd), lambda b: (b, 0, 0)),
        out_shape=jax.ShapeDtypeStruct((B, heads, hd), F32),
        compiler_params=_cparams(1),
        name="dilated_attention_sample",
    )(qkv, *views, bias)


def _to_window_body(*refs, heads, hd):
    o_ref = refs[-1]
    chunk = refs[0].shape[0]
    for layer in range(o_ref.shape[0]):
        for s in range(2):
            src = refs[2 * layer + s]
            for h in range(heads):
                o_ref[layer, pl.ds(s * heads + h, chunk, stride=2 * heads), :] = src[:, h * hd:(h + 1) * hd]


def to_window(qkvs, g, *, rows, groups, heads, hd):
    B, T, _ = qkvs[0].shape
    W = heads * hd
    chunk = _tile(rows, TILES["window_relayout_rows"], SUBLANES)
    assert (T - rows) % chunk == 0
    first = (T - rows) // chunk
    in_specs, args = [], []
    for qkv in qkvs:
        in_specs += [pl.BlockSpec((None, chunk, W), lambda b, c: (b, first + c, groups + g)),
                     pl.BlockSpec((None, chunk, W), lambda b, c: (b, first + c, 2 * groups + g))]
        args += [qkv, qkv]
    per_token = 2 * heads
    out = pl.pallas_call(
        functools.partial(_to_window_body, heads=heads, hd=hd),
        grid=(B, rows // chunk),
        in_specs=in_specs,
        out_specs=pl.BlockSpec((len(qkvs), None, chunk * per_token, hd), lambda b, c: (0, b, c, 0)),
        out_shape=jax.ShapeDtypeStruct((len(qkvs), B, rows * per_token, hd), F32),
        compiler_params=_cparams(2),
        name="to_window",
    )(*args)
    return out.reshape(len(qkvs), B, rows, 2, heads, hd)


def _window_update_body(main_ref, next_ref, new_ref, o_ref):
    rows = main_ref.shape[0]
    last_chunk = pl.program_id(2) == pl.num_programs(2) - 1
    o_ref[0:rows - 1] = main_ref[1:rows]
    o_ref[rows - 1:rows] = jnp.where(last_chunk, new_ref[...], next_ref[...])


def window_update(cache, new, *, chunk):
    n_layers, B, rows = cache.shape[:3]
    tail = cache.shape[3:]
    chunk = _tile(rows, chunk, 1)
    zeros = (0,) * len(tail)
    return pl.pallas_call(
        _window_update_body,
        grid=(n_layers, B, rows // chunk),
        in_specs=[pl.BlockSpec((None, None, chunk) + tail, lambda j, b, c: (j, b, c) + zeros),
                  pl.BlockSpec((None, None, 1) + tail,
                               lambda j, b, c: (j, b, jnp.minimum((c + 1) * chunk, rows - 1)) + zeros),
                  pl.BlockSpec((None, None, 1) + tail, lambda j, b, c: (j, b, 0) + zeros)],
        out_specs=pl.BlockSpec((None, None, chunk) + tail, lambda j, b, c: (j, b, c) + zeros),
        out_shape=jax.ShapeDtypeStruct(cache.shape, cache.dtype),
        compiler_params=_cparams(3),
        name="window_update",
    )(cache, cache, new)


def _gla_gate_body(x_ref, g_ref, wl_ref, wg_ref, b_ref, o_ref):
    xn = _rms(x_ref[...], g_ref[...]).astype(BF16)
    low = _dot(xn, wl_ref[...].astype(BF16))
    z = _dot(low.astype(BF16), wg_ref[...].astype(BF16)) + b_ref[...]
    log_sig = jnp.minimum(z, 0.0) - jnp.log(1.0 + jnp.exp(-jnp.abs(z)))
    o_ref[...] = log_sig / GLA_GATE_NORMALIZER


def gla_gate(x, gain, w_low, w_gate, b_gate, layer, *, bm):
    M, D = x.shape
    g_arr, g_idx = gain
    rank, dk = w_gate.shape[1], w_gate.shape[2]
    bm = _tile(M, bm, SUBLANES)
    return pl.pallas_call(
        _gla_gate_body,
        grid=(M // bm,),
        in_specs=[pl.BlockSpec((bm, D), lambda i: (i, 0)),
                  pl.BlockSpec((None, 1, D), lambda i: (g_idx, 0, 0)),
                  pl.BlockSpec((D, rank), lambda i: (0, 0)),
                  pl.BlockSpec((None, rank, dk), lambda i: (layer, 0, 0)),
                  pl.BlockSpec((None, 1, dk), lambda i: (layer, 0, 0))],
        out_specs=pl.BlockSpec((bm, dk), lambda i: (i, 0)),
        out_shape=jax.ShapeDtypeStruct((M, dk), F32),
        compiler_params=_cparams(1),
        name="gla_gate",
    )(x, g_arr, w_low, w_gate, b_gate)


def _cumsum_rows(x):
    rows = x.shape[0]
    row = lax.broadcasted_iota(jnp.int32, x.shape, 0)
    s = 1
    while s < rows:
        x = x + jnp.where(row >= s, pltpu.roll(x, s, axis=0), 0.0)
        s *= 2
    return x


def _block_refs(bcs, m, pairs):
    rows, dk = bcs.shape
    nb = rows // m
    ends = [bcs[(k + 1) * m - 1:(k + 1) * m, :] for k in range(nb)]
    parts = []
    for k in range(nb):
        if pairs:
            r = ends[k - 1] if k % 2 == 1 else ends[k]
        else:
            r = ends[k - 1] if k > 0 else jnp.zeros((1, dk), F32)
        parts.append(jnp.broadcast_to(r, (m, dk)))
    return parts[0] if nb == 1 else jnp.concatenate(parts, axis=0)


def _la_body(a_ref, b_ref, c_ref, gate_ref, e_ref, gn_ref, s0_ref, y_ref, so_ref, st_ref,
             *, hgrn2, heads, dk, dv, chunk, base, valid, q_scale):
    n = pl.program_id(1)

    @pl.when(n == 0)
    def _():
        for h in range(heads):
            st_ref[h] = s0_ref[h].T

    tq = lax.broadcasted_iota(jnp.int32, (chunk, chunk), 0)
    ts = lax.broadcasted_iota(jnp.int32, (chunk, chunk), 1)
    row = lax.broadcasted_iota(jnp.int32, (chunk, 1), 0)
    sh = int(math.log2(base))
    mask_diag = jnp.logical_and((tq >> sh) == (ts >> sh), ts <= tq)
    levels = []
    m = base
    while m < chunk:
        s1 = int(math.log2(m))
        mk = jnp.logical_and((tq >> (s1 + 1)) == (ts >> (s1 + 1)),
                             jnp.logical_and(((tq >> s1) & 1) == 1, ((ts >> s1) & 1) == 0))
        levels.append((m, mk))
        m *= 2
    gn = gn_ref[...]

    def one_head(h):
        ksl = slice(h * dk, (h + 1) * dk)
        vsl = slice(h * dv, (h + 1) * dv)
        if hgrn2:
            a = a_ref[:, ksl]
            lb = e_ref[:, ksl]
            f = lb + (1.0 - lb) * jax.nn.sigmoid(b_ref[:, ksl])
            qa = a * jax.nn.sigmoid(a)
            kk = 1.0 - f
            g = jnp.log(f)
        else:
            qa = a_ref[:, ksl] * q_scale
            kk = b_ref[:, ksl]
            g = e_ref[:, ksl]
        v = c_ref[:, vsl]
        if valid < chunk:
            kk = jnp.where(row < valid, kk, 0.0)
            g = jnp.where(row < valid, g, 0.0)
        bcs = _cumsum_rows(g)
        vb = v.astype(BF16)
        st = st_ref[h]

        o = _dot_nt((qa * jnp.exp(bcs)).astype(BF16), st.astype(BF16))

        ref_d = _block_refs(bcs, base, False)
        a_mat = _dot_nt((qa * jnp.exp(bcs - ref_d)).astype(BF16), (kk * jnp.exp(ref_d - bcs)).astype(BF16))
        a_mat = jnp.where(mask_diag, a_mat, 0.0)
        for (m, mk) in levels:
            dec = jnp.exp(-jnp.abs(bcs - _block_refs(bcs, m, True)))
            a_mat = a_mat + jnp.where(mk, _dot_nt((qa * dec).astype(BF16), (kk * dec).astype(BF16)), 0.0)
        o = o + _dot(a_mat.astype(BF16), vb)

        b_last = bcs[chunk - 1:chunk, :]
        kd = (kk * jnp.exp(b_last - bcs)).astype(BF16)
        st_new = jnp.exp(b_last) * st + _dot_tn(vb, kd)

        gate = gate_ref[:, vsl]
        return st_new, _rms(o, gn) * (gate * jax.nn.sigmoid(gate))

    for h in range(heads):
        st_ref[h], y_ref[:, h * dv:(h + 1) * dv] = one_head(h)

    @pl.when(n == pl.num_programs(1) - 1)
    def _():
        for h in range(heads):
            so_ref[h] = st_ref[h].T


def gated_linear_attention(proj, extra, gnorm, s0, *, hgrn2, chunk, valid=None):
    B, T, _ = proj.shape
    _, H, dk, dv = s0.shape
    chunk = min(chunk, T)
    assert T % chunk == 0
    base = min(LA_BASE, chunk)
    valid = chunk if valid is None else valid
    wk, wv = H * dk, H * dv

    def col(width, start):
        assert start % width == 0
        return pl.BlockSpec((None, chunk, width), lambda b, n: (b, n, start // width))

    if hgrn2:
        specs = [col(wk, 0), col(wk, wk), col(wv, 2 * wk), col(wv, 2 * wk + wv),
                 pl.BlockSpec((1, wk), lambda b, n: (0, 0))]
    else:
        specs = [col(wk, 0), col(wk, wk), col(wv, 2 * wk), col(wv, 2 * wk + wv),
                 pl.BlockSpec((None, chunk, wk), lambda b, n: (b, n, 0))]
    specs += [pl.BlockSpec((1, dv), lambda b, n: (0, 0)),
              pl.BlockSpec((None, H, dk, dv), lambda b, n: (b, 0, 0, 0))]
    return pl.pallas_call(
        functools.partial(_la_body, hgrn2=hgrn2, heads=H, dk=dk, dv=dv, chunk=chunk, base=base,
                          valid=valid, q_scale=1.0 if hgrn2 else dk ** -0.5),
        grid=(B, T // chunk),
        in_specs=specs,
        out_specs=[pl.BlockSpec((None, chunk, wv), lambda b, n: (b, n, 0)),
                   pl.BlockSpec((None, H, dk, dv), lambda b, n: (b, 0, 0, 0))],
        out_shape=[jax.ShapeDtypeStruct((B, T, wv), F32),
                   jax.ShapeDtypeStruct((B, H, dk, dv), F32)],
        scratch_shapes=[pltpu.VMEM((H, dv, dk), F32)],
        compiler_params=_cparams(2),
        name="gated_linear_attention",
    )(proj, proj, proj, proj, extra, gnorm, s0)


def _pad_rows(x, rows):
    return jnp.pad(x, ((0, 0), (0, rows - x.shape[1]), (0, 0)))


def _trunk(xp, xs, p_mem, s_mem, caches, b_states, c_states, P):
    Bp, T, D = xp.shape
    Bs = xs.shape[0]
    assert xs.shape[1] == 1 and Bs <= RIDER_ROWS
    Mp = Bp * T
    depth = P["norm_mix"].shape[0]
    hd_a, hpg = P["a_hd"], P["a_hpg"]
    groups = len(A_DILATIONS)
    Wa = hpg * hd_a
    tile, mlp_tile = TILES["in_proj"], TILES["mlp"]

    def ride(h):
        return jnp.pad(h, ((0, RIDER_ROWS - Bs), (0, 0)))

    hp, hs = xp.reshape(Mp, D), xs.reshape(Bs, D)
    qkv_prompt, new_rows, p_b, s_b, p_c, s_c = [], [], [], [], [], []
    for li in range(depth):
        kind, j = li % N_MIXERS, li // N_MIXERS
        gmix = (P["norm_mix"], li)
        if kind == 0:
            qkv_p, qkv_s = linear(hp, P["a_w_in"], j, gain=gmix, rider=ride(hs), **tile)
            qkv3 = qkv_p.reshape(Bp, T, 3 * groups * Wa)
            yp = dilated_attention_prompt(qkv3, P["a_bias_prompt"], heads=hpg, hd=hd_a,
                                          tile=TILES["attention_rows"])
            qkv_prompt.append(qkv3)
            qkv5 = qkv_s[:Bs].reshape(Bs, 3, groups, hpg, hd_a)
            ys = dilated_attention_sample(qkv5.reshape(Bs, 3 * groups, hpg, hd_a), caches, j,
                                          P["a_bias_sample"], heads=hpg, hd=hd_a, span=A_BLOCK)
            new_rows.append([qkv5[:, 1:3, g][:, None] for g in range(groups)])
            ys = ys.reshape(Bs, Wa)
            w_mix = P["a_w_out"]
        elif kind == 1:
            proj_p, proj_s = linear(hp, P["b_w_in"], j, gain=gmix, rider=ride(hs), **tile)
            lb = P["lower_bounds"][li][None, :]
            gn = P["b_gnorm"][j][None, :]
            yp, S = gated_linear_attention(proj_p.reshape(Bp, T, -1), lb, gn,
                                           jnp.zeros((Bp,) + b_states.shape[2:], F32), hgrn2=True, chunk=LA_CHUNK)
            p_b.append(S)
            ys, S = gated_linear_attention(_pad_rows(proj_s[:Bs, None], SUBLANES), lb, gn, b_states[j],
                                           hgrn2=True, chunk=SUBLANES, valid=1)
            s_b.append(S)
            w_mix = P["b_w_out"]
        else:
            wk = P["c_wk"]
            n_main = P["c_n_main"]
            proj_p, proj_s = linear(hp, P["c_w_in_t"], j, n_out=n_main, gain=gmix, rider=ride(hs),
                                    transposed=True, **tile)
            gate_args = (gmix, P["c_w_low"][j], P["c_w_gate"], P["c_b_gate"], j)
            gn = P["c_gnorm"][j][None, :]
            yp, S = gated_linear_attention(proj_p.reshape(Bp, T, n_main),
                                           gla_gate(hp, *gate_args, bm=TILES["gate_rows"]).reshape(Bp, T, wk), gn,
                                           jnp.zeros((Bp,) + c_states.shape[2:], F32), hgrn2=False, chunk=LA_CHUNK)
            p_c.append(S)
            ys, S = gated_linear_attention(_pad_rows(proj_s[:Bs, None], SUBLANES),
                                           _pad_rows(gla_gate(hs, *gate_args, bm=SUBLANES)[:, None], SUBLANES), gn,
                                           c_states[j], hgrn2=False, chunk=SUBLANES, valid=1)
            s_c.append(S)
            w_mix = P["c_w_out"]

        mem_kw = dict(heads=P["m_heads"], hd=P["m_hd"])
        hp = mix_out_mem_attend(yp.reshape(Bp, T, -1), w_mix, j, hp.reshape(Bp, T, D), P["norm_mem"], P["mem_w_q"],
                                p_mem, P["mem_w_out"], li, bm=TILES["mix_mem_rows"], **mem_kw).reshape(Mp, D)
        if ys.ndim == 3:
            ys = ys[:, 0]
        hs = mix_out_mem_attend_sample(ys, w_mix, j, hs, P["norm_mem"], P["mem_w_q"], s_mem, P["mem_w_out"], li,
                                       **mem_kw)
        final = P["norm_final"] if li == depth - 1 else None
        hp, hs = mlp(hp, P["norm_mlp"], P["mlp_w_up"], P["mlp_w_down"], li, final_gain=final,
                     rider=ride(hs), **mlp_tile)
        hs = hs[:Bs]

    new_rows = [jnp.stack([rows[g] for rows in new_rows], axis=0) for g in range(groups)]
    p_win = [to_window(qkv_prompt, g, rows=min(A_WINDOWS[g], T), groups=groups, heads=hpg, hd=hd_a)
             for g in range(groups)]
    return (hp.reshape(Bp, T, D), hs.reshape(Bs, 1, D), p_win, new_rows,
            jnp.stack(p_b, axis=0), jnp.stack(s_b, axis=0), jnp.stack(p_c, axis=0), jnp.stack(s_c, axis=0))


def _bias_tables(rel_bias_table, hpg, span):
    steps = np.arange(A_BLOCK)[:, None] + A_BLOCK - np.arange(2 * A_BLOCK)[None, :]
    eye = np.eye(N_BUCKETS, dtype=np.float32)
    prompt, sample = [], []
    for g, d in enumerate(A_DILATIONS):
        tg_t = rel_bias_table[:, g * hpg:(g + 1) * hpg].T
        onehot = eye[:, t5_bucket(np.clip(steps, 0, span) * d).reshape(-1)]
        bias = jnp.dot(tg_t, jnp.asarray(onehot), precision=lax.Precision.HIGHEST).reshape(hpg, A_BLOCK, 2 * A_BLOCK)
        band = jnp.asarray((steps >= 0) & (steps <= span))
        prompt.append(jnp.where(band[None], bias, -jnp.inf))
        dist = np.concatenate([(span - np.arange(span)) * d, np.zeros(1, np.int64)])
        row = jnp.dot(tg_t, jnp.asarray(eye[:, t5_bucket(dist)]), precision=lax.Precision.HIGHEST)
        sample.append(jnp.broadcast_to(row.T[:, :, None], (span + 1, hpg, LANES)))
    return jnp.stack(prompt, axis=0), jnp.stack(sample, axis=0)


def kernel(x_prompt, x_sample, cache_win0, cache_win1, cache_win2, state_hgrn, state_gla, cache_mem, mem_prompt, rel_bias_table, a_w_in, a_w_out, b_w_in, b_w_out, b_lower_bound, b_gnorm, c_w_in, c_w_gate, c_b_gate, c_w_out, c_gnorm, mem_norm, mem_w_kv, mem_w_q, mem_w_out, norm_mix, norm_mem, norm_mlp, mlp_w_up, mlp_w_down, norm_final):
    depth, D = norm_mix.shape
    hpg, hd_a = cache_win0.shape[-2], cache_win0.shape[-1]
    span = A_BLOCK
    assert all(w // d == span for w, d in zip(A_WINDOWS, A_DILATIONS))
    m_heads, m_hd = cache_mem.shape[-2], cache_mem.shape[-1]
    c_heads, c_dk, c_dv = state_gla.shape[2:]
    wk = c_heads * c_dk
    n_main = 2 * wk + 2 * c_heads * c_dv
    bias_prompt, bias_sample = _bias_tables(rel_bias_table, hpg, span)

    sm = jax.nn.softmax(b_lower_bound.astype(F32), axis=0)
    P = dict(
        a_hd=hd_a, a_hpg=hpg, m_heads=m_heads, m_hd=m_hd,
        a_bias_prompt=bias_prompt, a_bias_sample=bias_sample,
        a_w_in=a_w_in, a_w_out=a_w_out.astype(BF16), b_w_in=b_w_in, b_w_out=b_w_out.astype(BF16),
        lower_bounds=jnp.cumsum(sm, axis=0) - sm, b_gnorm=b_gnorm,
        c_w_in_t=jnp.swapaxes(c_w_in, 1, 2), c_w_low=c_w_in[:, :, n_main:], c_w_gate=c_w_gate,
        c_b_gate=c_b_gate[:, None, :], c_w_out=c_w_out.astype(BF16), c_gnorm=c_gnorm,
        c_wk=wk, c_n_main=n_main,
        mem_w_q=mem_w_q.astype(BF16), mem_w_out=mem_w_out.astype(BF16),
        norm_mix=norm_mix[:, None, :], norm_mem=norm_mem[:, None, :], norm_mlp=norm_mlp[:, None, :],
        mlp_w_up=mlp_w_up, mlp_w_down=mlp_w_down, norm_final=norm_final[None, :],
    )

    Bm, Mt, _ = mem_prompt.shape
    mem_flat = mem_prompt.reshape(Bm * Mt, D)
    mem_gain = mem_norm[:, None, :]
    p_mem = jnp.stack([linear(mem_flat, mem_w_kv, li, gain=(mem_gain, li), **TILES["mem_kv"])
                       for li in range(depth)], axis=0).reshape(depth, Bm, Mt, 2 * m_heads * m_hd)

    caches = (cache_win0, cache_win1, cache_win2)
    y_prompt, y_sample, p_win, new_rows, p_hgrn, s_hgrn, p_gla, s_gla = _trunk(
        x_prompt, x_sample, p_mem, cache_mem, caches, state_hgrn, state_gla, P)
    s_win = [window_update(c, n, chunk=TILES["window_copy_rows"]) for c, n in zip(caches, new_rows)]
    return (y_prompt, y_sample, p_win[0], p_win[1], p_win[2], p_hgrn, p_gla,
            p_mem.reshape(depth, Bm, Mt, 2, m_heads, m_hd),
            s_win[0], s_win[1], s_win[2], s_hgrn, s_gla)
```

```python
import functools
import math

import numpy as np
import jax
import jax.numpy as jnp
from jax import lax
from jax.experimental import pallas as pl
from jax.experimental.pallas import tpu as pltpu

F32 = jnp.float32
BF16 = jnp.bfloat16

EPS = 1e-6
N_MIXERS = 3
A_WINDOWS = (128, 512, 2048)
A_DILATIONS = (1, 4, 16)
A_BLOCK = 128
N_BUCKETS = 32
MAX_DISTANCE = 2048
GLA_GATE_NORMALIZER = 16.0
LA_BASE = 16
LA_CHUNK = 128
RIDER_ROWS = 16

TILES = dict(
    in_proj=dict(bm=2048, bn=512),
    mlp=dict(bm=1024, bf=512),
    mix_mem_rows=512,
    gate_rows=1024,
    attention_rows=16 * A_BLOCK,
    mem_kv=dict(bm=512, bn=512),
    window_copy_rows=512,
    window_relayout_rows=256,
)

V7X_VMEM_BYTES = 64 * 1024 * 1024
VMEM_LIMIT = V7X_VMEM_BYTES - 8 * 1024 * 1024
SUBLANES = 8
LANES = 128


def _cparams(n_axes):
    return pltpu.CompilerParams(dimension_semantics=("arbitrary",) * n_axes,
                                vmem_limit_bytes=VMEM_LIMIT)


def _tile(n, target, align):
    if n <= target:
        return n
    t = (target // align) * align
    while n % t:
        t -= align
    return t


def _rms(x, g):
    r = lax.rsqrt(jnp.mean(x * x, axis=-1, keepdims=True) + EPS)
    return (x * r) * g


def _dot(a, b):
    return jnp.dot(a, b, preferred_element_type=F32)


def _dot_nt(a, b):
    return lax.dot_general(a, b, (((1,), (1,)), ((), ())), preferred_element_type=F32)


def _dot_tn(a, b):
    return lax.dot_general(a, b, (((0,), (0,)), ((), ())), preferred_element_type=F32)


def _bf_round(a):
    return a.astype(BF16).astype(F32)


def _linear_body(*refs, riders, transposed):
    if riders:
        x_ref, w_ref, g_ref, r_ref, o_ref, or_ref, xs_ref = refs
    else:
        x_ref, w_ref, g_ref, o_ref, xs_ref = refs
    bm = x_ref.shape[0]

    @pl.when(pl.program_id(1) == 0)
    def _():
        xs_ref[0:bm] = _rms(x_ref[...], g_ref[...]).astype(BF16)
        if riders:
            xs_ref[bm:bm + riders] = _rms(r_ref[...], g_ref[...]).astype(BF16)

    w = w_ref[...].astype(BF16)
    acc = _dot_nt(xs_ref[...], w) if transposed else _dot(xs_ref[...], w)
    if riders:
        o_ref[...] = acc[:bm]
        or_ref[...] = acc[bm:]
    else:
        o_ref[...] = acc


def linear(x, w, layer, *, n_out=None, gain, rider=None, transposed=False, bm, bn):
    M, K = x.shape
    N = (w.shape[1] if transposed else w.shape[2]) if n_out is None else n_out
    bm, bn = _tile(M, bm, SUBLANES), _tile(N, bn, LANES)
    riders = 0 if rider is None else rider.shape[0]
    g_arr, g_idx = gain
    w_spec = (pl.BlockSpec((None, bn, K), lambda i, j: (layer, j, 0)) if transposed
              else pl.BlockSpec((None, K, bn), lambda i, j: (layer, 0, j)))
    in_specs = [pl.BlockSpec((bm, K), lambda i, j: (i, 0), pipeline_mode=pl.Buffered(1)),
                w_spec,
                pl.BlockSpec((None, 1, K), lambda i, j: (g_idx, 0, 0))]
    args = [x, w, g_arr]
    out_specs = [pl.BlockSpec((bm, bn), lambda i, j: (i, j))]
    out_shape = [jax.ShapeDtypeStruct((M, N), F32)]
    if riders:
        in_specs.append(pl.BlockSpec((riders, K), lambda i, j: (0, 0)))
        args.append(rider)
        out_specs.append(pl.BlockSpec((None, riders, bn), lambda i, j: (i, 0, j)))
        out_shape.append(jax.ShapeDtypeStruct((M // bm, riders, N), F32))
    outs = pl.pallas_call(
        functools.partial(_linear_body, riders=riders, transposed=transposed),
        grid=(M // bm, N // bn),
        in_specs=in_specs,
        out_specs=out_specs,
        out_shape=out_shape,
        scratch_shapes=[pltpu.VMEM((bm + riders, K), BF16)],
        compiler_params=_cparams(2),
        name="linear",
    )(*args)
    return (outs[0], outs[1][0]) if riders else outs[0]


def _mlp_body(*refs, has_final, riders):
    refs = list(refs)
    x_ref, g_ref, wu_ref, wd_ref = refs[:4]
    pos = 4
    fg_ref = r_ref = or_ref = None
    if has_final:
        fg_ref = refs[pos]
        pos += 1
    if riders:
        r_ref = refs[pos]
        pos += 1
    o_ref = refs[pos]
    if riders:
        or_ref = refs[pos + 1]
    xs_ref = refs[-1]
    bm = x_ref.shape[0]
    f = pl.program_id(1)

    @pl.when(f == 0)
    def _():
        x = x_ref[...]
        xs_ref[0:bm] = _rms(x, g_ref[...]).astype(BF16)
        o_ref[...] = x
        if riders:
            r = r_ref[...]
            xs_ref[bm:bm + riders] = _rms(r, g_ref[...]).astype(BF16)
            or_ref[...] = r

    def up_down(xs):
        hdn = _dot(xs, wu_ref[...].astype(BF16))
        hdn = jnp.square(jnp.maximum(hdn, 0.0)).astype(BF16)
        return _dot(hdn, wd_ref[...].astype(BF16))

    if riders:
        @pl.when(pl.program_id(0) == 0)
        def _():
            down = up_down(xs_ref[...])
            o_ref[...] += down[:bm]
            or_ref[...] += down[bm:]

        @pl.when(pl.program_id(0) != 0)
        def _():
            o_ref[...] += up_down(xs_ref[0:bm])
    else:
        o_ref[...] += up_down(xs_ref[...])

    if has_final:
        @pl.when(f == pl.num_programs(1) - 1)
        def _():
            o_ref[...] = _rms(o_ref[...], fg_ref[...])
            if riders:
                or_ref[...] = _rms(or_ref[...], fg_ref[...])


def mlp(x, gains, w_up, w_down, layer, *, final_gain=None, rider=None, bm, bf):
    M, D = x.shape
    FF = w_up.shape[-1]
    bm, bf = _tile(M, bm, SUBLANES), _tile(FF, bf, LANES)
    riders = 0 if rider is None else rider.shape[0]
    in_specs = [pl.BlockSpec((bm, D), lambda i, f: (i, 0), pipeline_mode=pl.Buffered(1)),
                pl.BlockSpec((None, 1, D), lambda i, f: (layer, 0, 0)),
                pl.BlockSpec((None, D, bf), lambda i, f: (layer, 0, f)),
                pl.BlockSpec((None, bf, D), lambda i, f: (layer, f, 0))]
    args = [x, gains, w_up, w_down]
    if final_gain is not None:
        in_specs.append(pl.BlockSpec((1, D), lambda i, f: (0, 0)))
        args.append(final_gain)
    out_specs = [pl.BlockSpec((bm, D), lambda i, f: (i, 0), pipeline_mode=pl.Buffered(1))]
    out_shape = [jax.ShapeDtypeStruct((M, D), F32)]
    if riders:
        in_specs.append(pl.BlockSpec((riders, D), lambda i, f: (0, 0)))
        args.append(rider)
        out_specs.append(pl.BlockSpec((None, riders, D), lambda i, f: (i, 0, 0)))
        out_shape.append(jax.ShapeDtypeStruct((M // bm, riders, D), F32))
    outs = pl.pallas_call(
        functools.partial(_mlp_body, has_final=final_gain is not None, riders=riders),
        grid=(M // bm, FF // bf),
        in_specs=in_specs,
        out_specs=out_specs,
        out_shape=out_shape,
        scratch_shapes=[pltpu.VMEM((bm + riders, D), BF16)],
        compiler_params=_cparams(2),
        name="mlp",
    )(*args)
    return (outs[0], outs[1][0]) if riders else outs[0]


def _mem_body(y_ref, wm_ref, h_ref, g_ref, wq_ref, kv_ref, wo_ref, o_ref, *, heads, hd):
    x = h_ref[...] + _dot(y_ref[...].astype(BF16), wm_ref[...])
    xn = _rms(x, g_ref[...]).astype(BF16)
    q = _dot(xn, wq_ref[...])
    width = heads * hd
    scale = hd ** -0.5
    outs = []
    for h in range(heads):
        qh = q[:, h * hd:(h + 1) * hd].astype(BF16)
        kh = kv_ref[:, h * hd:(h + 1) * hd].astype(BF16)
        vh = kv_ref[:, width + h * hd:width + (h + 1) * hd].astype(BF16)
        s = _dot_nt(qh, kh) * scale
        p = jnp.exp(s - jnp.max(s, axis=-1, keepdims=True))
        p = p / jnp.sum(p, axis=-1, keepdims=True)
        outs.append(_dot(p.astype(BF16), vh))
    o = jnp.concatenate(outs, axis=-1).astype(BF16)
    o_ref[...] = x + _dot(o, wo_ref[...])


def mix_out_mem_attend(y, w_mix, mix_layer, h, gains, w_q, kv, w_o, layer, *, heads, hd, bm):
    B, T, D = h.shape
    Ky = y.shape[-1]
    Mt = kv.shape[2]
    width = heads * hd
    bm = _tile(T, bm, SUBLANES)
    return pl.pallas_call(
        functools.partial(_mem_body, heads=heads, hd=hd),
        grid=(B, T // bm),
        in_specs=[pl.BlockSpec((None, bm, Ky), lambda b, i: (b, i, 0)),
                  pl.BlockSpec((None, Ky, D), lambda b, i: (mix_layer, 0, 0), pipeline_mode=pl.Buffered(1)),
                  pl.BlockSpec((None, bm, D), lambda b, i: (b, i, 0)),
                  pl.BlockSpec((None, 1, D), lambda b, i: (layer, 0, 0)),
                  pl.BlockSpec((None, D, width), lambda b, i: (layer, 0, 0), pipeline_mode=pl.Buffered(1)),
                  pl.BlockSpec((None, None, Mt, 2 * width), lambda b, i: (layer, b, 0, 0)),
                  pl.BlockSpec((None, width, D), lambda b, i: (layer, 0, 0), pipeline_mode=pl.Buffered(1))],
        out_specs=pl.BlockSpec((None, bm, D), lambda b, i: (b, i, 0)),
        out_shape=jax.ShapeDtypeStruct((B, T, D), F32),
        compiler_params=_cparams(2),
        name="mix_out_mem_attend",
    )(y, w_mix, h, gains, w_q, kv, w_o)


def _mem_sample_body(y_ref, wm_ref, h_ref, g_ref, wq_ref, kv_ref, wo_ref, o_ref, *, heads, hd):
    batch = h_ref.shape[0]
    scale = hd ** -0.5
    x = h_ref[...] + _dot(y_ref[...].astype(BF16), wm_ref[...])
    q = _bf_round(_dot(_rms(x, g_ref[...]).astype(BF16), wq_ref[...]))
    outs = []
    for b in range(batch):
        qb = jnp.concatenate([q[b:b + 1, h * hd:(h + 1) * hd] for h in range(heads)], axis=0)
        k = _bf_round(kv_ref[b, :, 0])
        v = _bf_round(kv_ref[b, :, 1])
        s = jnp.sum(k * qb[None], axis=-1, keepdims=True) * scale
        p = jnp.exp(s - jnp.max(s, axis=0))
        p = _bf_round(p / jnp.sum(p, axis=0))
        ob = jnp.sum(p * v, axis=0)
        outs.append(jnp.concatenate([ob[h:h + 1] for h in range(heads)], axis=1))
    o = jnp.concatenate(outs, axis=0).astype(BF16)
    o_ref[...] = x + _dot(o, wo_ref[...])


def mix_out_mem_attend_sample(y, w_mix, mix_layer, h, gains, w_q, kv, w_o, layer, *, heads, hd):
    B, D = h.shape
    Ky = y.shape[-1]
    Mt = kv.shape[2]
    width = heads * hd
    one = pl.Buffered(1)
    return pl.pallas_call(
        functools.partial(_mem_sample_body, heads=heads, hd=hd),
        grid=(1,),
        in_specs=[pl.BlockSpec((B, Ky), lambda i: (0, 0)),
                  pl.BlockSpec((None, Ky, D), lambda i: (mix_layer, 0, 0), pipeline_mode=one),
                  pl.BlockSpec((B, D), lambda i: (0, 0)),
                  pl.BlockSpec((None, 1, D), lambda i: (layer, 0, 0)),
                  pl.BlockSpec((None, D, width), lambda i: (layer, 0, 0), pipeline_mode=one),
                  pl.BlockSpec((None, B, Mt, 2, heads, hd), lambda i: (layer, 0, 0, 0, 0, 0), pipeline_mode=one),
                  pl.BlockSpec((None, width, D), lambda i: (layer, 0, 0), pipeline_mode=one)],
        out_specs=pl.BlockSpec((B, D), lambda i: (0, 0)),
        out_shape=jax.ShapeDtypeStruct((B, D), F32),
        compiler_params=_cparams(1),
        name="mix_out_mem_attend_sample",
    )(y, w_mix, h, gains, w_q, kv, w_o)


def t5_bucket(dist):
    dist = np.asarray(dist, np.int64)
    max_exact = N_BUCKETS // 2
    d = np.maximum(dist, 1).astype(np.float64)
    large = max_exact + (np.log(d / max_exact) / math.log(MAX_DISTANCE / max_exact)
                         * (N_BUCKETS - max_exact)).astype(np.int64)
    return np.where(dist < max_exact, dist, np.minimum(large, N_BUCKETS - 1)).astype(np.int32)


def _attn_block(q, k2, v2, bias2, scale):
    blk = q.shape[0]
    s = _dot_nt(q.astype(BF16), k2.astype(BF16)) * scale + bias2
    m = jnp.max(jnp.maximum(s[:, :blk], s[:, blk:]), axis=-1, keepdims=True)
    p = jnp.exp(s - m)
    l = jnp.sum(p[:, :blk] + p[:, blk:], axis=-1, keepdims=True)
    o = _dot(p.astype(BF16), v2.astype(BF16))
    return o / l, m + jnp.log(l)


def _dil_attn_body(*refs, dils, tile, hd):
    n_groups = len(dils)
    bias_ref, o_ref, og_ref, lse_ref = refs[5 * n_groups:]
    not_first_tile = pl.program_id(1) > 0
    blk = A_BLOCK
    scale = hd ** -0.5

    for g, d in enumerate(dils):
        q_ref, kc_ref, vc_ref, kp_ref, vp_ref = refs[5 * g:5 * g + 5]
        stream_rows = blk * d
        n_blocks = tile // stream_rows

        def rows(start):
            return pl.ds(start, blk) if d == 1 else pl.ds(start, blk, stride=d)

        bias2 = bias_ref[g]
        bias2_first = jnp.concatenate(
            [jnp.where(not_first_tile, bias2[:, :blk], -jnp.inf), bias2[:, blk:]], axis=1)

        for m in range(n_blocks):
            for r in range(d):
                s = r + m * stream_rows
                if m == 0:
                    kp, vp, bias = kp_ref[rows(s), :], vp_ref[rows(s), :], bias2_first
                else:
                    kp, vp, bias = kc_ref[rows(s - stream_rows), :], vc_ref[rows(s - stream_rows), :], bias2
                k2 = jnp.concatenate([kp, kc_ref[rows(s), :]], axis=0)
                v2 = jnp.concatenate([vp, vc_ref[rows(s), :]], axis=0)
                o, lse = _attn_block(q_ref[rows(s), :], k2, v2, bias, scale)
                og_ref[g, rows(s), :] = o
                lse_ref[g, rows(s), :] = jnp.broadcast_to(lse, (blk, hd))

    lses = [lse_ref[g] for g in range(n_groups)]
    mx = functools.reduce(jnp.maximum, lses)
    es = [jnp.exp(l - mx) for l in lses]
    den = functools.reduce(lambda a, b: a + b, es)
    acc = (es[0] / den) * og_ref[0]
    for g in range(1, n_groups):
        acc = acc + (es[g] / den) * og_ref[g]
    o_ref[...] = acc


def dilated_attention_prompt(qkv, bias, *, heads, hd, tile):
    B, T, _ = qkv.shape
    dils = A_DILATIONS
    n_groups = len(dils)
    tile = min(tile, T)
    assert hd == LANES and T % tile == 0 and all(tile % (A_BLOCK * d) == 0 for d in dils)
    sec = n_groups * heads

    def cur(which, g):
        return pl.BlockSpec((None, tile, hd), lambda b, n, h: (b, n, which * sec + g * heads + h))

    def prev(which, g):
        rows = A_BLOCK * dils[g]
        per = tile // rows
        return pl.BlockSpec((None, rows, hd),
                            lambda b, n, h: (b, jnp.maximum(n * per - 1, 0), which * sec + g * heads + h))

    in_specs = []
    for g in range(n_groups):
        in_specs += [cur(0, g), cur(1, g), cur(2, g), prev(1, g), prev(2, g)]
    in_specs.append(pl.BlockSpec((n_groups, None, A_BLOCK, 2 * A_BLOCK), lambda b, n, h: (0, h, 0, 0)))
    return pl.pallas_call(
        functools.partial(_dil_attn_body, dils=dils, tile=tile, hd=hd),
        grid=(B, T // tile, heads),
        in_specs=in_specs,
        out_specs=pl.BlockSpec((None, tile, hd), lambda b, n, h: (b, n, h)),
        out_shape=jax.ShapeDtypeStruct((B, T, heads * hd), F32),
        scratch_shapes=[pltpu.VMEM((n_groups, tile, hd), F32), pltpu.VMEM((n_groups, tile, hd), F32)],
        compiler_params=_cparams(3),
        name="dilated_attention_prompt",
    )(*([qkv] * (5 * n_groups)), bias)


def _dec_attn_body(qkv_ref, *refs, groups, span, scale):
    bufs, bias_ref, o_ref = refs[:groups], refs[groups], refs[groups + 1]
    outs, lses = [], []
    for g in range(groups):
        q = _bf_round(qkv_ref[g])
        kn = _bf_round(qkv_ref[groups + g])
        vn = _bf_round(qkv_ref[2 * groups + g])
        k = _bf_round(bufs[g][:, 0])
        v = _bf_round(bufs[g][:, 1])
        s = jnp.sum(k * q[None], axis=-1, keepdims=True) * scale + bias_ref[g, :span]
        s_self = jnp.sum(q * kn, axis=-1, keepdims=True) * scale + bias_ref[g, span]
        m = jnp.maximum(jnp.max(s, axis=0), s_self)
        p = jnp.exp(s - m[None])
        p_self = jnp.exp(s_self - m)
        l = jnp.sum(p, axis=0) + p_self
        outs.append(jnp.sum(_bf_round(p / l[None]) * v, axis=0) + _bf_round(p_self / l) * vn)
        lses.append(m + jnp.log(l))
    mx = functools.reduce(jnp.maximum, lses)
    es = [jnp.exp(l - mx) for l in lses]
    den = functools.reduce(lambda a, b: a + b, es)
    acc = (es[0] / den) * outs[0]
    for g in range(1, groups):
        acc = acc + (es[g] / den) * outs[g]
    o_ref[...] = acc


def dilated_attention_sample(qkv, caches, layer, bias, *, heads, hd, span):
    B = qkv.shape[0]
    groups = len(caches)
    in_specs = [pl.BlockSpec((None, 3 * groups, heads, hd), lambda b: (b, 0, 0, 0))]
    views = []
    for g, c in enumerate(caches):
        d = A_DILATIONS[g]
        assert c.shape[2] == span * d
        views.append(c.reshape(c.shape[0], B, span, d, 2, heads, hd))
        in_specs.append(pl.BlockSpec((None, None, span, None, 2, heads, hd),
                                     lambda b: (layer, b, 0, 0, 0, 0, 0)))
    in_specs.append(pl.BlockSpec((groups, span + 1, heads, hd), lambda b: (0, 0, 0, 0)))
    return pl.pallas_call(
        functools.partial(_dec_attn_body, groups=groups, span=span, scale=hd ** -0.5),
        grid=(B,),
        in_specs=in_specs,
        out_specs=pl.BlockSpec((None, heads, hd), lambda b: (b, 0, 0)),
        out_shape=jax.ShapeDtypeStruct((B, heads, hd), F32),
        compiler_params=_cparams(1),
        name="dilated_attention_sample",
    )(qkv, *views, bias)


def _to_window_body(*refs, heads, hd):
    o_ref = refs[-1]
    chunk = refs[0].shape[0]
    for layer in range(o_ref.shape[0]):
        for s in range(2):
            src = refs[2 * layer + s]
            for h in range(heads):
                o_ref[layer, pl.ds(s * heads + h, chunk, stride=2 * heads), :] = src[:, h * hd:(h + 1) * hd]


def to_window(qkvs, g, *, rows, groups, heads, hd):
    B, T, _ = qkvs[0].shape
    W = heads * hd
    chunk = _tile(rows, TILES["window_relayout_rows"], SUBLANES)
    assert (T - rows) % chunk == 0
    first = (T - rows) // chunk
    in_specs, args = [], []
    for qkv in qkvs:
        in_specs += [pl.BlockSpec((None, chunk, W), lambda b, c: (b, first + c, groups + g)),
                     pl.BlockSpec((None, chunk, W), lambda b, c: (b, first + c, 2 * groups + g))]
        args += [qkv, qkv]
    per_token = 2 * heads
    out = pl.pallas_call(
        functools.partial(_to_window_body, heads=heads, hd=hd),
        grid=(B, rows // chunk),
        in_specs=in_specs,
        out_specs=pl.BlockSpec((len(qkvs), None, chunk * per_token, hd), lambda b, c: (0, b, c, 0)),
        out_shape=jax.ShapeDtypeStruct((len(qkvs), B, rows * per_token, hd), F32),
        compiler_params=_cparams(2),
        name="to_window",
    )(*args)
    return out.reshape(len(qkvs), B, rows, 2, heads, hd)


def _window_update_body(main_ref, next_ref, new_ref, o_ref):
    rows = main_ref.shape[0]
    last_chunk = pl.program_id(2) == pl.num_programs(2) - 1
    o_ref[0:rows - 1] = main_ref[1:rows]
    o_ref[rows - 1:rows] = jnp.where(last_chunk, new_ref[...], next_ref[...])


def window_update(cache, new, *, chunk):
    n_layers, B, rows = cache.shape[:3]
    tail = cache.shape[3:]
    chunk = _tile(rows, chunk, 1)
    zeros = (0,) * len(tail)
    return pl.pallas_call(
        _window_update_body,
        grid=(n_layers, B, rows // chunk),
        in_specs=[pl.BlockSpec((None, None, chunk) + tail, lambda j, b, c: (j, b, c) + zeros),
                  pl.BlockSpec((None, None, 1) + tail,
                               lambda j, b, c: (j, b, jnp.minimum((c + 1) * chunk, rows - 1)) + zeros),
                  pl.BlockSpec((None, None, 1) + tail, lambda j, b, c: (j, b, 0) + zeros)],
        out_specs=pl.BlockSpec((None, None, chunk) + tail, lambda j, b, c: (j, b, c) + zeros),
        out_shape=jax.ShapeDtypeStruct(cache.shape, cache.dtype),
        compiler_params=_cparams(3),
        name="window_update",
    )(cache, cache, new)


def _gla_gate_body(x_ref, g_ref, wl_ref, wg_ref, b_ref, o_ref):
    xn = _rms(x_ref[...], g_ref[...]).astype(BF16)
    low = _dot(xn, wl_ref[...].astype(BF16))
    z = _dot(low.astype(BF16), wg_ref[...].astype(BF16)) + b_ref[...]
    log_sig = jnp.minimum(z, 0.0) - jnp.log(1.0 + jnp.exp(-jnp.abs(z)))
    o_ref[...] = log_sig / GLA_GATE_NORMALIZER


def gla_gate(x, gain, w_low, w_gate, b_gate, layer, *, bm):
    M, D = x.shape
    g_arr, g_idx = gain
    rank, dk = w_gate.shape[1], w_gate.shape[2]
    bm = _tile(M, bm, SUBLANES)
    return pl.pallas_call(
        _gla_gate_body,
        grid=(M // bm,),
        in_specs=[pl.BlockSpec((bm, D), lambda i: (i, 0)),
                  pl.BlockSpec((None, 1, D), lambda i: (g_idx, 0, 0)),
                  pl.BlockSpec((D, rank), lambda i: (0, 0)),
                  pl.BlockSpec((None, rank, dk), lambda i: (layer, 0, 0)),
                  pl.BlockSpec((None, 1, dk), lambda i: (layer, 0, 0))],
        out_specs=pl.BlockSpec((bm, dk), lambda i: (i, 0)),
        out_shape=jax.ShapeDtypeStruct((M, dk), F32),
        compiler_params=_cparams(1),
        name="gla_gate",
    )(x, g_arr, w_low, w_gate, b_gate)


def _cumsum_rows(x):
    rows = x.shape[0]
    row = lax.broadcasted_iota(jnp.int32, x.shape, 0)
    s = 1
    while s < rows:
        x = x + jnp.where(row >= s, pltpu.roll(x, s, axis=0), 0.0)
        s *= 2
    return x


def _block_refs(bcs, m, pairs):
    rows, dk = bcs.shape
    nb = rows // m
    ends = [bcs[(k + 1) * m - 1:(k + 1) * m, :] for k in range(nb)]
    parts = []
    for k in range(nb):
        if pairs:
            r = ends[k - 1] if k % 2 == 1 else ends[k]
        else:
            r = ends[k - 1] if k > 0 else jnp.zeros((1, dk), F32)
        parts.append(jnp.broadcast_to(r, (m, dk)))
    return parts[0] if nb == 1 else jnp.concatenate(parts, axis=0)


def _la_body(a_ref, b_ref, c_ref, gate_ref, e_ref, gn_ref, s0_ref, y_ref, so_ref, st_ref,
             *, hgrn2, heads, dk, dv, chunk, base, valid, q_scale):
    n = pl.program_id(1)

    @pl.when(n == 0)
    def _():
        for h in range(heads):
            st_ref[h] = s0_ref[h].T

    tq = lax.broadcasted_iota(jnp.int32, (chunk, chunk), 0)
    ts = lax.broadcasted_iota(jnp.int32, (chunk, chunk), 1)
    row = lax.broadcasted_iota(jnp.int32, (chunk, 1), 0)
    sh = int(math.log2(base))
    mask_diag = jnp.logical_and((tq >> sh) == (ts >> sh), ts <= tq)
    levels = []
    m = base
    while m < chunk:
        s1 = int(math.log2(m))
        mk = jnp.logical_and((tq >> (s1 + 1)) == (ts >> (s1 + 1)),
                             jnp.logical_and(((tq >> s1) & 1) == 1, ((ts >> s1) & 1) == 0))
        levels.append((m, mk))
        m *= 2
    gn = gn_ref[...]

    def one_head(h):
        ksl = slice(h * dk, (h + 1) * dk)
        vsl = slice(h * dv, (h + 1) * dv)
        if hgrn2:
            a = a_ref[:, ksl]
            lb = e_ref[:, ksl]
            f = lb + (1.0 - lb) * jax.nn.sigmoid(b_ref[:, ksl])
            qa = a * jax.nn.sigmoid(a)
            kk = 1.0 - f
            g = jnp.log(f)
        else:
            qa = a_ref[:, ksl] * q_scale
            kk = b_ref[:, ksl]
            g = e_ref[:, ksl]
        v = c_ref[:, vsl]
        if valid < chunk:
            kk = jnp.where(row < valid, kk, 0.0)
            g = jnp.where(row < valid, g, 0.0)
        bcs = _cumsum_rows(g)
        vb = v.astype(BF16)
        st = st_ref[h]

        o = _dot_nt((qa * jnp.exp(bcs)).astype(BF16), st.astype(BF16))

        ref_d = _block_refs(bcs, base, False)
        a_mat = _dot_nt((qa * jnp.exp(bcs - ref_d)).astype(BF16), (kk * jnp.exp(ref_d - bcs)).astype(BF16))
        a_mat = jnp.where(mask_diag, a_mat, 0.0)
        for (m, mk) in levels:
            dec = jnp.exp(-jnp.abs(bcs - _block_refs(bcs, m, True)))
            a_mat = a_mat + jnp.where(mk, _dot_nt((qa * dec).astype(BF16), (kk * dec).astype(BF16)), 0.0)
        o = o + _dot(a_mat.astype(BF16), vb)

        b_last = bcs[chunk - 1:chunk, :]
        kd = (kk * jnp.exp(b_last - bcs)).astype(BF16)
        st_new = jnp.exp(b_last) * st + _dot_tn(vb, kd)

        gate = gate_ref[:, vsl]
        return st_new, _rms(o, gn) * (gate * jax.nn.sigmoid(gate))

    for h in range(heads):
        st_ref[h], y_ref[:, h * dv:(h + 1) * dv] = one_head(h)

    @pl.when(n == pl.num_programs(1) - 1)
    def _():
        for h in range(heads):
            so_ref[h] = st_ref[h].T


def gated_linear_attention(proj, extra, gnorm, s0, *, hgrn2, chunk, valid=None):
    B, T, _ = proj.shape
    _, H, dk, dv = s0.shape
    chunk = min(chunk, T)
    assert T % chunk == 0
    base = min(LA_BASE, chunk)
    valid = chunk if valid is None else valid
    wk, wv = H * dk, H * dv

    def col(width, start):
        assert start % width == 0
        return pl.BlockSpec((None, chunk, width), lambda b, n: (b, n, start // width))

    if hgrn2:
        specs = [col(wk, 0), col(wk, wk), col(wv, 2 * wk), col(wv, 2 * wk + wv),
                 pl.BlockSpec((1, wk), lambda b, n: (0, 0))]
    else:
        specs = [col(wk, 0), col(wk, wk), col(wv, 2 * wk), col(wv, 2 * wk + wv),
                 pl.BlockSpec((None, chunk, wk), lambda b, n: (b, n, 0))]
    specs += [pl.BlockSpec((1, dv), lambda b, n: (0, 0)),
              pl.BlockSpec((None, H, dk, dv), lambda b, n: (b, 0, 0, 0))]
    return pl.pallas_call(
        functools.partial(_la_body, hgrn2=hgrn2, heads=H, dk=dk, dv=dv, chunk=chunk, base=base,
                          valid=valid, q_scale=1.0 if hgrn2 else dk ** -0.5),
        grid=(B, T // chunk),
        in_specs=specs,
        out_specs=[pl.BlockSpec((None, chunk, wv), lambda b, n: (b, n, 0)),
                   pl.BlockSpec((None, H, dk, dv), lambda b, n: (b, 0, 0, 0))],
        out_shape=[jax.ShapeDtypeStruct((B, T, wv), F32),
                   jax.ShapeDtypeStruct((B, H, dk, dv), F32)],
        scratch_shapes=[pltpu.VMEM((H, dv, dk), F32)],
        compiler_params=_cparams(2),
        name="gated_linear_attention",
    )(proj, proj, proj, proj, extra, gnorm, s0)


def _pad_rows(x, rows):
    return jnp.pad(x, ((0, 0), (0, rows - x.shape[1]), (0, 0)))


def _trunk(xp, xs, p_mem, s_mem, caches, b_states, c_states, P):
    Bp, T, D = xp.shape
    Bs = xs.shape[0]
    assert xs.shape[1] == 1 and Bs <= RIDER_ROWS
    Mp = Bp * T
    depth = P["norm_mix"].shape[0]
    hd_a, hpg = P["a_hd"], P["a_hpg"]
    groups = len(A_DILATIONS)
    Wa = hpg * hd_a
    tile, mlp_tile = TILES["in_proj"], TILES["mlp"]

    def ride(h):
        return jnp.pad(h, ((0, RIDER_ROWS - Bs), (0, 0)))

    hp, hs = xp.reshape(Mp, D), xs.reshape(Bs, D)
    qkv_prompt, new_rows, p_b, s_b, p_c, s_c = [], [], [], [], [], []
    for li in range(depth):
        kind, j = li % N_MIXERS, li // N_MIXERS
        gmix = (P["norm_mix"], li)
        if kind == 0:
            qkv_p, qkv_s = linear(hp, P["a_w_in"], j, gain=gmix, rider=ride(hs), **tile)
            qkv3 = qkv_p.reshape(Bp, T, 3 * groups * Wa)
            yp = dilated_attention_prompt(qkv3, P["a_bias_prompt"], heads=hpg, hd=hd_a,
                                          tile=TILES["attention_rows"])
            qkv_prompt.append(qkv3)
            qkv5 = qkv_s[:Bs].reshape(Bs, 3, groups, hpg, hd_a)
            ys = dilated_attention_sample(qkv5.reshape(Bs, 3 * groups, hpg, hd_a), caches, j,
                                          P["a_bias_sample"], heads=hpg, hd=hd_a, span=A_BLOCK)
            new_rows.append([qkv5[:, 1:3, g][:, None] for g in range(groups)])
            ys = ys.reshape(Bs, Wa)
            w_mix = P["a_w_out"]
        elif kind == 1:
            proj_p, proj_s = linear(hp, P["b_w_in"], j, gain=gmix, rider=ride(hs), **tile)
            lb = P["lower_bounds"][li][None, :]
            gn = P["b_gnorm"][j][None, :]
            yp, S = gated_linear_attention(proj_p.reshape(Bp, T, -1), lb, gn,
                                           jnp.zeros((Bp,) + b_states.shape[2:], F32), hgrn2=True, chunk=LA_CHUNK)
            p_b.append(S)
            ys, S = gated_linear_attention(_pad_rows(proj_s[:Bs, None], SUBLANES), lb, gn, b_states[j],
                                           hgrn2=True, chunk=SUBLANES, valid=1)
            s_b.append(S)
            w_mix = P["b_w_out"]
        else:
            wk = P["c_wk"]
            n_main = P["c_n_main"]
            proj_p, proj_s = linear(hp, P["c_w_in_t"], j, n_out=n_main, gain=gmix, rider=ride(hs),
                                    transposed=True, **tile)
            gate_args = (gmix, P["c_w_low"][j], P["c_w_gate"], P["c_b_gate"], j)
            gn = P["c_gnorm"][j][None, :]
            yp, S = gated_linear_attention(proj_p.reshape(Bp, T, n_main),
                                           gla_gate(hp, *gate_args, bm=TILES["gate_rows"]).reshape(Bp, T, wk), gn,
                                           jnp.zeros((Bp,) + c_states.shape[2:], F32), hgrn2=False, chunk=LA_CHUNK)
            p_c.append(S)
            ys, S = gated_linear_attention(_pad_rows(proj_s[:Bs, None], SUBLANES),
                                           _pad_rows(gla_gate(hs, *gate_args, bm=SUBLANES)[:, None], SUBLANES), gn,
                                           c_states[j], hgrn2=False, chunk=SUBLANES, valid=1)
            s_c.append(S)
            w_mix = P["c_w_out"]

        mem_kw = dict(heads=P["m_heads"], hd=P["m_hd"])
        hp = mix_out_mem_attend(yp.reshape(Bp, T, -1), w_mix, j, hp.reshape(Bp, T, D), P["norm_mem"], P["mem_w_q"],
                                p_mem, P["mem_w_out"], li, bm=TILES["mix_mem_rows"], **mem_kw).reshape(Mp, D)
        if ys.ndim == 3:
            ys = ys[:, 0]
        hs = mix_out_mem_attend_sample(ys, w_mix, j, hs, P["norm_mem"], P["mem_w_q"], s_mem, P["mem_w_out"], li,
                                       **mem_kw)
        final = P["norm_final"] if li == depth - 1 else None
        hp, hs = mlp(hp, P["norm_mlp"], P["mlp_w_up"], P["mlp_w_down"], li, final_gain=final,
                     rider=ride(hs), **mlp_tile)
        hs = hs[:Bs]

    new_rows = [jnp.stack([rows[g] for rows in new_rows], axis=0) for g in range(groups)]
    p_win = [to_window(qkv_prompt, g, rows=min(A_WINDOWS[g], T), groups=groups, heads=hpg, hd=hd_a)
             for g in range(groups)]
    return (hp.reshape(Bp, T, D), hs.reshape(Bs, 1, D), p_win, new_rows,
            jnp.stack(p_b, axis=0), jnp.stack(s_b, axis=0), jnp.stack(p_c, axis=0), jnp.stack(s_c, axis=0))


def _bias_tables(rel_bias_table, hpg, span):
    steps = np.arange(A_BLOCK)[:, None] + A_BLOCK - np.arange(2 * A_BLOCK)[None, :]
    eye = np.eye(N_BUCKETS, dtype=np.float32)
    prompt, sample = [], []
    for g, d in enumerate(A_DILATIONS):
        tg_t = rel_bias_table[:, g * hpg:(g + 1) * hpg].T
        onehot = eye[:, t5_bucket(np.clip(steps, 0, span) * d).reshape(-1)]
        bias = jnp.dot(tg_t, jnp.asarray(onehot), precision=lax.Precision.HIGHEST).reshape(hpg, A_BLOCK, 2 * A_BLOCK)
        band = jnp.asarray((steps >= 0) & (steps <= span))
        prompt.append(jnp.where(band[None], bias, -jnp.inf))
        dist = np.concatenate([(span - np.arange(span)) * d, np.zeros(1, np.int64)])
        row = jnp.dot(tg_t, jnp.asarray(eye[:, t5_bucket(dist)]), precision=lax.Precision.HIGHEST)
        sample.append(jnp.broadcast_to(row.T[:, :, None], (span + 1, hpg, LANES)))
    return jnp.stack(prompt, axis=0), jnp.stack(sample, axis=0)


def kernel(x_prompt, x_sample, cache_win0, cache_win1, cache_win2, state_hgrn, state_gla, cache_mem, mem_prompt, rel_bias_table, a_w_in, a_w_out, b_w_in, b_w_out, b_lower_bound, b_gnorm, c_w_in, c_w_gate, c_b_gate, c_w_out, c_gnorm, mem_norm, mem_w_kv, mem_w_q, mem_w_out, norm_mix, norm_mem, norm_mlp, mlp_w_up, mlp_w_down, norm_final):
    depth, D = norm_mix.shape
    hpg, hd_a = cache_win0.shape[-2], cache_win0.shape[-1]
    span = A_BLOCK
    assert all(w // d == span for w, d in zip(A_WINDOWS, A_DILATIONS))
    m_heads, m_hd = cache_mem.shape[-2], cache_mem.shape[-1]
    c_heads, c_dk, c_dv = state_gla.shape[2:]
    wk = c_heads * c_dk
    n_main = 2 * wk + 2 * c_heads * c_dv
    bias_prompt, bias_sample = _bias_tables(rel_bias_table, hpg, span)

    sm = jax.nn.softmax(b_lower_bound.astype(F32), axis=0)
    P = dict(
        a_hd=hd_a, a_hpg=hpg, m_heads=m_heads, m_hd=m_hd,
        a_bias_prompt=bias_prompt, a_bias_sample=bias_sample,
        a_w_in=a_w_in, a_w_out=a_w_out.astype(BF16), b_w_in=b_w_in, b_w_out=b_w_out.astype(BF16),
        lower_bounds=jnp.cumsum(sm, axis=0) - sm, b_gnorm=b_gnorm,
        c_w_in_t=jnp.swapaxes(c_w_in, 1, 2), c_w_low=c_w_in[:, :, n_main:], c_w_gate=c_w_gate,
        c_b_gate=c_b_gate[:, None, :], c_w_out=c_w_out.astype(BF16), c_gnorm=c_gnorm,
        c_wk=wk, c_n_main=n_main,
        mem_w_q=mem_w_q.astype(BF16), mem_w_out=mem_w_out.astype(BF16),
        norm_mix=norm_mix[:, None, :], norm_mem=norm_mem[:, None, :], norm_mlp=norm_mlp[:, None, :],
        mlp_w_up=mlp_w_up, mlp_w_down=mlp_w_down, norm_final=norm_final[None, :],
    )

    Bm, Mt, _ = mem_prompt.shape
    mem_flat = mem_prompt.reshape(Bm * Mt, D)
    mem_gain = mem_norm[:, None, :]
    p_mem = jnp.stack([linear(mem_flat, mem_w_kv, li, gain=(mem_gain, li), **TILES["mem_kv"])
                       for li in range(depth)], axis=0).reshape(depth, Bm, Mt, 2 * m_heads * m_hd)

    caches = (cache_win0, cache_win1, cache_win2)
    y_prompt, y_sample, p_win, new_rows, p_hgrn, s_hgrn, p_gla, s_gla = _trunk(
        x_prompt, x_sample, p_mem, cache_mem, caches, state_hgrn, state_gla, P)
    s_win = [window_update(c, n, chunk=TILES["window_copy_rows"]) for c, n in zip(caches, new_rows)]
    return (y_prompt, y_sample, p_win[0], p_win[1], p_win[2], p_hgrn, p_gla,
            p_mem.reshape(depth, Bm, Mt, 2, m_heads, m_hd),
            s_win[0], s_win[1], s_win[2], s_hgrn, s_gla)
```

```python
import functools
import math

import numpy as np
import jax
import jax.numpy as jnp
from jax import lax
from jax.experimental import pallas as pl
from jax.experimental.pallas import tpu as pltpu

F32 = jnp.float32
BF16 = jnp.bfloat16

EPS = 1e-6
N_MIXERS = 3
A_WINDOWS = (128, 512, 2048)
A_DILATIONS = (1, 4, 16)
A_BLOCK = 128
N_BUCKETS = 32
MAX_DISTANCE = 2048
GLA_GATE_NORMALIZER = 16.0
LA_BASE = 16
LA_CHUNK = 128
RIDER_ROWS = 16

TILES = dict(
    in_proj=dict(bm=2048, bn=512),
    mlp=dict(bm=1024, bf=512),
    mix_mem_rows=512,
    gate_rows=1024,
    attention_rows=16 * A_BLOCK,
    mem_kv=dict(bm=512, bn=512),
    window_copy_rows=512,
    window_relayout_rows=256,
)

V7X_VMEM_BYTES = 64 * 1024 * 1024
VMEM_LIMIT = V7X_VMEM_BYTES - 8 * 1024 * 1024
SUBLANES = 8
LANES = 128
V7X_MXU_WIDTH = 256


def _cparams(n_axes):
    return pltpu.CompilerParams(dimension_semantics=("arbitrary",) * n_axes,
                                vmem_limit_bytes=VMEM_LIMIT)


def _tile(n, target, align):
    if n <= target:
        return n
    t = (target // align) * align
    while n % t:
        t -= align
    return t


def _rms(x, g):
    r = lax.rsqrt(jnp.mean(x * x, axis=-1, keepdims=True) + EPS)
    return (x * r) * g


def _dot(a, b):
    return jnp.dot(a, b, preferred_element_type=F32)


def _dot_nt(a, b):
    return lax.dot_general(a, b, (((1,), (1,)), ((), ())), preferred_element_type=F32)


def _dot_tn(a, b):
    return lax.dot_general(a, b, (((0,), (0,)), ((), ())), preferred_element_type=F32)


def _bf_round(a):
    return a.astype(BF16).astype(F32)


def _linear_body(*refs, riders, transposed):
    if riders:
        x_ref, w_ref, g_ref, r_ref, o_ref, or_ref, xs_ref = refs
    else:
        x_ref, w_ref, g_ref, o_ref, xs_ref = refs
    bm = x_ref.shape[0]

    @pl.when(pl.program_id(1) == 0)
    def _():
        xs_ref[0:bm] = _rms(x_ref[...], g_ref[...]).astype(BF16)
        if riders:
            xs_ref[bm:bm + riders] = _rms(r_ref[...], g_ref[...]).astype(BF16)

    w = w_ref[...].astype(BF16)
    acc = _dot_nt(xs_ref[...], w) if transposed else _dot(xs_ref[...], w)
    if riders:
        o_ref[...] = acc[:bm]
        or_ref[...] = acc[bm:]
    else:
        o_ref[...] = acc


def linear(x, w, layer, *, n_out=None, gain, rider=None, transposed=False, bm, bn):
    M, K = x.shape
    N = (w.shape[1] if transposed else w.shape[2]) if n_out is None else n_out
    bm, bn = _tile(M, bm, SUBLANES), _tile(N, bn, LANES)
    riders = 0 if rider is None else rider.shape[0]
    g_arr, g_idx = gain
    w_spec = (pl.BlockSpec((None, bn, K), lambda i, j: (layer, j, 0)) if transposed
              else pl.BlockSpec((None, K, bn), lambda i, j: (layer, 0, j)))
    in_specs = [pl.BlockSpec((bm, K), lambda i, j: (i, 0), pipeline_mode=pl.Buffered(1)),
                w_spec,
                pl.BlockSpec((None, 1, K), lambda i, j: (g_idx, 0, 0))]
    args = [x, w, g_arr]
    out_specs = [pl.BlockSpec((bm, bn), lambda i, j: (i, j))]
    out_shape = [jax.ShapeDtypeStruct((M, N), F32)]
    if riders:
        in_specs.append(pl.BlockSpec((riders, K), lambda i, j: (0, 0)))
        args.append(rider)
        out_specs.append(pl.BlockSpec((None, riders, bn), lambda i, j: (i, 0, j)))
        out_shape.append(jax.ShapeDtypeStruct((M // bm, riders, N), F32))
    outs = pl.pallas_call(
        functools.partial(_linear_body, riders=riders, transposed=transposed),
        grid=(M // bm, N // bn),
        in_specs=in_specs,
        out_specs=out_specs,
        out_shape=out_shape,
        scratch_shapes=[pltpu.VMEM((bm + riders, K), BF16)],
        compiler_params=_cparams(2),
        name="linear",
    )(*args)
    return (outs[0], outs[1][0]) if riders else outs[0]


def _mlp_body(*refs, has_final, riders):
    refs = list(refs)
    x_ref, g_ref, wu_ref, wd_ref = refs[:4]
    pos = 4
    fg_ref = r_ref = or_ref = None
    if has_final:
        fg_ref = refs[pos]
        pos += 1
    if riders:
        r_ref = refs[pos]
        pos += 1
    o_ref = refs[pos]
    if riders:
        or_ref = refs[pos + 1]
    xs_ref = refs[-1]
    bm = x_ref.shape[0]
    f = pl.program_id(1)

    @pl.when(f == 0)
    def _():
        x = x_ref[...]
        xs_ref[0:bm] = _rms(x, g_ref[...]).astype(BF16)
        o_ref[...] = x
        if riders:
            r = r_ref[...]
            xs_ref[bm:bm + riders] = _rms(r, g_ref[...]).astype(BF16)
            or_ref[...] = r

    def up_down(xs):
        hdn = _dot(xs, wu_ref[...].astype(BF16))
        hdn = jnp.square(jnp.maximum(hdn, 0.0)).astype(BF16)
        return _dot(hdn, wd_ref[...].astype(BF16))

    if riders:
        @pl.when(pl.program_id(0) == 0)
        def _():
            down = up_down(xs_ref[...])
            o_ref[...] += down[:bm]
            or_ref[...] += down[bm:]

        @pl.when(pl.program_id(0) != 0)
        def _():
            o_ref[...] += up_down(xs_ref[0:bm])
    else:
        o_ref[...] += up_down(xs_ref[...])

    if has_final:
        @pl.when(f == pl.num_programs(1) - 1)
        def _():
            o_ref[...] = _rms(o_ref[...], fg_ref[...])
            if riders:
                or_ref[...] = _rms(or_ref[...], fg_ref[...])


def mlp(x, gains, w_up, w_down, layer, *, final_gain=None, rider=None, bm, bf):
    M, D = x.shape
    FF = w_up.shape[-1]
    bm, bf = _tile(M, bm, SUBLANES), _tile(FF, bf, LANES)
    riders = 0 if rider is None else rider.shape[0]
    in_specs = [pl.BlockSpec((bm, D), lambda i, f: (i, 0), pipeline_mode=pl.Buffered(1)),
                pl.BlockSpec((None, 1, D), lambda i, f: (layer, 0, 0)),
                pl.BlockSpec((None, D, bf), lambda i, f: (layer, 0, f)),
                pl.BlockSpec((None, bf, D), lambda i, f: (layer, f, 0))]
    args = [x, gains, w_up, w_down]
    if final_gain is not None:
        in_specs.append(pl.BlockSpec((1, D), lambda i, f: (0, 0)))
        args.append(final_gain)
    out_specs = [pl.BlockSpec((bm, D), lambda i, f: (i, 0), pipeline_mode=pl.Buffered(1))]
    out_shape = [jax.ShapeDtypeStruct((M, D), F32)]
    if riders:
        in_specs.append(pl.BlockSpec((riders, D), lambda i, f: (0, 0)))
        args.append(rider)
        out_specs.append(pl.BlockSpec((None, riders, D), lambda i, f: (i, 0, 0)))
        out_shape.append(jax.ShapeDtypeStruct((M // bm, riders, D), F32))
    outs = pl.pallas_call(
        functools.partial(_mlp_body, has_final=final_gain is not None, riders=riders),
        grid=(M // bm, FF // bf),
        in_specs=in_specs,
        out_specs=out_specs,
        out_shape=out_shape,
        scratch_shapes=[pltpu.VMEM((bm + riders, D), BF16)],
        compiler_params=_cparams(2),
        name="mlp",
    )(*args)
    return (outs[0], outs[1][0]) if riders else outs[0]


def _mem_body(y_ref, wm_ref, h_ref, g_ref, wq_ref, kv_ref, wo_ref, o_ref, *, heads, hd):
    x = h_ref[...] + _dot(y_ref[...].astype(BF16), wm_ref[...])
    xn = _rms(x, g_ref[...]).astype(BF16)
    q = _dot(xn, wq_ref[...])
    width = heads * hd
    scale = hd ** -0.5
    outs = []
    for h in range(heads):
        qh = q[:, h * hd:(h + 1) * hd].astype(BF16)
        kh = kv_ref[:, h * hd:(h + 1) * hd].astype(BF16)
        vh = kv_ref[:, width + h * hd:width + (h + 1) * hd].astype(BF16)
        s = _dot_nt(qh, kh) * scale
        p = jnp.exp(s - jnp.max(s, axis=-1, keepdims=True))
        p = p / jnp.sum(p, axis=-1, keepdims=True)
        outs.append(_dot(p.astype(BF16), vh))
    o = jnp.concatenate(outs, axis=-1).astype(BF16)
    o_ref[...] = x + _dot(o, wo_ref[...])


def mix_out_mem_attend(y, w_mix, mix_layer, h, gains, w_q, kv, w_o, layer, *, heads, hd, bm):
    B, T, D = h.shape
    Ky = y.shape[-1]
    Mt = kv.shape[2]
    width = heads * hd
    bm = _tile(T, bm, SUBLANES)
    return pl.pallas_call(
        functools.partial(_mem_body, heads=heads, hd=hd),
        grid=(B, T // bm),
        in_specs=[pl.BlockSpec((None, bm, Ky), lambda b, i: (b, i, 0)),
                  pl.BlockSpec((None, Ky, D), lambda b, i: (mix_layer, 0, 0), pipeline_mode=pl.Buffered(1)),
                  pl.BlockSpec((None, bm, D), lambda b, i: (b, i, 0)),
                  pl.BlockSpec((None, 1, D), lambda b, i: (layer, 0, 0)),
                  pl.BlockSpec((None, D, width), lambda b, i: (layer, 0, 0), pipeline_mode=pl.Buffered(1)),
                  pl.BlockSpec((None, None, Mt, 2 * width), lambda b, i: (layer, b, 0, 0)),
                  pl.BlockSpec((None, width, D), lambda b, i: (layer, 0, 0), pipeline_mode=pl.Buffered(1))],
        out_specs=pl.BlockSpec((None, bm, D), lambda b, i: (b, i, 0)),
        out_shape=jax.ShapeDtypeStruct((B, T, D), F32),
        compiler_params=_cparams(2),
        name="mix_out_mem_attend",
    )(y, w_mix, h, gains, w_q, kv, w_o)


def _mem_sample_body(y_ref, wm_ref, h_ref, g_ref, wq_ref, kv_ref, wo_ref, o_ref, *, heads, hd):
    batch = h_ref.shape[0]
    scale = hd ** -0.5
    x = h_ref[...] + _dot(y_ref[...].astype(BF16), wm_ref[...])
    q = _bf_round(_dot(_rms(x, g_ref[...]).astype(BF16), wq_ref[...]))
    outs = []
    for b in range(batch):
        qb = jnp.concatenate([q[b:b + 1, h * hd:(h + 1) * hd] for h in range(heads)], axis=0)
        k = _bf_round(kv_ref[b, :, 0])
        v = _bf_round(kv_ref[b, :, 1])
        s = jnp.sum(k * qb[None], axis=-1, keepdims=True) * scale
        p = jnp.exp(s - jnp.max(s, axis=0))
        p = _bf_round(p / jnp.sum(p, axis=0))
        ob = jnp.sum(p * v, axis=0)
        outs.append(jnp.concatenate([ob[h:h + 1] for h in range(heads)], axis=1))
    o = jnp.concatenate(outs, axis=0).astype(BF16)
    o_ref[...] = x + _dot(o, wo_ref[...])


def mix_out_mem_attend_sample(y, w_mix, mix_layer, h, gains, w_q, kv, w_o, layer, *, heads, hd):
    B, D = h.shape
    Ky = y.shape[-1]
    Mt = kv.shape[2]
    width = heads * hd
    one = pl.Buffered(1)
    return pl.pallas_call(
        functools.partial(_mem_sample_body, heads=heads, hd=hd),
        grid=(1,),
        in_specs=[pl.BlockSpec((B, Ky), lambda i: (0, 0)),
                  pl.BlockSpec((None, Ky, D), lambda i: (mix_layer, 0, 0), pipeline_mode=one),
                  pl.BlockSpec((B, D), lambda i: (0, 0)),
                  pl.BlockSpec((None, 1, D), lambda i: (layer, 0, 0)),
                  pl.BlockSpec((None, D, width), lambda i: (layer, 0, 0), pipeline_mode=one),
                  pl.BlockSpec((None, B, Mt, 2, heads, hd), lambda i: (layer, 0, 0, 0, 0, 0), pipeline_mode=one),
                  pl.BlockSpec((None, width, D), lambda i: (layer, 0, 0), pipeline_mode=one)],
        out_specs=pl.BlockSpec((B, D), lambda i: (0, 0)),
        out_shape=jax.ShapeDtypeStruct((B, D), F32),
        compiler_params=_cparams(1),
        name="mix_out_mem_attend_sample",
    )(y, w_mix, h, gains, w_q, kv, w_o)


def t5_bucket(dist):
    dist = np.asarray(dist, np.int64)
    max_exact = N_BUCKETS // 2
    d = np.maximum(dist, 1).astype(np.float64)
    large = max_exact + (np.log(d / max_exact) / math.log(MAX_DISTANCE / max_exact)
                         * (N_BUCKETS - max_exact)).astype(np.int64)
    return np.where(dist < max_exact, dist, np.minimum(large, N_BUCKETS - 1)).astype(np.int32)


def _attn_block(q, k2, v2, bias2, scale):
    blk = q.shape[0]
    st = _dot_nt(k2.astype(BF16), q.astype(BF16)) * scale + bias2
    m = jnp.max(st, axis=0, keepdims=True)
    p = jnp.exp(st - m)
    l = jnp.sum(p, axis=0, keepdims=True)
    o = _dot_tn((p * (1.0 / l)).astype(BF16), v2.astype(BF16))
    return o, jnp.broadcast_to(m + jnp.log(l), (blk, blk)).T


def _dil_attn_body(*refs, dils, tile, hd):
    n_groups = len(dils)
    bias_ref, o_ref, og_ref, lse_ref = refs[5 * n_groups:]
    not_first_tile = pl.program_id(1) > 0
    blk = A_BLOCK
    scale = hd ** -0.5

    for g, d in enumerate(dils):
        q_ref, kc_ref, vc_ref, kp_ref, vp_ref = refs[5 * g:5 * g + 5]
        stream_rows = blk * d
        n_blocks = tile // stream_rows

        def rows(start):
            return pl.ds(start, blk) if d == 1 else pl.ds(start, blk, stride=d)

        bias2 = bias_ref[g]
        bias2_first = jnp.concatenate(
            [jnp.where(not_first_tile, bias2[:blk], -jnp.inf), bias2[blk:]], axis=0)

        for m in range(n_blocks):
            for r in range(d):
                s = r + m * stream_rows
                if m == 0:
                    kp, vp, bias = kp_ref[rows(s), :], vp_ref[rows(s), :], bias2_first
                else:
                    kp, vp, bias = kc_ref[rows(s - stream_rows), :], vc_ref[rows(s - stream_rows), :], bias2
                k2 = jnp.concatenate([kp, kc_ref[rows(s), :]], axis=0)
                v2 = jnp.concatenate([vp, vc_ref[rows(s), :]], axis=0)
                o, lse = _attn_block(q_ref[rows(s), :], k2, v2, bias, scale)
                og_ref[g, rows(s), :] = o
                lse_ref[g, rows(s), :] = lse

    lses = [lse_ref[g] for g in range(n_groups)]
    mx = functools.reduce(jnp.maximum, lses)
    es = [jnp.exp(l - mx) for l in lses]
    den = functools.reduce(lambda a, b: a + b, es)
    acc = es[0] * og_ref[0]
    for g in range(1, n_groups):
        acc = acc + es[g] * og_ref[g]
    o_ref[...] = acc / den


def dilated_attention_prompt(qkv, bias, *, heads, hd, tile):
    B, T, _ = qkv.shape
    dils = A_DILATIONS
    n_groups = len(dils)
    tile = min(tile, T)
    assert hd == LANES and T % tile == 0 and all(tile % (A_BLOCK * d) == 0 for d in dils)
    sec = n_groups * heads

    def cur(which, g):
        return pl.BlockSpec((None, tile, hd), lambda b, n, h: (b, n, which * sec + g * heads + h))

    def prev(which, g):
        rows = A_BLOCK * dils[g]
        per = tile // rows
        return pl.BlockSpec((None, rows, hd),
                            lambda b, n, h: (b, jnp.maximum(n * per - 1, 0), which * sec + g * heads + h))

    in_specs = []
    for g in range(n_groups):
        in_specs += [cur(0, g), cur(1, g), cur(2, g), prev(1, g), prev(2, g)]
    in_specs.append(pl.BlockSpec((n_groups, None, 2 * A_BLOCK, A_BLOCK), lambda b, n, h: (0, h, 0, 0)))
    return pl.pallas_call(
        functools.partial(_dil_attn_body, dils=dils, tile=tile, hd=hd),
        grid=(B, T // tile, heads),
        in_specs=in_specs,
        out_specs=pl.BlockSpec((None, tile, hd), lambda b, n, h: (b, n, h)),
        out_shape=jax.ShapeDtypeStruct((B, T, heads * hd), F32),
        scratch_shapes=[pltpu.VMEM((n_groups, tile, hd), F32), pltpu.VMEM((n_groups, tile, hd), F32)],
        compiler_params=_cparams(3),
        name="dilated_attention_prompt",
    )(*([qkv] * (5 * n_groups)), bias)


def _dec_attn_body(qkv_ref, *refs, groups, span, scale):
    bufs, bias_ref, o_ref = refs[:groups], refs[groups], refs[groups + 1]
    outs, lses = [], []
    for g in range(groups):
        q = _bf_round(qkv_ref[g])
        kn = _bf_round(qkv_ref[groups + g])
        vn = _bf_round(qkv_ref[2 * groups + g])
        k = _bf_round(bufs[g][:, 0])
        v = _bf_round(bufs[g][:, 1])
        s = jnp.sum(k * q[None], axis=-1, keepdims=True) * scale + bias_ref[g, :span]
        s_self = jnp.sum(q * kn, axis=-1, keepdims=True) * scale + bias_ref[g, span]
        m = jnp.maximum(jnp.max(s, axis=0), s_self)
        p = jnp.exp(s - m[None])
        p_self = jnp.exp(s_self - m)
        l = jnp.sum(p, axis=0) + p_self
        outs.append(jnp.sum(_bf_round(p / l[None]) * v, axis=0) + _bf_round(p_self / l) * vn)
        lses.append(m + jnp.log(l))
    mx = functools.reduce(jnp.maximum, lses)
    es = [jnp.exp(l - mx) for l in lses]
    den = functools.reduce(lambda a, b: a + b, es)
    acc = (es[0] / den) * outs[0]
    for g in range(1, groups):
        acc = acc + (es[g] / den) * outs[g]
    o_ref[...] = acc


def dilated_attention_sample(qkv, caches, layer, bias, *, heads, hd, span):
    B = qkv.shape[0]
    groups = len(caches)
    in_specs = [pl.BlockSpec((None, 3 * groups, heads, hd), lambda b: (b, 0, 0, 0))]
    views = []
    for g, c in enumerate(caches):
        d = A_DILATIONS[g]
        assert c.shape[2] == span * d
        views.append(c.reshape(c.shape[0], B, span, d, 2, heads, hd))
        in_specs.append(pl.BlockSpec((None, None, span, None, 2, heads, hd),
                                     lambda b: (layer, b, 0, 0, 0, 0, 0)))
    in_specs.append(pl.BlockSpec((groups, span + 1, heads, hd), lambda b: (0, 0, 0, 0)))
    return pl.pallas_call(
        functools.partial(_dec_attn_body, groups=groups, span=span, scale=hd ** -0.5),
        grid=(B,),
        in_specs=in_specs,
        out_specs=pl.BlockSpec((None, heads, hd), lambda b: (b, 0, 0)),
        out_shape=jax.ShapeDtypeStruct((B, heads, hd), F32),
        compiler_params=_cparams(1),
        name="dilated_attention_sample",
    )(qkv, *views, bias)


def _to_window_body(*refs, heads, hd):
    o_ref = refs[-1]
    chunk = refs[0].shape[0]
    for layer in range(o_ref.shape[0]):
        for s in range(2):
            src = refs[2 * layer + s]
            for h in range(heads):
                o_ref[layer, pl.ds(s * heads + h, chunk, stride=2 * heads), :] = src[:, h * hd:(h + 1) * hd]


def to_window(qkvs, g, *, rows, groups, heads, hd):
    B, T, _ = qkvs[0].shape
    W = heads * hd
    chunk = _tile(rows, TILES["window_relayout_rows"], SUBLANES)
    assert (T - rows) % chunk == 0
    first = (T - rows) // chunk
    in_specs, args = [], []
    for qkv in qkvs:
        in_specs += [pl.BlockSpec((None, chunk, W), lambda b, c: (b, first + c, groups + g)),
                     pl.BlockSpec((None, chunk, W), lambda b, c: (b, first + c, 2 * groups + g))]
        args += [qkv, qkv]
    per_token = 2 * heads
    out = pl.pallas_call(
        functools.partial(_to_window_body, heads=heads, hd=hd),
        grid=(B, rows // chunk),
        in_specs=in_specs,
        out_specs=pl.BlockSpec((len(qkvs), None, chunk * per_token, hd), lambda b, c: (0, b, c, 0)),
        out_shape=jax.ShapeDtypeStruct((len(qkvs), B, rows * per_token, hd), F32),
        compiler_params=_cparams(2),
        name="to_window",
    )(*args)
    return out.reshape(len(qkvs), B, rows, 2, heads, hd)


def _window_update_body(main_ref, next_ref, new_ref, o_ref):
    rows = main_ref.shape[0]
    last_chunk = pl.program_id(2) == pl.num_programs(2) - 1
    o_ref[0:rows - 1] = main_ref[1:rows]
    o_ref[rows - 1:rows] = jnp.where(last_chunk, new_ref[...], next_ref[...])


def window_update(cache, new, *, chunk):
    n_layers, B, rows = cache.shape[:3]
    tail = cache.shape[3:]
    chunk = _tile(rows, chunk, 1)
    zeros = (0,) * len(tail)
    return pl.pallas_call(
        _window_update_body,
        grid=(n_layers, B, rows // chunk),
        in_specs=[pl.BlockSpec((None, None, chunk) + tail, lambda j, b, c: (j, b, c) + zeros),
                  pl.BlockSpec((None, None, 1) + tail,
                               lambda j, b, c: (j, b, jnp.minimum((c + 1) * chunk, rows - 1)) + zeros),
                  pl.BlockSpec((None, None, 1) + tail, lambda j, b, c: (j, b, 0) + zeros)],
        out_specs=pl.BlockSpec((None, None, chunk) + tail, lambda j, b, c: (j, b, c) + zeros),
        out_shape=jax.ShapeDtypeStruct(cache.shape, cache.dtype),
        compiler_params=_cparams(3),
        name="window_update",
    )(cache, cache, new)


def _gla_gate_body(x_ref, g_ref, wl_ref, wg_ref, b_ref, o_ref):
    xn = _rms(x_ref[...], g_ref[...]).astype(BF16)
    low = _dot(xn, wl_ref[...].astype(BF16))
    z = _dot(low.astype(BF16), wg_ref[...].astype(BF16)) + b_ref[...]
    log_sig = jnp.minimum(z, 0.0) - jnp.log(1.0 + jnp.exp(-jnp.abs(z)))
    o_ref[...] = log_sig / GLA_GATE_NORMALIZER


def gla_gate(x, gain, w_low, w_gate, b_gate, layer, *, bm):
    M, D = x.shape
    g_arr, g_idx = gain
    rank, dk = w_gate.shape[1], w_gate.shape[2]
    bm = _tile(M, bm, SUBLANES)
    return pl.pallas_call(
        _gla_gate_body,
        grid=(M // bm,),
        in_specs=[pl.BlockSpec((bm, D), lambda i: (i, 0)),
                  pl.BlockSpec((None, 1, D), lambda i: (g_idx, 0, 0)),
                  pl.BlockSpec((D, rank), lambda i: (0, 0)),
                  pl.BlockSpec((None, rank, dk), lambda i: (layer, 0, 0)),
                  pl.BlockSpec((None, 1, dk), lambda i: (layer, 0, 0))],
        out_specs=pl.BlockSpec((bm, dk), lambda i: (i, 0)),
        out_shape=jax.ShapeDtypeStruct((M, dk), F32),
        compiler_params=_cparams(1),
        name="gla_gate",
    )(x, g_arr, w_low, w_gate, b_gate)


def _cumsum_rows(x):
    rows = x.shape[0]
    row = lax.broadcasted_iota(jnp.int32, x.shape, 0)
    s = 1
    while s < rows:
        x = x + jnp.where(row >= s, pltpu.roll(x, s, axis=0), 0.0)
        s *= 2
    return x


def _block_refs(bcs, m, pairs):
    rows, dk = bcs.shape
    nb = rows // m
    ends = [bcs[(k + 1) * m - 1:(k + 1) * m, :] for k in range(nb)]
    parts = []
    for k in range(nb):
        if pairs:
            r = ends[k - 1] if k % 2 == 1 else ends[k]
        else:
            r = ends[k - 1] if k > 0 else jnp.zeros((1, dk), F32)
        parts.append(jnp.broadcast_to(r, (m, dk)))
    return parts[0] if nb == 1 else jnp.concatenate(parts, axis=0)


def _la_body(a_ref, b_ref, c_ref, gate_ref, e_ref, gn_ref, s0_ref, y_ref, so_ref, st_ref,
             *, hgrn2, heads, dk, dv, chunk, base, valid, q_scale, gsz):
    n = pl.program_id(1)
    groups = heads // gsz

    def own_block(shape, row_width, col_width):
        r = lax.broadcasted_iota(jnp.int32, shape, 0) // row_width
        c = lax.broadcasted_iota(jnp.int32, shape, 1) // col_width
        return r == c

    def block_rows(x, width):
        if gsz == 1:
            return x
        lane_head = lax.broadcasted_iota(jnp.int32, x.shape, 1) // width
        return jnp.concatenate([jnp.where(lane_head == i, x, jnp.zeros_like(x)) for i in range(gsz)], axis=0)

    @pl.when(n == 0)
    def _():
        for gi in range(groups):
            rows = []
            for i in range(gsz):
                blocks = [s0_ref[gi * gsz + i].T if j == i else jnp.zeros((dv, dk), F32) for j in range(gsz)]
                rows.append(blocks[0] if gsz == 1 else jnp.concatenate(blocks, axis=1))
            st_ref[gi] = rows[0] if gsz == 1 else jnp.concatenate(rows, axis=0)

    tq = lax.broadcasted_iota(jnp.int32, (chunk, chunk), 0)
    ts = lax.broadcasted_iota(jnp.int32, (chunk, chunk), 1)
    row = lax.broadcasted_iota(jnp.int32, (chunk, 1), 0)
    sh = int(math.log2(base))
    mask_diag = jnp.logical_and((tq >> sh) == (ts >> sh), ts <= tq)
    levels = []
    m = base
    while m < chunk:
        s1 = int(math.log2(m))
        mk = jnp.logical_and((tq >> (s1 + 1)) == (ts >> (s1 + 1)),
                             jnp.logical_and(((tq >> s1) & 1) == 1, ((ts >> s1) & 1) == 0))
        levels.append((m, mk))
        m *= 2
    if gsz > 1:
        mask_diag = jnp.concatenate([mask_diag] * gsz, axis=1)
        levels = [(m, jnp.concatenate([mk] * gsz, axis=1)) for m, mk in levels]
        state_own = own_block((gsz * dv, gsz * dk), dv, dk)
    gn = gn_ref[...]

    def one_group(gi):
        ksl = slice(gi * gsz * dk, (gi + 1) * gsz * dk)
        vsl = slice(gi * gsz * dv, (gi + 1) * gsz * dv)
        if hgrn2:
            a = a_ref[:, ksl]
            lb = e_ref[:, ksl]
            f = lb + (1.0 - lb) * jax.nn.sigmoid(b_ref[:, ksl])
            qa = a * jax.nn.sigmoid(a)
            kk = 1.0 - f
            g = jnp.log(f)
        else:
            qa = a_ref[:, ksl] * q_scale
            kk = b_ref[:, ksl]
            g = e_ref[:, ksl]
        v = c_ref[:, vsl]
        if valid < chunk:
            kk = jnp.where(row < valid, kk, 0.0)
            g = jnp.where(row < valid, g, 0.0)
        bcs = _cumsum_rows(g)
        vb = v.astype(BF16)
        st = st_ref[gi]

        o = _dot_nt((qa * jnp.exp(bcs)).astype(BF16), st.astype(BF16))

        ref_d = _block_refs(bcs, base, False)
        a_mat = _dot_nt((qa * jnp.exp(bcs - ref_d)).astype(BF16),
                        block_rows((kk * jnp.exp(ref_d - bcs)).astype(BF16), dk))
        a_mat = jnp.where(mask_diag, a_mat, 0.0)
        for (m, mk) in levels:
            dec = jnp.exp(-jnp.abs(bcs - _block_refs(bcs, m, True)))
            a_mat = a_mat + jnp.where(mk, _dot_nt((qa * dec).astype(BF16), block_rows((kk * dec).astype(BF16), dk)),
                                      0.0)
        o = o + _dot(a_mat.astype(BF16), block_rows(vb, dv))

        b_last = bcs[chunk - 1:chunk, :]
        kd = (kk * jnp.exp(b_last - bcs)).astype(BF16)
        update = _dot_tn(vb, kd)
        if gsz > 1:
            update = jnp.where(state_own, update, 0.0)
        st_new = jnp.exp(b_last) * st + update

        gate = gate_ref[:, vsl]
        normed = [_rms(o[:, i * dv:(i + 1) * dv], gn) for i in range(gsz)]
        normed = normed[0] if gsz == 1 else jnp.concatenate(normed, axis=1)
        return st_new, normed * (gate * jax.nn.sigmoid(gate))

    for gi in range(groups):
        st_ref[gi], y_ref[:, gi * gsz * dv:(gi + 1) * gsz * dv] = one_group(gi)

    @pl.when(n == pl.num_programs(1) - 1)
    def _():
        for h in range(heads):
            gi, i = divmod(h, gsz)
            so_ref[h] = st_ref[gi][i * dv:(i + 1) * dv, i * dk:(i + 1) * dk].T


def gated_linear_attention(proj, extra, gnorm, s0, *, hgrn2, chunk, valid=None):
    B, T, _ = proj.shape
    _, H, dk, dv = s0.shape
    chunk = min(chunk, T)
    assert T % chunk == 0
    base = min(LA_BASE, chunk)
    valid = chunk if valid is None else valid
    wk, wv = H * dk, H * dv
    gsz = max(1, V7X_MXU_WIDTH // max(dk, dv))
    if H % gsz or chunk < LANES:
        gsz = 1

    def col(width, start):
        assert start % width == 0
        return pl.BlockSpec((None, chunk, width), lambda b, n: (b, n, start // width))

    if hgrn2:
        specs = [col(wk, 0), col(wk, wk), col(wv, 2 * wk), col(wv, 2 * wk + wv),
                 pl.BlockSpec((1, wk), lambda b, n: (0, 0))]
    else:
        specs = [col(wk, 0), col(wk, wk), col(wv, 2 * wk), col(wv, 2 * wk + wv),
                 pl.BlockSpec((None, chunk, wk), lambda b, n: (b, n, 0))]
    specs += [pl.BlockSpec((1, dv), lambda b, n: (0, 0)),
              pl.BlockSpec((None, H, dk, dv), lambda b, n: (b, 0, 0, 0))]
    return pl.pallas_call(
        functools.partial(_la_body, hgrn2=hgrn2, heads=H, dk=dk, dv=dv, chunk=chunk, base=base,
                          valid=valid, q_scale=1.0 if hgrn2 else dk ** -0.5, gsz=gsz),
        grid=(B, T // chunk),
        in_specs=specs,
        out_specs=[pl.BlockSpec((None, chunk, wv), lambda b, n: (b, n, 0)),
                   pl.BlockSpec((None, H, dk, dv), lambda b, n: (b, 0, 0, 0))],
        out_shape=[jax.ShapeDtypeStruct((B, T, wv), F32),
                   jax.ShapeDtypeStruct((B, H, dk, dv), F32)],
        scratch_shapes=[pltpu.VMEM((H // gsz, gsz * dv, gsz * dk), F32)],
        compiler_params=_cparams(2),
        name="gated_linear_attention",
    )(proj, proj, proj, proj, extra, gnorm, s0)


def _pad_rows(x, rows):
    return jnp.pad(x, ((0, 0), (0, rows - x.shape[1]), (0, 0)))


def _trunk(xp, xs, p_mem, s_mem, caches, b_states, c_states, P):
    Bp, T, D = xp.shape
    Bs = xs.shape[0]
    assert xs.shape[1] == 1 and Bs <= RIDER_ROWS
    Mp = Bp * T
    depth = P["norm_mix"].shape[0]
    hd_a, hpg = P["a_hd"], P["a_hpg"]
    groups = len(A_DILATIONS)
    Wa = hpg * hd_a
    tile, mlp_tile = TILES["in_proj"], TILES["mlp"]

    def ride(h):
        return jnp.pad(h, ((0, RIDER_ROWS - Bs), (0, 0)))

    hp, hs = xp.reshape(Mp, D), xs.reshape(Bs, D)
    qkv_prompt, new_rows, p_b, s_b, p_c, s_c = [], [], [], [], [], []
    for li in range(depth):
        kind, j = li % N_MIXERS, li // N_MIXERS
        gmix = (P["norm_mix"], li)
        if kind == 0:
            qkv_p, qkv_s = linear(hp, P["a_w_in"], j, gain=gmix, rider=ride(hs), **tile)
            qkv3 = qkv_p.reshape(Bp, T, 3 * groups * Wa)
            yp = dilated_attention_prompt(qkv3, P["a_bias_prompt"], heads=hpg, hd=hd_a,
                                          tile=TILES["attention_rows"])
            qkv_prompt.append(qkv3)
            qkv5 = qkv_s[:Bs].reshape(Bs, 3, groups, hpg, hd_a)
            ys = dilated_attention_sample(qkv5.reshape(Bs, 3 * groups, hpg, hd_a), caches, j,
                                          P["a_bias_sample"], heads=hpg, hd=hd_a, span=A_BLOCK)
            new_rows.append([qkv5[:, 1:3, g][:, None] for g in range(groups)])
            ys = ys.reshape(Bs, Wa)
            w_mix = P["a_w_out"]
        elif kind == 1:
            proj_p, proj_s = linear(hp, P["b_w_in"], j, gain=gmix, rider=ride(hs), **tile)
            lb = P["lower_bounds"][li][None, :]
            gn = P["b_gnorm"][j][None, :]
            yp, S = gated_linear_attention(proj_p.reshape(Bp, T, -1), lb, gn,
                                           jnp.zeros((Bp,) + b_states.shape[2:], F32), hgrn2=True, chunk=LA_CHUNK)
            p_b.append(S)
            ys, S = gated_linear_attention(_pad_rows(proj_s[:Bs, None], SUBLANES), lb, gn, b_states[j],
                                           hgrn2=True, chunk=SUBLANES, valid=1)
            s_b.append(S)
            w_mix = P["b_w_out"]
        else:
            wk = P["c_wk"]
            n_main = P["c_n_main"]
            proj_p, proj_s = linear(hp, P["c_w_in_t"], j, n_out=n_main, gain=gmix, rider=ride(hs),
                                    transposed=True, **tile)
            gate_args = (gmix, P["c_w_low"][j], P["c_w_gate"], P["c_b_gate"], j)
            gn = P["c_gnorm"][j][None, :]
            yp, S = gated_linear_attention(proj_p.reshape(Bp, T, n_main),
                                           gla_gate(hp, *gate_args, bm=TILES["gate_rows"]).reshape(Bp, T, wk), gn,
                                           jnp.zeros((Bp,) + c_states.shape[2:], F32), hgrn2=False, chunk=LA_CHUNK)
            p_c.append(S)
            ys, S = gated_linear_attention(_pad_rows(proj_s[:Bs, None], SUBLANES),
                                           _pad_rows(gla_gate(hs, *gate_args, bm=SUBLANES)[:, None], SUBLANES), gn,
                                           c_states[j], hgrn2=False, chunk=SUBLANES, valid=1)
            s_c.append(S)
            w_mix = P["c_w_out"]

        mem_kw = dict(heads=P["m_heads"], hd=P["m_hd"])
        hp = mix_out_mem_attend(yp.reshape(Bp, T, -1), w_mix, j, hp.reshape(Bp, T, D), P["norm_mem"], P["mem_w_q"],
                                p_mem, P["mem_w_out"], li, bm=TILES["mix_mem_rows"], **mem_kw).reshape(Mp, D)
        if ys.ndim == 3:
            ys = ys[:, 0]
        hs = mix_out_mem_attend_sample(ys, w_mix, j, hs, P["norm_mem"], P["mem_w_q"], s_mem, P["mem_w_out"], li,
                                       **mem_kw)
        final = P["norm_final"] if li == depth - 1 else None
        hp, hs = mlp(hp, P["norm_mlp"], P["mlp_w_up"], P["mlp_w_down"], li, final_gain=final,
                     rider=ride(hs), **mlp_tile)
        hs = hs[:Bs]

    new_rows = [jnp.stack([rows[g] for rows in new_rows], axis=0) for g in range(groups)]
    p_win = [to_window(qkv_prompt, g, rows=min(A_WINDOWS[g], T), groups=groups, heads=hpg, hd=hd_a)
             for g in range(groups)]
    return (hp.reshape(Bp, T, D), hs.reshape(Bs, 1, D), p_win, new_rows,
            jnp.stack(p_b, axis=0), jnp.stack(s_b, axis=0), jnp.stack(p_c, axis=0), jnp.stack(s_c, axis=0))


def _bias_tables(rel_bias_table, hpg, span):
    steps = np.arange(A_BLOCK)[None, :] + A_BLOCK - np.arange(2 * A_BLOCK)[:, None]
    eye = np.eye(N_BUCKETS, dtype=np.float32)
    prompt, sample = [], []
    for g, d in enumerate(A_DILATIONS):
        tg_t = rel_bias_table[:, g * hpg:(g + 1) * hpg].T
        onehot = eye[:, t5_bucket(np.clip(steps, 0, span) * d).reshape(-1)]
        bias = jnp.dot(tg_t, jnp.asarray(onehot), precision=lax.Precision.HIGHEST).reshape(hpg, 2 * A_BLOCK, A_BLOCK)
        band = jnp.asarray((steps >= 0) & (steps <= span))
        prompt.append(jnp.where(band[None], bias, -jnp.inf))
        dist = np.concatenate([(span - np.arange(span)) * d, np.zeros(1, np.int64)])
        row = jnp.dot(tg_t, jnp.asarray(eye[:, t5_bucket(dist)]), precision=lax.Precision.HIGHEST)
        sample.append(jnp.broadcast_to(row.T[:, :, None], (span + 1, hpg, LANES)))
    return jnp.stack(prompt, axis=0), jnp.stack(sample, axis=0)


def kernel(x_prompt, x_sample, cache_win0, cache_win1, cache_win2, state_hgrn, state_gla, cache_mem, mem_prompt, rel_bias_table, a_w_in, a_w_out, b_w_in, b_w_out, b_lower_bound, b_gnorm, c_w_in, c_w_gate, c_b_gate, c_w_out, c_gnorm, mem_norm, mem_w_kv, mem_w_q, mem_w_out, norm_mix, norm_mem, norm_mlp, mlp_w_up, mlp_w_down, norm_final):
    depth, D = norm_mix.shape
    hpg, hd_a = cache_win0.shape[-2], cache_win0.shape[-1]
    span = A_BLOCK
    assert all(w // d == span for w, d in zip(A_WINDOWS, A_DILATIONS))
    m_heads, m_hd = cache_mem.shape[-2], cache_mem.shape[-1]
    c_heads, c_dk, c_dv = state_gla.shape[2:]
    wk = c_heads * c_dk
    n_main = 2 * wk + 2 * c_heads * c_dv
    bias_prompt, bias_sample = _bias_tables(rel_bias_table, hpg, span)

    sm = jax.nn.softmax(b_lower_bound.astype(F32), axis=0)
    P = dict(
        a_hd=hd_a, a_hpg=hpg, m_heads=m_heads, m_hd=m_hd,
        a_bias_prompt=bias_prompt, a_bias_sample=bias_sample,
        a_w_in=a_w_in, a_w_out=a_w_out.astype(BF16), b_w_in=b_w_in, b_w_out=b_w_out.astype(BF16),
        lower_bounds=jnp.cumsum(sm, axis=0) - sm, b_gnorm=b_gnorm,
        c_w_in_t=jnp.swapaxes(c_w_in, 1, 2), c_w_low=c_w_in[:, :, n_main:], c_w_gate=c_w_gate,
        c_b_gate=c_b_gate[:, None, :], c_w_out=c_w_out.astype(BF16), c_gnorm=c_gnorm,
        c_wk=wk, c_n_main=n_main,
        mem_w_q=mem_w_q.astype(BF16), mem_w_out=mem_w_out.astype(BF16),
        norm_mix=norm_mix[:, None, :], norm_mem=norm_mem[:, None, :], norm_mlp=norm_mlp[:, None, :],
        mlp_w_up=mlp_w_up, mlp_w_down=mlp_w_down, norm_final=norm_final[None, :],
    )

    Bm, Mt, _ = mem_prompt.shape
    mem_flat = mem_prompt.reshape(Bm * Mt, D)
    mem_gain = mem_norm[:, None, :]
    p_mem = jnp.stack([linear(mem_flat, mem_w_kv, li, gain=(mem_gain, li), **TILES["mem_kv"])
                       for li in range(depth)], axis=0).reshape(depth, Bm, Mt, 2 * m_heads * m_hd)

    caches = (cache_win0, cache_win1, cache_win2)
    y_prompt, y_sample, p_win, new_rows, p_hgrn, s_hgrn, p_gla, s_gla = _trunk(
        x_prompt, x_sample, p_mem, cache_mem, caches, state_hgrn, state_gla, P)
    s_win = [window_update(c, n, chunk=TILES["window_copy_rows"]) for c, n in zip(caches, new_rows)]
    return (y_prompt, y_sample, p_win[0], p_win[1], p_win[2], p_hgrn, p_gla,
            p_mem.reshape(depth, Bm, Mt, 2, m_heads, m_hd),
            s_win[0], s_win[1], s_win[2], s_hgrn, s_gla)
```

```python
import functools
import math

import numpy as np
import jax
import jax.numpy as jnp
from jax import lax
from jax.experimental import pallas as pl
from jax.experimental.pallas import tpu as pltpu

F32 = jnp.float32
BF16 = jnp.bfloat16

EPS = 1e-6
N_MIXERS = 3
A_WINDOWS = (128, 512, 2048)
A_DILATIONS = (1, 4, 16)
A_BLOCK = 128
N_BUCKETS = 32
MAX_DISTANCE = 2048
GLA_GATE_NORMALIZER = 16.0
LA_BASE = 16
LA_CHUNK = 128
RIDER_ROWS = 16

TILES = dict(
    in_proj=dict(bm=2048, bn=512),
    mlp=dict(bm=1024, bf=512),
    mix_mem_rows=512,
    gate_rows=1024,
    attention_rows=16 * A_BLOCK,
    mem_kv=dict(bm=512, bn=512),
    window_copy_rows=512,
    window_relayout_rows=256,
)

V7X_VMEM_BYTES = 64 * 1024 * 1024
VMEM_LIMIT = V7X_VMEM_BYTES - 8 * 1024 * 1024
SUBLANES = 8
LANES = 128
V7X_MXU_WIDTH = 256


def _cparams(n_axes):
    return pltpu.CompilerParams(dimension_semantics=("arbitrary",) * n_axes,
                                vmem_limit_bytes=VMEM_LIMIT)


def _tile(n, target, align):
    if n <= target:
        return n
    t = (target // align) * align
    while n % t:
        t -= align
    return t


def _rms(x, g):
    r = lax.rsqrt(jnp.mean(x * x, axis=-1, keepdims=True) + EPS)
    return (x * r) * g


def _dot(a, b):
    return jnp.dot(a, b, preferred_element_type=F32)


def _dot_nt(a, b):
    return lax.dot_general(a, b, (((1,), (1,)), ((), ())), preferred_element_type=F32)


def _dot_tn(a, b):
    return lax.dot_general(a, b, (((0,), (0,)), ((), ())), preferred_element_type=F32)


def _bf_round(a):
    return a.astype(BF16).astype(F32)


def _linear_body(*refs, riders, transposed):
    if riders:
        x_ref, w_ref, g_ref, r_ref, o_ref, or_ref, xs_ref = refs
    else:
        x_ref, w_ref, g_ref, o_ref, xs_ref = refs
    bm = x_ref.shape[0]

    @pl.when(pl.program_id(1) == 0)
    def _():
        xs_ref[0:bm] = _rms(x_ref[...], g_ref[...]).astype(BF16)
        if riders:
            xs_ref[bm:bm + riders] = _rms(r_ref[...], g_ref[...]).astype(BF16)

    w = w_ref[...].astype(BF16)
    acc = _dot_nt(xs_ref[...], w) if transposed else _dot(xs_ref[...], w)
    if riders:
        o_ref[...] = acc[:bm]
        or_ref[...] = acc[bm:]
    else:
        o_ref[...] = acc


def linear(x, w, layer, *, n_out=None, gain, rider=None, transposed=False, bm, bn):
    M, K = x.shape
    N = (w.shape[1] if transposed else w.shape[2]) if n_out is None else n_out
    bm, bn = _tile(M, bm, SUBLANES), _tile(N, bn, LANES)
    riders = 0 if rider is None else rider.shape[0]
    g_arr, g_idx = gain
    w_spec = (pl.BlockSpec((None, bn, K), lambda i, j: (layer, j, 0)) if transposed
              else pl.BlockSpec((None, K, bn), lambda i, j: (layer, 0, j)))
    in_specs = [pl.BlockSpec((bm, K), lambda i, j: (i, 0), pipeline_mode=pl.Buffered(1)),
                w_spec,
                pl.BlockSpec((None, 1, K), lambda i, j: (g_idx, 0, 0))]
    args = [x, w, g_arr]
    out_specs = [pl.BlockSpec((bm, bn), lambda i, j: (i, j))]
    out_shape = [jax.ShapeDtypeStruct((M, N), F32)]
    if riders:
        in_specs.append(pl.BlockSpec((riders, K), lambda i, j: (0, 0)))
        args.append(rider)
        out_specs.append(pl.BlockSpec((None, riders, bn), lambda i, j: (i, 0, j)))
        out_shape.append(jax.ShapeDtypeStruct((M // bm, riders, N), F32))
    outs = pl.pallas_call(
        functools.partial(_linear_body, riders=riders, transposed=transposed),
        grid=(M // bm, N // bn),
        in_specs=in_specs,
        out_specs=out_specs,
        out_shape=out_shape,
        scratch_shapes=[pltpu.VMEM((bm + riders, K), BF16)],
        compiler_params=_cparams(2),
        name="linear",
    )(*args)
    return (outs[0], outs[1][0]) if riders else outs[0]


def _mlp_body(*refs, has_final, riders):
    refs = list(refs)
    x_ref, g_ref, wu_ref, wd_ref = refs[:4]
    pos = 4
    fg_ref = r_ref = or_ref = None
    if has_final:
        fg_ref = refs[pos]
        pos += 1
    if riders:
        r_ref = refs[pos]
        pos += 1
    o_ref = refs[pos]
    if riders:
        or_ref = refs[pos + 1]
    xs_ref = refs[-1]
    bm = x_ref.shape[0]
    f = pl.program_id(1)

    @pl.when(f == 0)
    def _():
        x = x_ref[...]
        xs_ref[0:bm] = _rms(x, g_ref[...]).astype(BF16)
        o_ref[...] = x
        if riders:
            r = r_ref[...]
            xs_ref[bm:bm + riders] = _rms(r, g_ref[...]).astype(BF16)
            or_ref[...] = r

    def up_down(xs):
        hdn = _dot(xs, wu_ref[...].astype(BF16))
        hdn = jnp.square(jnp.maximum(hdn, 0.0)).astype(BF16)
        return _dot(hdn, wd_ref[...].astype(BF16))

    if riders:
        @pl.when(pl.program_id(0) == 0)
        def _():
            down = up_down(xs_ref[...])
            o_ref[...] += down[:bm]
            or_ref[...] += down[bm:]

        @pl.when(pl.program_id(0) != 0)
        def _():
            o_ref[...] += up_down(xs_ref[0:bm])
    else:
        o_ref[...] += up_down(xs_ref[...])

    if has_final:
        @pl.when(f == pl.num_programs(1) - 1)
        def _():
            o_ref[...] = _rms(o_ref[...], fg_ref[...])
            if riders:
                or_ref[...] = _rms(or_ref[...], fg_ref[...])


def mlp(x, gains, w_up, w_down, layer, *, final_gain=None, rider=None, bm, bf):
    M, D = x.shape
    FF = w_up.shape[-1]
    bm, bf = _tile(M, bm, SUBLANES), _tile(FF, bf, LANES)
    riders = 0 if rider is None else rider.shape[0]
    in_specs = [pl.BlockSpec((bm, D), lambda i, f: (i, 0)),
                pl.BlockSpec((None, 1, D), lambda i, f: (layer, 0, 0)),
                pl.BlockSpec((None, D, bf), lambda i, f: (layer, 0, f)),
                pl.BlockSpec((None, bf, D), lambda i, f: (layer, f, 0))]
    args = [x, gains, w_up, w_down]
    if final_gain is not None:
        in_specs.append(pl.BlockSpec((1, D), lambda i, f: (0, 0)))
        args.append(final_gain)
    out_specs = [pl.BlockSpec((bm, D), lambda i, f: (i, 0))]
    out_shape = [jax.ShapeDtypeStruct((M, D), F32)]
    if riders:
        in_specs.append(pl.BlockSpec((riders, D), lambda i, f: (0, 0)))
        args.append(rider)
        out_specs.append(pl.BlockSpec((None, riders, D), lambda i, f: (i, 0, 0)))
        out_shape.append(jax.ShapeDtypeStruct((M // bm, riders, D), F32))
    outs = pl.pallas_call(
        functools.partial(_mlp_body, has_final=final_gain is not None, riders=riders),
        grid=(M // bm, FF // bf),
        in_specs=in_specs,
        out_specs=out_specs,
        out_shape=out_shape,
        scratch_shapes=[pltpu.VMEM((bm + riders, D), BF16)],
        compiler_params=_cparams(2),
        name="mlp",
    )(*args)
    return (outs[0], outs[1][0]) if riders else outs[0]


def _mem_body(y_ref, wm_ref, h_ref, g_ref, wq_ref, kv_ref, wo_ref, o_ref, *, heads, hd):
    x = h_ref[...] + _dot(y_ref[...].astype(BF16), wm_ref[...])
    xn = _rms(x, g_ref[...]).astype(BF16)
    q = _dot(xn, wq_ref[...])
    width = heads * hd
    scale = hd ** -0.5
    outs = []
    for h in range(heads):
        qh = q[:, h * hd:(h + 1) * hd].astype(BF16)
        kh = kv_ref[:, h * hd:(h + 1) * hd].astype(BF16)
        vh = kv_ref[:, width + h * hd:width + (h + 1) * hd].astype(BF16)
        s = _dot_nt(qh, kh) * scale
        p = jnp.exp(s - jnp.max(s, axis=-1, keepdims=True))
        p = p / jnp.sum(p, axis=-1, keepdims=True)
        outs.append(_dot(p.astype(BF16), vh))
    o = jnp.concatenate(outs, axis=-1).astype(BF16)
    o_ref[...] = x + _dot(o, wo_ref[...])


def mix_out_mem_attend(y, w_mix, mix_layer, h, gains, w_q, kv, w_o, layer, *, heads, hd, bm):
    B, T, D = h.shape
    Ky = y.shape[-1]
    Mt = kv.shape[2]
    width = heads * hd
    bm = _tile(T, bm, SUBLANES)
    return pl.pallas_call(
        functools.partial(_mem_body, heads=heads, hd=hd),
        grid=(B, T // bm),
        in_specs=[pl.BlockSpec((None, bm, Ky), lambda b, i: (b, i, 0)),
                  pl.BlockSpec((None, Ky, D), lambda b, i: (mix_layer, 0, 0), pipeline_mode=pl.Buffered(1)),
                  pl.BlockSpec((None, bm, D), lambda b, i: (b, i, 0)),
                  pl.BlockSpec((None, 1, D), lambda b, i: (layer, 0, 0)),
                  pl.BlockSpec((None, D, width), lambda b, i: (layer, 0, 0), pipeline_mode=pl.Buffered(1)),
                  pl.BlockSpec((None, None, Mt, 2 * width), lambda b, i: (layer, b, 0, 0)),
                  pl.BlockSpec((None, width, D), lambda b, i: (layer, 0, 0), pipeline_mode=pl.Buffered(1))],
        out_specs=pl.BlockSpec((None, bm, D), lambda b, i: (b, i, 0)),
        out_shape=jax.ShapeDtypeStruct((B, T, D), F32),
        compiler_params=_cparams(2),
        name="mix_out_mem_attend",
    )(y, w_mix, h, gains, w_q, kv, w_o)


def _mem_sample_body(y_ref, wm_ref, h_ref, g_ref, wq_ref, kv_ref, wo_ref, o_ref, *, heads, hd):
    batch = h_ref.shape[0]
    scale = hd ** -0.5
    x = h_ref[...] + _dot(y_ref[...].astype(BF16), wm_ref[...])
    q = _bf_round(_dot(_rms(x, g_ref[...]).astype(BF16), wq_ref[...]))
    outs = []
    for b in range(batch):
        qb = jnp.concatenate([q[b:b + 1, h * hd:(h + 1) * hd] for h in range(heads)], axis=0)
        k = _bf_round(kv_ref[b, :, 0])
        v = _bf_round(kv_ref[b, :, 1])
        s = jnp.sum(k * qb[None], axis=-1, keepdims=True) * scale
        p = jnp.exp(s - jnp.max(s, axis=0))
        p = _bf_round(p / jnp.sum(p, axis=0))
        ob = jnp.sum(p * v, axis=0)
        outs.append(jnp.concatenate([ob[h:h + 1] for h in range(heads)], axis=1))
    o = jnp.concatenate(outs, axis=0).astype(BF16)
    o_ref[...] = x + _dot(o, wo_ref[...])


def mix_out_mem_attend_sample(y, w_mix, mix_layer, h, gains, w_q, kv, w_o, layer, *, heads, hd):
    B, D = h.shape
    Ky = y.shape[-1]
    Mt = kv.shape[2]
    width = heads * hd
    one = pl.Buffered(1)
    return pl.pallas_call(
        functools.partial(_mem_sample_body, heads=heads, hd=hd),
        grid=(1,),
        in_specs=[pl.BlockSpec((B, Ky), lambda i: (0, 0)),
                  pl.BlockSpec((None, Ky, D), lambda i: (mix_layer, 0, 0), pipeline_mode=one),
                  pl.BlockSpec((B, D), lambda i: (0, 0)),
                  pl.BlockSpec((None, 1, D), lambda i: (layer, 0, 0)),
                  pl.BlockSpec((None, D, width), lambda i: (layer, 0, 0), pipeline_mode=one),
                  pl.BlockSpec((None, B, Mt, 2, heads, hd), lambda i: (layer, 0, 0, 0, 0, 0), pipeline_mode=one),
                  pl.BlockSpec((None, width, D), lambda i: (layer, 0, 0), pipeline_mode=one)],
        out_specs=pl.BlockSpec((B, D), lambda i: (0, 0)),
        out_shape=jax.ShapeDtypeStruct((B, D), F32),
        compiler_params=_cparams(1),
        name="mix_out_mem_attend_sample",
    )(y, w_mix, h, gains, w_q, kv, w_o)


def t5_bucket(dist):
    dist = np.asarray(dist, np.int64)
    max_exact = N_BUCKETS // 2
    d = np.maximum(dist, 1).astype(np.float64)
    large = max_exact + (np.log(d / max_exact) / math.log(MAX_DISTANCE / max_exact)
                         * (N_BUCKETS - max_exact)).astype(np.int64)
    return np.where(dist < max_exact, dist, np.minimum(large, N_BUCKETS - 1)).astype(np.int32)


def _attn_block(q, k2, v2, bias2, scale):
    blk = q.shape[0]
    st = _dot_nt(k2.astype(BF16), q.astype(BF16)) * scale + bias2
    m = jnp.max(st, axis=0, keepdims=True)
    p = jnp.exp(st - m)
    l = jnp.sum(p, axis=0, keepdims=True)
    o = _dot_tn((p * (1.0 / l)).astype(BF16), v2.astype(BF16))
    return o, jnp.broadcast_to(m + jnp.log(l), (blk, blk)).T


def _dil_attn_body(*refs, dils, tile, hd):
    n_groups = len(dils)
    bias_ref, o_ref, og_ref, lse_ref = refs[5 * n_groups:]
    not_first_tile = pl.program_id(1) > 0
    blk = A_BLOCK
    scale = hd ** -0.5

    for g, d in enumerate(dils):
        q_ref, kc_ref, vc_ref, kp_ref, vp_ref = refs[5 * g:5 * g + 5]
        stream_rows = blk * d
        n_blocks = tile // stream_rows

        def rows(start):
            return pl.ds(start, blk) if d == 1 else pl.ds(start, blk, stride=d)

        bias2 = bias_ref[g]
        bias2_first = jnp.concatenate(
            [jnp.where(not_first_tile, bias2[:blk], -jnp.inf), bias2[blk:]], axis=0)

        for m in range(n_blocks):
            for r in range(d):
                s = r + m * stream_rows
                if m == 0:
                    kp, vp, bias = kp_ref[rows(s), :], vp_ref[rows(s), :], bias2_first
                else:
                    kp, vp, bias = kc_ref[rows(s - stream_rows), :], vc_ref[rows(s - stream_rows), :], bias2
                k2 = jnp.concatenate([kp, kc_ref[rows(s), :]], axis=0)
                v2 = jnp.concatenate([vp, vc_ref[rows(s), :]], axis=0)
                o, lse = _attn_block(q_ref[rows(s), :], k2, v2, bias, scale)
                og_ref[g, rows(s), :] = o
                lse_ref[g, rows(s), :] = lse

    lses = [lse_ref[g] for g in range(n_groups)]
    mx = functools.reduce(jnp.maximum, lses)
    es = [jnp.exp(l - mx) for l in lses]
    den = functools.reduce(lambda a, b: a + b, es)
    acc = es[0] * og_ref[0]
    for g in range(1, n_groups):
        acc = acc + es[g] * og_ref[g]
    o_ref[...] = acc / den


def dilated_attention_prompt(qkv, bias, *, heads, hd, tile):
    B, T, _ = qkv.shape
    dils = A_DILATIONS
    n_groups = len(dils)
    tile = min(tile, T)
    assert hd == LANES and T % tile == 0 and all(tile % (A_BLOCK * d) == 0 for d in dils)
    sec = n_groups * heads

    def cur(which, g):
        return pl.BlockSpec((None, tile, hd), lambda b, n, h: (b, n, which * sec + g * heads + h))

    def prev(which, g):
        rows = A_BLOCK * dils[g]
        per = tile // rows
        return pl.BlockSpec((None, rows, hd),
                            lambda b, n, h: (b, jnp.maximum(n * per - 1, 0), which * sec + g * heads + h))

    in_specs = []
    for g in range(n_groups):
        in_specs += [cur(0, g), cur(1, g), cur(2, g), prev(1, g), prev(2, g)]
    in_specs.append(pl.BlockSpec((n_groups, None, 2 * A_BLOCK, A_BLOCK), lambda b, n, h: (0, h, 0, 0)))
    return pl.pallas_call(
        functools.partial(_dil_attn_body, dils=dils, tile=tile, hd=hd),
        grid=(B, T // tile, heads),
        in_specs=in_specs,
        out_specs=pl.BlockSpec((None, tile, hd), lambda b, n, h: (b, n, h)),
        out_shape=jax.ShapeDtypeStruct((B, T, heads * hd), F32),
        scratch_shapes=[pltpu.VMEM((n_groups, tile, hd), F32), pltpu.VMEM((n_groups, tile, hd), F32)],
        compiler_params=_cparams(3),
        name="dilated_attention_prompt",
    )(*([qkv] * (5 * n_groups)), bias)


def _dec_attn_body(qkv_ref, *refs, groups, span, scale):
    bufs, bias_ref, o_ref = refs[:groups], refs[groups], refs[groups + 1]
    outs, lses = [], []
    for g in range(groups):
        q = _bf_round(qkv_ref[g])
        kn = _bf_round(qkv_ref[groups + g])
        vn = _bf_round(qkv_ref[2 * groups + g])
        k = _bf_round(bufs[g][:, 0])
        v = _bf_round(bufs[g][:, 1])
        s = jnp.sum(k * q[None], axis=-1, keepdims=True) * scale + bias_ref[g, :span]
        s_self = jnp.sum(q * kn, axis=-1, keepdims=True) * scale + bias_ref[g, span]
        m = jnp.maximum(jnp.max(s, axis=0), s_self)
        p = jnp.exp(s - m[None])
        p_self = jnp.exp(s_self - m)
        l = jnp.sum(p, axis=0) + p_self
        outs.append(jnp.sum(_bf_round(p / l[None]) * v, axis=0) + _bf_round(p_self / l) * vn)
        lses.append(m + jnp.log(l))
    mx = functools.reduce(jnp.maximum, lses)
    es = [jnp.exp(l - mx) for l in lses]
    den = functools.reduce(lambda a, b: a + b, es)
    acc = (es[0] / den) * outs[0]
    for g in range(1, groups):
        acc = acc + (es[g] / den) * outs[g]
    o_ref[...] = acc


def dilated_attention_sample(qkv, caches, layer, bias, *, heads, hd, span):
    B = qkv.shape[0]
    groups = len(caches)
    in_specs = [pl.BlockSpec((None, 3 * groups, heads, hd), lambda b: (b, 0, 0, 0))]
    views = []
    for g, c in enumerate(caches):
        d = A_DILATIONS[g]
        assert c.shape[2] == span * d
        views.append(c.reshape(c.shape[0], B, span, d, 2, heads, hd))
        in_specs.append(pl.BlockSpec((None, None, span, None, 2, heads, hd),
                                     lambda b: (layer, b, 0, 0, 0, 0, 0)))
    in_specs.append(pl.BlockSpec((groups, span + 1, heads, hd), lambda b: (0, 0, 0, 0)))
    return pl.pallas_call(
        functools.partial(_dec_attn_body, groups=groups, span=span, scale=hd ** -0.5),
        grid=(B,),
        in_specs=in_specs,
        out_specs=pl.BlockSpec((None, heads, hd), lambda b: (b, 0, 0)),
        out_shape=jax.ShapeDtypeStruct((B, heads, hd), F32),
        compiler_params=_cparams(1),
        name="dilated_attention_sample",
    )(qkv, *views, bias)


def _to_window_body(*refs, heads, hd):
    o_ref = refs[-1]
    chunk = refs[0].shape[0]
    for layer in range(o_ref.shape[0]):
        for s in range(2):
            src = refs[2 * layer + s]
            for h in range(heads):
                o_ref[layer, pl.ds(s * heads + h, chunk, stride=2 * heads), :] = src[:, h * hd:(h + 1) * hd]


def to_window(qkvs, g, *, rows, groups, heads, hd):
    B, T, _ = qkvs[0].shape
    W = heads * hd
    chunk = _tile(rows, TILES["window_relayout_rows"], SUBLANES)
    assert (T - rows) % chunk == 0
    first = (T - rows) // chunk
    in_specs, args = [], []
    for qkv in qkvs:
        in_specs += [pl.BlockSpec((None, chunk, W), lambda b, c: (b, first + c, groups + g)),
                     pl.BlockSpec((None, chunk, W), lambda b, c: (b, first + c, 2 * groups + g))]
        args += [qkv, qkv]
    per_token = 2 * heads
    out = pl.pallas_call(
        functools.partial(_to_window_body, heads=heads, hd=hd),
        grid=(B, rows // chunk),
        in_specs=in_specs,
        out_specs=pl.BlockSpec((len(qkvs), None, chunk * per_token, hd), lambda b, c: (0, b, c, 0)),
        out_shape=jax.ShapeDtypeStruct((len(qkvs), B, rows * per_token, hd), F32),
        compiler_params=_cparams(2),
        name="to_window",
    )(*args)
    return out.reshape(len(qkvs), B, rows, 2, heads, hd)


def _window_update_body(main_ref, next_ref, new_ref, o_ref):
    rows = main_ref.shape[0]
    last_chunk = pl.program_id(2) == pl.num_programs(2) - 1
    o_ref[0:rows - 1] = main_ref[1:rows]
    o_ref[rows - 1:rows] = jnp.where(last_chunk, new_ref[...], next_ref[...])


def window_update(cache, new, *, chunk):
    n_layers, B, rows = cache.shape[:3]
    tail = cache.shape[3:]
    chunk = _tile(rows, chunk, 1)
    zeros = (0,) * len(tail)
    return pl.pallas_call(
        _window_update_body,
        grid=(n_layers, B, rows // chunk),
        in_specs=[pl.BlockSpec((None, None, chunk) + tail, lambda j, b, c: (j, b, c) + zeros),
                  pl.BlockSpec((None, None, 1) + tail,
                               lambda j, b, c: (j, b, jnp.minimum((c + 1) * chunk, rows - 1)) + zeros),
                  pl.BlockSpec((None, None, 1) + tail, lambda j, b, c: (j, b, 0) + zeros)],
        out_specs=pl.BlockSpec((None, None, chunk) + tail, lambda j, b, c: (j, b, c) + zeros),
        out_shape=jax.ShapeDtypeStruct(cache.shape, cache.dtype),
        compiler_params=_cparams(3),
        name="window_update",
    )(cache, cache, new)


def _gla_gate_body(x_ref, g_ref, wl_ref, wg_ref, b_ref, o_ref):
    xn = _rms(x_ref[...], g_ref[...]).astype(BF16)
    low = _dot(xn, wl_ref[...].astype(BF16))
    z = _dot(low.astype(BF16), wg_ref[...].astype(BF16)) + b_ref[...]
    log_sig = jnp.minimum(z, 0.0) - jnp.log(1.0 + jnp.exp(-jnp.abs(z)))
    o_ref[...] = log_sig / GLA_GATE_NORMALIZER


def gla_gate(x, gain, w_low, w_gate, b_gate, layer, *, bm):
    M, D = x.shape
    g_arr, g_idx = gain
    rank, dk = w_gate.shape[1], w_gate.shape[2]
    bm = _tile(M, bm, SUBLANES)
    return pl.pallas_call(
        _gla_gate_body,
        grid=(M // bm,),
        in_specs=[pl.BlockSpec((bm, D), lambda i: (i, 0)),
                  pl.BlockSpec((None, 1, D), lambda i: (g_idx, 0, 0)),
                  pl.BlockSpec((D, rank), lambda i: (0, 0)),
                  pl.BlockSpec((None, rank, dk), lambda i: (layer, 0, 0)),
                  pl.BlockSpec((None, 1, dk), lambda i: (layer, 0, 0))],
        out_specs=pl.BlockSpec((bm, dk), lambda i: (i, 0)),
        out_shape=jax.ShapeDtypeStruct((M, dk), F32),
        compiler_params=_cparams(1),
        name="gla_gate",
    )(x, g_arr, w_low, w_gate, b_gate)


def _cumsum_rows(x):
    rows = x.shape[0]
    row = lax.broadcasted_iota(jnp.int32, x.shape, 0)
    s = 1
    while s < rows:
        x = x + jnp.where(row >= s, pltpu.roll(x, s, axis=0), 0.0)
        s *= 2
    return x


def _block_refs(bcs, m, pairs):
    rows, dk = bcs.shape
    nb = rows // m
    ends = [bcs[(k + 1) * m - 1:(k + 1) * m, :] for k in range(nb)]
    parts = []
    for k in range(nb):
        if pairs:
            r = ends[k - 1] if k % 2 == 1 else ends[k]
        else:
            r = ends[k - 1] if k > 0 else jnp.zeros((1, dk), F32)
        parts.append(jnp.broadcast_to(r, (m, dk)))
    return parts[0] if nb == 1 else jnp.concatenate(parts, axis=0)


def _la_body(a_ref, b_ref, c_ref, gate_ref, e_ref, gn_ref, s0_ref, y_ref, so_ref, st_ref,
             *, hgrn2, heads, dk, dv, chunk, base, valid, q_scale, gsz):
    n = pl.program_id(1)
    groups = heads // gsz

    def own_block(shape, row_width, col_width):
        r = lax.broadcasted_iota(jnp.int32, shape, 0) // row_width
        c = lax.broadcasted_iota(jnp.int32, shape, 1) // col_width
        return r == c

    def block_rows(x, width):
        if gsz == 1:
            return x
        lane_head = lax.broadcasted_iota(jnp.int32, x.shape, 1) // width
        return jnp.concatenate([jnp.where(lane_head == i, x, jnp.zeros_like(x)) for i in range(gsz)], axis=0)

    @pl.when(n == 0)
    def _():
        for gi in range(groups):
            rows = []
            for i in range(gsz):
                blocks = [s0_ref[gi * gsz + i].T if j == i else jnp.zeros((dv, dk), F32) for j in range(gsz)]
                rows.append(blocks[0] if gsz == 1 else jnp.concatenate(blocks, axis=1))
            st_ref[gi] = rows[0] if gsz == 1 else jnp.concatenate(rows, axis=0)

    tq = lax.broadcasted_iota(jnp.int32, (chunk, chunk), 0)
    ts = lax.broadcasted_iota(jnp.int32, (chunk, chunk), 1)
    row = lax.broadcasted_iota(jnp.int32, (chunk, 1), 0)
    sh = int(math.log2(base))
    mask_diag = jnp.logical_and((tq >> sh) == (ts >> sh), ts <= tq)
    levels = []
    m = base
    while m < chunk:
        s1 = int(math.log2(m))
        mk = jnp.logical_and((tq >> (s1 + 1)) == (ts >> (s1 + 1)),
                             jnp.logical_and(((tq >> s1) & 1) == 1, ((ts >> s1) & 1) == 0))
        levels.append((m, mk))
        m *= 2
    if gsz > 1:
        mask_diag = jnp.concatenate([mask_diag] * gsz, axis=1)
        levels = [(m, jnp.concatenate([mk] * gsz, axis=1)) for m, mk in levels]
        state_own = own_block((gsz * dv, gsz * dk), dv, dk)
    gn = gn_ref[...]

    def one_group(gi):
        ksl = slice(gi * gsz * dk, (gi + 1) * gsz * dk)
        vsl = slice(gi * gsz * dv, (gi + 1) * gsz * dv)
        if hgrn2:
            a = a_ref[:, ksl]
            lb = e_ref[:, ksl]
            f = lb + (1.0 - lb) * jax.nn.sigmoid(b_ref[:, ksl])
            qa = a * jax.nn.sigmoid(a)
            kk = 1.0 - f
            g = jnp.log(f)
        else:
            qa = a_ref[:, ksl] * q_scale
            kk = b_ref[:, ksl]
            g = e_ref[:, ksl]
        v = c_ref[:, vsl]
        if valid < chunk:
            kk = jnp.where(row < valid, kk, 0.0)
            g = jnp.where(row < valid, g, 0.0)
        bcs = _cumsum_rows(g)
        vb = v.astype(BF16)
        st = st_ref[gi]

        o = _dot_nt((qa * jnp.exp(bcs)).astype(BF16), st.astype(BF16))

        ref_d = _block_refs(bcs, base, False)
        a_mat = _dot_nt((qa * jnp.exp(bcs - ref_d)).astype(BF16),
                        block_rows((kk * jnp.exp(ref_d - bcs)).astype(BF16), dk))
        a_mat = jnp.where(mask_diag, a_mat, 0.0)
        for (m, mk) in levels:
            dec = jnp.exp(-jnp.abs(bcs - _block_refs(bcs, m, True)))
            a_mat = a_mat + jnp.where(mk, _dot_nt((qa * dec).astype(BF16), block_rows((kk * dec).astype(BF16), dk)),
                                      0.0)
        o = o + _dot(a_mat.astype(BF16), block_rows(vb, dv))

        b_last = bcs[chunk - 1:chunk, :]
        kd = (kk * jnp.exp(b_last - bcs)).astype(BF16)
        update = _dot_tn(vb, kd)
        if gsz > 1:
            update = jnp.where(state_own, update, 0.0)
        st_new = jnp.exp(b_last) * st + update

        gate = gate_ref[:, vsl]
        normed = [_rms(o[:, i * dv:(i + 1) * dv], gn) for i in range(gsz)]
        normed = normed[0] if gsz == 1 else jnp.concatenate(normed, axis=1)
        return st_new, normed * (gate * jax.nn.sigmoid(gate))

    for gi in range(groups):
        st_ref[gi], y_ref[:, gi * gsz * dv:(gi + 1) * gsz * dv] = one_group(gi)

    @pl.when(n == pl.num_programs(1) - 1)
    def _():
        for h in range(heads):
            gi, i = divmod(h, gsz)
            so_ref[h] = st_ref[gi][i * dv:(i + 1) * dv, i * dk:(i + 1) * dk].T


def gated_linear_attention(proj, extra, gnorm, s0, *, hgrn2, chunk, valid=None):
    B, T, _ = proj.shape
    _, H, dk, dv = s0.shape
    chunk = min(chunk, T)
    assert T % chunk == 0
    base = min(LA_BASE, chunk)
    valid = chunk if valid is None else valid
    wk, wv = H * dk, H * dv
    gsz = max(1, V7X_MXU_WIDTH // max(dk, dv))
    if H % gsz or chunk < LANES:
        gsz = 1

    def col(width, start):
        assert start % width == 0
        return pl.BlockSpec((None, chunk, width), lambda b, n: (b, n, start // width))

    if hgrn2:
        specs = [col(wk, 0), col(wk, wk), col(wv, 2 * wk), col(wv, 2 * wk + wv),
                 pl.BlockSpec((1, wk), lambda b, n: (0, 0))]
    else:
        specs = [col(wk, 0), col(wk, wk), col(wv, 2 * wk), col(wv, 2 * wk + wv),
                 pl.BlockSpec((None, chunk, wk), lambda b, n: (b, n, 0))]
    specs += [pl.BlockSpec((1, dv), lambda b, n: (0, 0)),
              pl.BlockSpec((None, H, dk, dv), lambda b, n: (b, 0, 0, 0))]
    return pl.pallas_call(
        functools.partial(_la_body, hgrn2=hgrn2, heads=H, dk=dk, dv=dv, chunk=chunk, base=base,
                          valid=valid, q_scale=1.0 if hgrn2 else dk ** -0.5, gsz=gsz),
        grid=(B, T // chunk),
        in_specs=specs,
        out_specs=[pl.BlockSpec((None, chunk, wv), lambda b, n: (b, n, 0)),
                   pl.BlockSpec((None, H, dk, dv), lambda b, n: (b, 0, 0, 0))],
        out_shape=[jax.ShapeDtypeStruct((B, T, wv), F32),
                   jax.ShapeDtypeStruct((B, H, dk, dv), F32)],
        scratch_shapes=[pltpu.VMEM((H // gsz, gsz * dv, gsz * dk), F32)],
        compiler_params=_cparams(2),
        name="gated_linear_attention",
    )(proj, proj, proj, proj, extra, gnorm, s0)


def _pad_rows(x, rows):
    return jnp.pad(x, ((0, 0), (0, rows - x.shape[1]), (0, 0)))


def _trunk(xp, xs, p_mem, s_mem, caches, b_states, c_states, P):
    Bp, T, D = xp.shape
    Bs = xs.shape[0]
    assert xs.shape[1] == 1 and Bs <= RIDER_ROWS
    Mp = Bp * T
    depth = P["norm_mix"].shape[0]
    hd_a, hpg = P["a_hd"], P["a_hpg"]
    groups = len(A_DILATIONS)
    Wa = hpg * hd_a
    tile, mlp_tile = TILES["in_proj"], TILES["mlp"]

    def ride(h):
        return jnp.pad(h, ((0, RIDER_ROWS - Bs), (0, 0)))

    hp, hs = xp.reshape(Mp, D), xs.reshape(Bs, D)
    qkv_prompt, new_rows, p_b, s_b, p_c, s_c = [], [], [], [], [], []
    for li in range(depth):
        kind, j = li % N_MIXERS, li // N_MIXERS
        gmix = (P["norm_mix"], li)
        if kind == 0:
            qkv_p, qkv_s = linear(hp, P["a_w_in"], j, gain=gmix, rider=ride(hs), **tile)
            qkv3 = qkv_p.reshape(Bp, T, 3 * groups * Wa)
            yp = dilated_attention_prompt(qkv3, P["a_bias_prompt"], heads=hpg, hd=hd_a,
                                          tile=TILES["attention_rows"])
            qkv_prompt.append(qkv3)
            qkv5 = qkv_s[:Bs].reshape(Bs, 3, groups, hpg, hd_a)
            ys = dilated_attention_sample(qkv5.reshape(Bs, 3 * groups, hpg, hd_a), caches, j,
                                          P["a_bias_sample"], heads=hpg, hd=hd_a, span=A_BLOCK)
            new_rows.append([qkv5[:, 1:3, g][:, None] for g in range(groups)])
            ys = ys.reshape(Bs, Wa)
            w_mix = P["a_w_out"]
        elif kind == 1:
            proj_p, proj_s = linear(hp, P["b_w_in"], j, gain=gmix, rider=ride(hs), **tile)
            lb = P["lower_bounds"][li][None, :]
            gn = P["b_gnorm"][j][None, :]
            yp, S = gated_linear_attention(proj_p.reshape(Bp, T, -1), lb, gn,
                                           jnp.zeros((Bp,) + b_states.shape[2:], F32), hgrn2=True, chunk=LA_CHUNK)
            p_b.append(S)
            ys, S = gated_linear_attention(_pad_rows(proj_s[:Bs, None], SUBLANES), lb, gn, b_states[j],
                                           hgrn2=True, chunk=SUBLANES, valid=1)
            s_b.append(S)
            w_mix = P["b_w_out"]
        else:
            wk = P["c_wk"]
            n_main = P["c_n_main"]
            proj_p, proj_s = linear(hp, P["c_w_in_t"], j, n_out=n_main, gain=gmix, rider=ride(hs),
                                    transposed=True, **tile)
            gate_args = (gmix, P["c_w_low"][j], P["c_w_gate"], P["c_b_gate"], j)
            gn = P["c_gnorm"][j][None, :]
            yp, S = gated_linear_attention(proj_p.reshape(Bp, T, n_main),
                                           gla_gate(hp, *gate_args, bm=TILES["gate_rows"]).reshape(Bp, T, wk), gn,
                                           jnp.zeros((Bp,) + c_states.shape[2:], F32), hgrn2=False, chunk=LA_CHUNK)
            p_c.append(S)
            ys, S = gated_linear_attention(_pad_rows(proj_s[:Bs, None], SUBLANES),
                                           _pad_rows(gla_gate(hs, *gate_args, bm=SUBLANES)[:, None], SUBLANES), gn,
                                           c_states[j], hgrn2=False, chunk=SUBLANES, valid=1)
            s_c.append(S)
            w_mix = P["c_w_out"]

        mem_kw = dict(heads=P["m_heads"], hd=P["m_hd"])
        hp = mix_out_mem_attend(yp.reshape(Bp, T, -1), w_mix, j, hp.reshape(Bp, T, D), P["norm_mem"], P["mem_w_q"],
                                p_mem, P["mem_w_out"], li, bm=TILES["mix_mem_rows"], **mem_kw).reshape(Mp, D)
        if ys.ndim == 3:
            ys = ys[:, 0]
        hs = mix_out_mem_attend_sample(ys, w_mix, j, hs, P["norm_mem"], P["mem_w_q"], s_mem, P["mem_w_out"], li,
                                       **mem_kw)
        final = P["norm_final"] if li == depth - 1 else None
        hp, hs = mlp(hp, P["norm_mlp"], P["mlp_w_up"], P["mlp_w_down"], li, final_gain=final,
                     rider=ride(hs), **mlp_tile)
        hs = hs[:Bs]

    new_rows = [jnp.stack([rows[g] for rows in new_rows], axis=0) for g in range(groups)]
    p_win = [to_window(qkv_prompt, g, rows=min(A_WINDOWS[g], T), groups=groups, heads=hpg, hd=hd_a)
             for g in range(groups)]
    return (hp.reshape(Bp, T, D), hs.reshape(Bs, 1, D), p_win, new_rows,
            jnp.stack(p_b, axis=0), jnp.stack(s_b, axis=0), jnp.stack(p_c, axis=0), jnp.stack(s_c, axis=0))


def _bias_tables(rel_bias_table, hpg, span):
    steps = np.arange(A_BLOCK)[None, :] + A_BLOCK - np.arange(2 * A_BLOCK)[:, None]
    eye = np.eye(N_BUCKETS, dtype=np.float32)
    prompt, sample = [], []
    for g, d in enumerate(A_DILATIONS):
        tg_t = rel_bias_table[:, g * hpg:(g + 1) * hpg].T
        onehot = eye[:, t5_bucket(np.clip(steps, 0, span) * d).reshape(-1)]
        bias = jnp.dot(tg_t, jnp.asarray(onehot), precision=lax.Precision.HIGHEST).reshape(hpg, 2 * A_BLOCK, A_BLOCK)
        band = jnp.asarray((steps >= 0) & (steps <= span))
        prompt.append(jnp.where(band[None], bias, -jnp.inf))
        dist = np.concatenate([(span - np.arange(span)) * d, np.zeros(1, np.int64)])
        row = jnp.dot(tg_t, jnp.asarray(eye[:, t5_bucket(dist)]), precision=lax.Precision.HIGHEST)
        sample.append(jnp.broadcast_to(row.T[:, :, None], (span + 1, hpg, LANES)))
    return jnp.stack(prompt, axis=0), jnp.stack(sample, axis=0)


def kernel(x_prompt, x_sample, cache_win0, cache_win1, cache_win2, state_hgrn, state_gla, cache_mem, mem_prompt, rel_bias_table, a_w_in, a_w_out, b_w_in, b_w_out, b_lower_bound, b_gnorm, c_w_in, c_w_gate, c_b_gate, c_w_out, c_gnorm, mem_norm, mem_w_kv, mem_w_q, mem_w_out, norm_mix, norm_mem, norm_mlp, mlp_w_up, mlp_w_down, norm_final):
    depth, D = norm_mix.shape
    hpg, hd_a = cache_win0.shape[-2], cache_win0.shape[-1]
    span = A_BLOCK
    assert all(w // d == span for w, d in zip(A_WINDOWS, A_DILATIONS))
    m_heads, m_hd = cache_mem.shape[-2], cache_mem.shape[-1]
    c_heads, c_dk, c_dv = state_gla.shape[2:]
    wk = c_heads * c_dk
    n_main = 2 * wk + 2 * c_heads * c_dv
    bias_prompt, bias_sample = _bias_tables(rel_bias_table, hpg, span)

    sm = jax.nn.softmax(b_lower_bound.astype(F32), axis=0)
    P = dict(
        a_hd=hd_a, a_hpg=hpg, m_heads=m_heads, m_hd=m_hd,
        a_bias_prompt=bias_prompt, a_bias_sample=bias_sample,
        a_w_in=a_w_in, a_w_out=a_w_out.astype(BF16), b_w_in=b_w_in, b_w_out=b_w_out.astype(BF16),
        lower_bounds=jnp.cumsum(sm, axis=0) - sm, b_gnorm=b_gnorm,
        c_w_in_t=jnp.swapaxes(c_w_in, 1, 2), c_w_low=c_w_in[:, :, n_main:], c_w_gate=c_w_gate,
        c_b_gate=c_b_gate[:, None, :], c_w_out=c_w_out.astype(BF16), c_gnorm=c_gnorm,
        c_wk=wk, c_n_main=n_main,
        mem_w_q=mem_w_q.astype(BF16), mem_w_out=mem_w_out.astype(BF16),
        norm_mix=norm_mix[:, None, :], norm_mem=norm_mem[:, None, :], norm_mlp=norm_mlp[:, None, :],
        mlp_w_up=mlp_w_up, mlp_w_down=mlp_w_down, norm_final=norm_final[None, :],
    )

    Bm, Mt, _ = mem_prompt.shape
    mem_flat = mem_prompt.reshape(Bm * Mt, D)
    mem_gain = mem_norm[:, None, :]
    p_mem = jnp.stack([linear(mem_flat, mem_w_kv, li, gain=(mem_gain, li), **TILES["mem_kv"])
                       for li in range(depth)], axis=0).reshape(depth, Bm, Mt, 2 * m_heads * m_hd)

    caches = (cache_win0, cache_win1, cache_win2)
    y_prompt, y_sample, p_win, new_rows, p_hgrn, s_hgrn, p_gla, s_gla = _trunk(
        x_prompt, x_sample, p_mem, cache_mem, caches, state_hgrn, state_gla, P)
    s_win = [window_update(c, n, chunk=TILES["window_copy_rows"]) for c, n in zip(caches, new_rows)]
    return (y_prompt, y_sample, p_win[0], p_win[1], p_win[2], p_hgrn, p_gla,
            p_mem.reshape(depth, Bm, Mt, 2, m_heads, m_hd),
            s_win[0], s_win[1], s_win[2], s_hgrn, s_gla)
```

```python
import functools
import math

import numpy as np
import jax
import jax.numpy as jnp
from jax import lax
from jax.experimental import pallas as pl
from jax.experimental.pallas import tpu as pltpu

F32 = jnp.float32
BF16 = jnp.bfloat16

EPS = 1e-6
N_MIXERS = 3
A_WINDOWS = (128, 512, 2048)
A_DILATIONS = (1, 4, 16)
A_BLOCK = 128
N_BUCKETS = 32
MAX_DISTANCE = 2048
GLA_GATE_NORMALIZER = 16.0
LA_BASE = 16
LA_CHUNK = 128
RIDER_ROWS = 16

TILES = dict(
    in_proj=dict(bm=2048, bn=512),
    mlp=dict(bm=1024, bf=512),
    mix_mem_rows=512,
    gate_rows=1024,
    attention_rows=16 * A_BLOCK,
    mem_kv=dict(bm=512, bn=512),
    window_copy_rows=512,
    window_relayout_rows=256,
)

V7X_VMEM_BYTES = 64 * 1024 * 1024
VMEM_LIMIT = V7X_VMEM_BYTES - 2 * 1024 * 1024
SUBLANES = 8
LANES = 128
V7X_MXU_WIDTH = 256


def _cparams(n_axes):
    return pltpu.CompilerParams(dimension_semantics=("arbitrary",) * n_axes,
                                vmem_limit_bytes=VMEM_LIMIT)


def _tile(n, target, align):
    if n <= target:
        return n
    t = (target // align) * align
    while n % t:
        t -= align
    return t


def _rms(x, g):
    r = lax.rsqrt(jnp.mean(x * x, axis=-1, keepdims=True) + EPS)
    return (x * r) * g


def _dot(a, b):
    return jnp.dot(a, b, preferred_element_type=F32)


def _dot_nt(a, b):
    return lax.dot_general(a, b, (((1,), (1,)), ((), ())), preferred_element_type=F32)


def _dot_tn(a, b):
    return lax.dot_general(a, b, (((0,), (0,)), ((), ())), preferred_element_type=F32)


def _bf_round(a):
    return a.astype(BF16).astype(F32)


def _linear_body(*refs, riders, transposed):
    if riders:
        x_ref, w_ref, g_ref, r_ref, o_ref, or_ref, xs_ref = refs
    else:
        x_ref, w_ref, g_ref, o_ref, xs_ref = refs
    bm = x_ref.shape[0]

    @pl.when(pl.program_id(1) == 0)
    def _():
        xs_ref[0:bm] = _rms(x_ref[...], g_ref[...]).astype(BF16)
        if riders:
            xs_ref[bm:bm + riders] = _rms(r_ref[...], g_ref[...]).astype(BF16)

    w = w_ref[...].astype(BF16)
    acc = _dot_nt(xs_ref[...], w) if transposed else _dot(xs_ref[...], w)
    if riders:
        o_ref[...] = acc[:bm]
        or_ref[...] = acc[bm:]
    else:
        o_ref[...] = acc


def linear(x, w, layer, *, n_out=None, gain, rider=None, transposed=False, bm, bn):
    M, K = x.shape
    N = (w.shape[1] if transposed else w.shape[2]) if n_out is None else n_out
    bm, bn = _tile(M, bm, SUBLANES), _tile(N, bn, LANES)
    riders = 0 if rider is None else rider.shape[0]
    g_arr, g_idx = gain
    w_spec = (pl.BlockSpec((None, bn, K), lambda i, j: (layer, j, 0)) if transposed
              else pl.BlockSpec((None, K, bn), lambda i, j: (layer, 0, j)))
    in_specs = [pl.BlockSpec((bm, K), lambda i, j: (i, 0)),
                w_spec,
                pl.BlockSpec((None, 1, K), lambda i, j: (g_idx, 0, 0))]
    args = [x, w, g_arr]
    out_specs = [pl.BlockSpec((bm, bn), lambda i, j: (i, j))]
    out_shape = [jax.ShapeDtypeStruct((M, N), F32)]
    if riders:
        in_specs.append(pl.BlockSpec((riders, K), lambda i, j: (0, 0)))
        args.append(rider)
        out_specs.append(pl.BlockSpec((None, riders, bn), lambda i, j: (i, 0, j)))
        out_shape.append(jax.ShapeDtypeStruct((M // bm, riders, N), F32))
    outs = pl.pallas_call(
        functools.partial(_linear_body, riders=riders, transposed=transposed),
        grid=(M // bm, N // bn),
        in_specs=in_specs,
        out_specs=out_specs,
        out_shape=out_shape,
        scratch_shapes=[pltpu.VMEM((bm + riders, K), BF16)],
        compiler_params=_cparams(2),
        name="linear",
    )(*args)
    return (outs[0], outs[1][0]) if riders else outs[0]


def _mlp_body(*refs, has_final, riders):
    refs = list(refs)
    x_ref, g_ref, wu_ref, wd_ref = refs[:4]
    pos = 4
    fg_ref = r_ref = or_ref = None
    if has_final:
        fg_ref = refs[pos]
        pos += 1
    if riders:
        r_ref = refs[pos]
        pos += 1
    o_ref = refs[pos]
    if riders:
        or_ref = refs[pos + 1]
    xs_ref = refs[-1]
    bm = x_ref.shape[0]
    f = pl.program_id(1)

    @pl.when(f == 0)
    def _():
        x = x_ref[...]
        xs_ref[0:bm] = _rms(x, g_ref[...]).astype(BF16)
        o_ref[...] = x
        if riders:
            r = r_ref[...]
            xs_ref[bm:bm + riders] = _rms(r, g_ref[...]).astype(BF16)
            or_ref[...] = r

    def up_down(xs):
        hdn = _dot(xs, wu_ref[...].astype(BF16))
        hdn = jnp.square(jnp.maximum(hdn, 0.0)).astype(BF16)
        return _dot(hdn, wd_ref[...].astype(BF16))

    if riders:
        @pl.when(pl.program_id(0) == 0)
        def _():
            down = up_down(xs_ref[...])
            o_ref[...] += down[:bm]
            or_ref[...] += down[bm:]

        @pl.when(pl.program_id(0) != 0)
        def _():
            o_ref[...] += up_down(xs_ref[0:bm])
    else:
        o_ref[...] += up_down(xs_ref[...])

    if has_final:
        @pl.when(f == pl.num_programs(1) - 1)
        def _():
            o_ref[...] = _rms(o_ref[...], fg_ref[...])
            if riders:
                or_ref[...] = _rms(or_ref[...], fg_ref[...])


def mlp(x, gains, w_up, w_down, layer, *, final_gain=None, rider=None, bm, bf):
    M, D = x.shape
    FF = w_up.shape[-1]
    bm, bf = _tile(M, bm, SUBLANES), _tile(FF, bf, LANES)
    riders = 0 if rider is None else rider.shape[0]
    in_specs = [pl.BlockSpec((bm, D), lambda i, f: (i, 0)),
                pl.BlockSpec((None, 1, D), lambda i, f: (layer, 0, 0)),
                pl.BlockSpec((None, D, bf), lambda i, f: (layer, 0, f)),
                pl.BlockSpec((None, bf, D), lambda i, f: (layer, f, 0))]
    args = [x, gains, w_up, w_down]
    if final_gain is not None:
        in_specs.append(pl.BlockSpec((1, D), lambda i, f: (0, 0)))
        args.append(final_gain)
    out_specs = [pl.BlockSpec((bm, D), lambda i, f: (i, 0))]
    out_shape = [jax.ShapeDtypeStruct((M, D), F32)]
    if riders:
        in_specs.append(pl.BlockSpec((riders, D), lambda i, f: (0, 0)))
        args.append(rider)
        out_specs.append(pl.BlockSpec((None, riders, D), lambda i, f: (i, 0, 0)))
        out_shape.append(jax.ShapeDtypeStruct((M // bm, riders, D), F32))
    outs = pl.pallas_call(
        functools.partial(_mlp_body, has_final=final_gain is not None, riders=riders),
        grid=(M // bm, FF // bf),
        in_specs=in_specs,
        out_specs=out_specs,
        out_shape=out_shape,
        scratch_shapes=[pltpu.VMEM((bm + riders, D), BF16)],
        compiler_params=_cparams(2),
        name="mlp",
    )(*args)
    return (outs[0], outs[1][0]) if riders else outs[0]


def _mem_body(y_ref, wm_ref, h_ref, g_ref, wq_ref, kv_ref, wo_ref, o_ref, *, heads, hd):
    x = h_ref[...] + _dot(y_ref[...].astype(BF16), wm_ref[...])
    xn = _rms(x, g_ref[...]).astype(BF16)
    q = _dot(xn, wq_ref[...])
    width = heads * hd
    scale = hd ** -0.5
    outs = []
    for h in range(heads):
        qh = q[:, h * hd:(h + 1) * hd].astype(BF16)
        kh = kv_ref[:, h * hd:(h + 1) * hd].astype(BF16)
        vh = kv_ref[:, width + h * hd:width + (h + 1) * hd].astype(BF16)
        s = _dot_nt(qh, kh) * scale
        p = jnp.exp(s - jnp.max(s, axis=-1, keepdims=True))
        p = p / jnp.sum(p, axis=-1, keepdims=True)
        outs.append(_dot(p.astype(BF16), vh))
    o = jnp.concatenate(outs, axis=-1).astype(BF16)
    o_ref[...] = x + _dot(o, wo_ref[...])


def mix_out_mem_attend(y, w_mix, mix_layer, h, gains, w_q, kv, w_o, layer, *, heads, hd, bm):
    B, T, D = h.shape
    Ky = y.shape[-1]
    Mt = kv.shape[2]
    width = heads * hd
    bm = _tile(T, bm, SUBLANES)
    return pl.pallas_call(
        functools.partial(_mem_body, heads=heads, hd=hd),
        grid=(B, T // bm),
        in_specs=[pl.BlockSpec((None, bm, Ky), lambda b, i: (b, i, 0)),
                  pl.BlockSpec((None, Ky, D), lambda b, i: (mix_layer, 0, 0), pipeline_mode=pl.Buffered(1)),
                  pl.BlockSpec((None, bm, D), lambda b, i: (b, i, 0)),
                  pl.BlockSpec((None, 1, D), lambda b, i: (layer, 0, 0)),
                  pl.BlockSpec((None, D, width), lambda b, i: (layer, 0, 0), pipeline_mode=pl.Buffered(1)),
                  pl.BlockSpec((None, None, Mt, 2 * width), lambda b, i: (layer, b, 0, 0)),
                  pl.BlockSpec((None, width, D), lambda b, i: (layer, 0, 0), pipeline_mode=pl.Buffered(1))],
        out_specs=pl.BlockSpec((None, bm, D), lambda b, i: (b, i, 0)),
        out_shape=jax.ShapeDtypeStruct((B, T, D), F32),
        compiler_params=_cparams(2),
        name="mix_out_mem_attend",
    )(y, w_mix, h, gains, w_q, kv, w_o)


def _mem_sample_body(y_ref, wm_ref, h_ref, g_ref, wq_ref, kv_ref, wo_ref, o_ref, *, heads, hd):
    batch = h_ref.shape[0]
    scale = hd ** -0.5
    x = h_ref[...] + _dot(y_ref[...].astype(BF16), wm_ref[...])
    q = _bf_round(_dot(_rms(x, g_ref[...]).astype(BF16), wq_ref[...]))
    outs = []
    for b in range(batch):
        qb = jnp.concatenate([q[b:b + 1, h * hd:(h + 1) * hd] for h in range(heads)], axis=0)
        k = _bf_round(kv_ref[b, :, 0])
        v = _bf_round(kv_ref[b, :, 1])
        s = jnp.sum(k * qb[None], axis=-1, keepdims=True) * scale
        p = jnp.exp(s - jnp.max(s, axis=0))
        p = _bf_round(p / jnp.sum(p, axis=0))
        ob = jnp.sum(p * v, axis=0)
        outs.append(jnp.concatenate([ob[h:h + 1] for h in range(heads)], axis=1))
    o = jnp.concatenate(outs, axis=0).astype(BF16)
    o_ref[...] = x + _dot(o, wo_ref[...])


def mix_out_mem_attend_sample(y, w_mix, mix_layer, h, gains, w_q, kv, w_o, layer, *, heads, hd):
    B, D = h.shape
    Ky = y.shape[-1]
    Mt = kv.shape[2]
    width = heads * hd
    one = pl.Buffered(1)
    return pl.pallas_call(
        functools.partial(_mem_sample_body, heads=heads, hd=hd),
        grid=(1,),
        in_specs=[pl.BlockSpec((B, Ky), lambda i: (0, 0)),
                  pl.BlockSpec((None, Ky, D), lambda i: (mix_layer, 0, 0), pipeline_mode=one),
                  pl.BlockSpec((B, D), lambda i: (0, 0)),
                  pl.BlockSpec((None, 1, D), lambda i: (layer, 0, 0)),
                  pl.BlockSpec((None, D, width), lambda i: (layer, 0, 0), pipeline_mode=one),
                  pl.BlockSpec((None, B, Mt, 2, heads, hd), lambda i: (layer, 0, 0, 0, 0, 0), pipeline_mode=one),
                  pl.BlockSpec((None, width, D), lambda i: (layer, 0, 0), pipeline_mode=one)],
        out_specs=pl.BlockSpec((B, D), lambda i: (0, 0)),
        out_shape=jax.ShapeDtypeStruct((B, D), F32),
        compiler_params=_cparams(1),
        name="mix_out_mem_attend_sample",
    )(y, w_mix, h, gains, w_q, kv, w_o)


def t5_bucket(dist):
    dist = np.asarray(dist, np.int64)
    max_exact = N_BUCKETS // 2
    d = np.maximum(dist, 1).astype(np.float64)
    large = max_exact + (np.log(d / max_exact) / math.log(MAX_DISTANCE / max_exact)
                         * (N_BUCKETS - max_exact)).astype(np.int64)
    return np.where(dist < max_exact, dist, np.minimum(large, N_BUCKETS - 1)).astype(np.int32)


def _attn_block(q, k2, v2, bias2, scale):
    blk = q.shape[0]
    st = _dot_nt(k2.astype(BF16), q.astype(BF16)) * scale + bias2
    m = jnp.max(st, axis=0, keepdims=True)
    p = jnp.exp(st - m)
    l = jnp.sum(p, axis=0, keepdims=True)
    o = _dot_tn((p * (1.0 / l)).astype(BF16), v2.astype(BF16))
    return o, jnp.broadcast_to(m + jnp.log(l), (blk, blk)).T


def _dil_attn_body(*refs, dils, tile, hd):
    n_groups = len(dils)
    bias_ref, o_ref, og_ref, lse_ref = refs[5 * n_groups:]
    not_first_tile = pl.program_id(1) > 0
    blk = A_BLOCK
    scale = hd ** -0.5

    for g, d in enumerate(dils):
        q_ref, kc_ref, vc_ref, kp_ref, vp_ref = refs[5 * g:5 * g + 5]
        stream_rows = blk * d
        n_blocks = tile // stream_rows

        def rows(start):
            return pl.ds(start, blk) if d == 1 else pl.ds(start, blk, stride=d)

        bias2 = bias_ref[g]
        bias2_first = jnp.concatenate(
            [jnp.where(not_first_tile, bias2[:blk], -jnp.inf), bias2[blk:]], axis=0)

        for m in range(n_blocks):
            for r in range(d):
                s = r + m * stream_rows
                if m == 0:
                    kp, vp, bias = kp_ref[rows(s), :], vp_ref[rows(s), :], bias2_first
                else:
                    kp, vp, bias = kc_ref[rows(s - stream_rows), :], vc_ref[rows(s - stream_rows), :], bias2
                k2 = jnp.concatenate([kp, kc_ref[rows(s), :]], axis=0)
                v2 = jnp.concatenate([vp, vc_ref[rows(s), :]], axis=0)
                o, lse = _attn_block(q_ref[rows(s), :], k2, v2, bias, scale)
                og_ref[g, rows(s), :] = o
                lse_ref[g, rows(s), :] = lse

    lses = [lse_ref[g] for g in range(n_groups)]
    mx = functools.reduce(jnp.maximum, lses)
    es = [jnp.exp(l - mx) for l in lses]
    den = functools.reduce(lambda a, b: a + b, es)
    acc = es[0] * og_ref[0]
    for g in range(1, n_groups):
        acc = acc + es[g] * og_ref[g]
    o_ref[...] = acc / den


def dilated_attention_prompt(qkv, bias, *, heads, hd, tile):
    B, T, _ = qkv.shape
    dils = A_DILATIONS
    n_groups = len(dils)
    tile = min(tile, T)
    assert hd == LANES and T % tile == 0 and all(tile % (A_BLOCK * d) == 0 for d in dils)
    sec = n_groups * heads

    def cur(which, g):
        return pl.BlockSpec((None, tile, hd), lambda b, n, h: (b, n, which * sec + g * heads + h))

    def prev(which, g):
        rows = A_BLOCK * dils[g]
        per = tile // rows
        return pl.BlockSpec((None, rows, hd),
                            lambda b, n, h: (b, jnp.maximum(n * per - 1, 0), which * sec + g * heads + h))

    in_specs = []
    for g in range(n_groups):
        in_specs += [cur(0, g), cur(1, g), cur(2, g), prev(1, g), prev(2, g)]
    in_specs.append(pl.BlockSpec((n_groups, None, 2 * A_BLOCK, A_BLOCK), lambda b, n, h: (0, h, 0, 0)))
    return pl.pallas_call(
        functools.partial(_dil_attn_body, dils=dils, tile=tile, hd=hd),
        grid=(B, T // tile, heads),
        in_specs=in_specs,
        out_specs=pl.BlockSpec((None, tile, hd), lambda b, n, h: (b, n, h)),
        out_shape=jax.ShapeDtypeStruct((B, T, heads * hd), F32),
        scratch_shapes=[pltpu.VMEM((n_groups, tile, hd), F32), pltpu.VMEM((n_groups, tile, hd), F32)],
        compiler_params=_cparams(3),
        name="dilated_attention_prompt",
    )(*([qkv] * (5 * n_groups)), bias)


def _dec_attn_body(qkv_ref, *refs, groups, span, scale):
    bufs, bias_ref, o_ref = refs[:groups], refs[groups], refs[groups + 1]
    outs, lses = [], []
    for g in range(groups):
        q = _bf_round(qkv_ref[g])
        kn = _bf_round(qkv_ref[groups + g])
        vn = _bf_round(qkv_ref[2 * groups + g])
        k = _bf_round(bufs[g][:, 0])
        v = _bf_round(bufs[g][:, 1])
        s = jnp.sum(k * q[None], axis=-1, keepdims=True) * scale + bias_ref[g, :span]
        s_self = jnp.sum(q * kn, axis=-1, keepdims=True) * scale + bias_ref[g, span]
        m = jnp.maximum(jnp.max(s, axis=0), s_self)
        p = jnp.exp(s - m[None])
        p_self = jnp.exp(s_self - m)
        l = jnp.sum(p, axis=0) + p_self
        outs.append(jnp.sum(_bf_round(p / l[None]) * v, axis=0) + _bf_round(p_self / l) * vn)
        lses.append(m + jnp.log(l))
    mx = functools.reduce(jnp.maximum, lses)
    es = [jnp.exp(l - mx) for l in lses]
    den = functools.reduce(lambda a, b: a + b, es)
    acc = (es[0] / den) * outs[0]
    for g in range(1, groups):
        acc = acc + (es[g] / den) * outs[g]
    o_ref[...] = acc


def dilated_attention_sample(qkv, caches, layer, bias, *, heads, hd, span):
    B = qkv.shape[0]
    groups = len(caches)
    in_specs = [pl.BlockSpec((None, 3 * groups, heads, hd), lambda b: (b, 0, 0, 0))]
    views = []
    for g, c in enumerate(caches):
        d = A_DILATIONS[g]
        assert c.shape[2] == span * d
        views.append(c.reshape(c.shape[0], B, span, d, 2, heads, hd))
        in_specs.append(pl.BlockSpec((None, None, span, None, 2, heads, hd),
                                     lambda b: (layer, b, 0, 0, 0, 0, 0)))
    in_specs.append(pl.BlockSpec((groups, span + 1, heads, hd), lambda b: (0, 0, 0, 0)))
    return pl.pallas_call(
        functools.partial(_dec_attn_body, groups=groups, span=span, scale=hd ** -0.5),
        grid=(B,),
        in_specs=in_specs,
        out_specs=pl.BlockSpec((None, heads, hd), lambda b: (b, 0, 0)),
        out_shape=jax.ShapeDtypeStruct((B, heads, hd), F32),
        compiler_params=_cparams(1),
        name="dilated_attention_sample",
    )(qkv, *views, bias)


def _to_window_body(*refs, heads, hd):
    o_ref = refs[-1]
    chunk = refs[0].shape[0]
    for layer in range(o_ref.shape[0]):
        for s in range(2):
            src = refs[2 * layer + s]
            for h in range(heads):
                o_ref[layer, pl.ds(s * heads + h, chunk, stride=2 * heads), :] = src[:, h * hd:(h + 1) * hd]


def to_window(qkvs, g, *, rows, groups, heads, hd):
    B, T, _ = qkvs[0].shape
    W = heads * hd
    chunk = _tile(rows, TILES["window_relayout_rows"], SUBLANES)
    assert (T - rows) % chunk == 0
    first = (T - rows) // chunk
    in_specs, args = [], []
    for qkv in qkvs:
        in_specs += [pl.BlockSpec((None, chunk, W), lambda b, c: (b, first + c, groups + g)),
                     pl.BlockSpec((None, chunk, W), lambda b, c: (b, first + c, 2 * groups + g))]
        args += [qkv, qkv]
    per_token = 2 * heads
    out = pl.pallas_call(
        functools.partial(_to_window_body, heads=heads, hd=hd),
        grid=(B, rows // chunk),
        in_specs=in_specs,
        out_specs=pl.BlockSpec((len(qkvs), None, chunk * per_token, hd), lambda b, c: (0, b, c, 0)),
        out_shape=jax.ShapeDtypeStruct((len(qkvs), B, rows * per_token, hd), F32),
        compiler_params=_cparams(2),
        name="to_window",
    )(*args)
    return out.reshape(len(qkvs), B, rows, 2, heads, hd)


def _window_update_body(main_ref, next_ref, new_ref, o_ref):
    rows = main_ref.shape[0]
    last_chunk = pl.program_id(2) == pl.num_programs(2) - 1
    o_ref[0:rows - 1] = main_ref[1:rows]
    o_ref[rows - 1:rows] = jnp.where(last_chunk, new_ref[...], next_ref[...])


def window_update(cache, new, *, chunk):
    n_layers, B, rows = cache.shape[:3]
    tail = cache.shape[3:]
    chunk = _tile(rows, chunk, 1)
    zeros = (0,) * len(tail)
    return pl.pallas_call(
        _window_update_body,
        grid=(n_layers, B, rows // chunk),
        in_specs=[pl.BlockSpec((None, None, chunk) + tail, lambda j, b, c: (j, b, c) + zeros),
                  pl.BlockSpec((None, None, 1) + tail,
                               lambda j, b, c: (j, b, jnp.minimum((c + 1) * chunk, rows - 1)) + zeros),
                  pl.BlockSpec((None, None, 1) + tail, lambda j, b, c: (j, b, 0) + zeros)],
        out_specs=pl.BlockSpec((None, None, chunk) + tail, lambda j, b, c: (j, b, c) + zeros),
        out_shape=jax.ShapeDtypeStruct(cache.shape, cache.dtype),
        compiler_params=_cparams(3),
        name="window_update",
    )(cache, cache, new)


def _gla_gate_body(x_ref, g_ref, wl_ref, wg_ref, b_ref, o_ref):
    xn = _rms(x_ref[...], g_ref[...]).astype(BF16)
    low = _dot(xn, wl_ref[...].astype(BF16))
    z = _dot(low.astype(BF16), wg_ref[...].astype(BF16)) + b_ref[...]
    log_sig = jnp.minimum(z, 0.0) - jnp.log(1.0 + jnp.exp(-jnp.abs(z)))
    o_ref[...] = log_sig / GLA_GATE_NORMALIZER


def gla_gate(x, gain, w_low, w_gate, b_gate, layer, *, bm):
    M, D = x.shape
    g_arr, g_idx = gain
    rank, dk = w_gate.shape[1], w_gate.shape[2]
    bm = _tile(M, bm, SUBLANES)
    return pl.pallas_call(
        _gla_gate_body,
        grid=(M // bm,),
        in_specs=[pl.BlockSpec((bm, D), lambda i: (i, 0)),
                  pl.BlockSpec((None, 1, D), lambda i: (g_idx, 0, 0)),
                  pl.BlockSpec((D, rank), lambda i: (0, 0)),
                  pl.BlockSpec((None, rank, dk), lambda i: (layer, 0, 0)),
                  pl.BlockSpec((None, 1, dk), lambda i: (layer, 0, 0))],
        out_specs=pl.BlockSpec((bm, dk), lambda i: (i, 0)),
        out_shape=jax.ShapeDtypeStruct((M, dk), F32),
        compiler_params=_cparams(1),
        name="gla_gate",
    )(x, g_arr, w_low, w_gate, b_gate)


def _cumsum_rows(x):
    rows = x.shape[0]
    row = lax.broadcasted_iota(jnp.int32, x.shape, 0)
    s = 1
    while s < rows:
        x = x + jnp.where(row >= s, pltpu.roll(x, s, axis=0), 0.0)
        s *= 2
    return x


def _block_refs(bcs, m, pairs):
    rows, dk = bcs.shape
    nb = rows // m
    ends = [bcs[(k + 1) * m - 1:(k + 1) * m, :] for k in range(nb)]
    parts = []
    for k in range(nb):
        if pairs:
            r = ends[k - 1] if k % 2 == 1 else ends[k]
        else:
            r = ends[k - 1] if k > 0 else jnp.zeros((1, dk), F32)
        parts.append(jnp.broadcast_to(r, (m, dk)))
    return parts[0] if nb == 1 else jnp.concatenate(parts, axis=0)


def _la_body(a_ref, b_ref, c_ref, gate_ref, e_ref, gn_ref, s0_ref, y_ref, so_ref, st_ref,
             *, hgrn2, heads, dk, dv, chunk, base, valid, q_scale, gsz):
    n = pl.program_id(1)
    groups = heads // gsz

    def own_block(shape, row_width, col_width):
        r = lax.broadcasted_iota(jnp.int32, shape, 0) // row_width
        c = lax.broadcasted_iota(jnp.int32, shape, 1) // col_width
        return r == c

    def block_rows(x, width):
        if gsz == 1:
            return x
        lane_head = lax.broadcasted_iota(jnp.int32, x.shape, 1) // width
        return jnp.concatenate([jnp.where(lane_head == i, x, jnp.zeros_like(x)) for i in range(gsz)], axis=0)

    @pl.when(n == 0)
    def _():
        for gi in range(groups):
            rows = []
            for i in range(gsz):
                blocks = [s0_ref[gi * gsz + i].T if j == i else jnp.zeros((dv, dk), F32) for j in range(gsz)]
                rows.append(blocks[0] if gsz == 1 else jnp.concatenate(blocks, axis=1))
            st_ref[gi] = rows[0] if gsz == 1 else jnp.concatenate(rows, axis=0)

    tq = lax.broadcasted_iota(jnp.int32, (chunk, chunk), 0)
    ts = lax.broadcasted_iota(jnp.int32, (chunk, chunk), 1)
    row = lax.broadcasted_iota(jnp.int32, (chunk, 1), 0)
    sh = int(math.log2(base))
    mask_diag = jnp.logical_and((tq >> sh) == (ts >> sh), ts <= tq)
    levels = []
    m = base
    while m < chunk:
        s1 = int(math.log2(m))
        mk = jnp.logical_and((tq >> (s1 + 1)) == (ts >> (s1 + 1)),
                             jnp.logical_and(((tq >> s1) & 1) == 1, ((ts >> s1) & 1) == 0))
        levels.append((m, mk))
        m *= 2
    if gsz > 1:
        mask_diag = jnp.concatenate([mask_diag] * gsz, axis=1)
        levels = [(m, jnp.concatenate([mk] * gsz, axis=1)) for m, mk in levels]
        state_own = own_block((gsz * dv, gsz * dk), dv, dk)
    gn = gn_ref[...]

    def one_group(gi):
        ksl = slice(gi * gsz * dk, (gi + 1) * gsz * dk)
        vsl = slice(gi * gsz * dv, (gi + 1) * gsz * dv)
        if hgrn2:
            a = a_ref[:, ksl]
            lb = e_ref[:, ksl]
            f = lb + (1.0 - lb) * jax.nn.sigmoid(b_ref[:, ksl])
            qa = a * jax.nn.sigmoid(a)
            kk = 1.0 - f
            g = jnp.log(f)
        else:
            qa = a_ref[:, ksl] * q_scale
            kk = b_ref[:, ksl]
            g = e_ref[:, ksl]
        v = c_ref[:, vsl]
        if valid < chunk:
            kk = jnp.where(row < valid, kk, 0.0)
            g = jnp.where(row < valid, g, 0.0)
        bcs = _cumsum_rows(g)
        vb = v.astype(BF16)
        st = st_ref[gi]

        o = _dot_nt((qa * jnp.exp(bcs)).astype(BF16), st.astype(BF16))

        ref_d = _block_refs(bcs, base, False)
        a_mat = _dot_nt((qa * jnp.exp(bcs - ref_d)).astype(BF16),
                        block_rows((kk * jnp.exp(ref_d - bcs)).astype(BF16), dk))
        a_mat = jnp.where(mask_diag, a_mat, 0.0)
        for (m, mk) in levels:
            dec = jnp.exp(-jnp.abs(bcs - _block_refs(bcs, m, True)))
            a_mat = a_mat + jnp.where(mk, _dot_nt((qa * dec).astype(BF16), block_rows((kk * dec).astype(BF16), dk)),
                                      0.0)
        o = o + _dot(a_mat.astype(BF16), block_rows(vb, dv))

        b_last = bcs[chunk - 1:chunk, :]
        kd = (kk * jnp.exp(b_last - bcs)).astype(BF16)
        update = _dot_tn(vb, kd)
        if gsz > 1:
            update = jnp.where(state_own, update, 0.0)
        st_new = jnp.exp(b_last) * st + update

        gate = gate_ref[:, vsl]
        normed = [_rms(o[:, i * dv:(i + 1) * dv], gn) for i in range(gsz)]
        normed = normed[0] if gsz == 1 else jnp.concatenate(normed, axis=1)
        return st_new, normed * (gate * jax.nn.sigmoid(gate))

    for gi in range(groups):
        st_ref[gi], y_ref[:, gi * gsz * dv:(gi + 1) * gsz * dv] = one_group(gi)

    @pl.when(n == pl.num_programs(1) - 1)
    def _():
        for h in range(heads):
            gi, i = divmod(h, gsz)
            so_ref[h] = st_ref[gi][i * dv:(i + 1) * dv, i * dk:(i + 1) * dk].T


def gated_linear_attention(proj, extra, gnorm, s0, *, hgrn2, chunk, valid=None):
    B, T, _ = proj.shape
    _, H, dk, dv = s0.shape
    chunk = min(chunk, T)
    assert T % chunk == 0
    base = min(LA_BASE, chunk)
    valid = chunk if valid is None else valid
    wk, wv = H * dk, H * dv
    gsz = max(1, V7X_MXU_WIDTH // max(dk, dv))
    if H % gsz or chunk < LANES:
        gsz = 1

    def col(width, start):
        assert start % width == 0
        return pl.BlockSpec((None, chunk, width), lambda b, n: (b, n, start // width))

    if hgrn2:
        specs = [col(wk, 0), col(wk, wk), col(wv, 2 * wk), col(wv, 2 * wk + wv),
                 pl.BlockSpec((1, wk), lambda b, n: (0, 0))]
    else:
        specs = [col(wk, 0), col(wk, wk), col(wv, 2 * wk), col(wv, 2 * wk + wv),
                 pl.BlockSpec((None, chunk, wk), lambda b, n: (b, n, 0))]
    specs += [pl.BlockSpec((1, dv), lambda b, n: (0, 0)),
              pl.BlockSpec((None, H, dk, dv), lambda b, n: (b, 0, 0, 0))]
    return pl.pallas_call(
        functools.partial(_la_body, hgrn2=hgrn2, heads=H, dk=dk, dv=dv, chunk=chunk, base=base,
                          valid=valid, q_scale=1.0 if hgrn2 else dk ** -0.5, gsz=gsz),
        grid=(B, T // chunk),
        in_specs=specs,
        out_specs=[pl.BlockSpec((None, chunk, wv), lambda b, n: (b, n, 0)),
                   pl.BlockSpec((None, H, dk, dv), lambda b, n: (b, 0, 0, 0))],
        out_shape=[jax.ShapeDtypeStruct((B, T, wv), F32),
                   jax.ShapeDtypeStruct((B, H, dk, dv), F32)],
        scratch_shapes=[pltpu.VMEM((H // gsz, gsz * dv, gsz * dk), F32)],
        compiler_params=_cparams(2),
        name="gated_linear_attention",
    )(proj, proj, proj, proj, extra, gnorm, s0)


def _pad_rows(x, rows):
    return jnp.pad(x, ((0, 0), (0, rows - x.shape[1]), (0, 0)))


def _trunk(xp, xs, p_mem, s_mem, caches, b_states, c_states, P):
    Bp, T, D = xp.shape
    Bs = xs.shape[0]
    assert xs.shape[1] == 1 and Bs <= RIDER_ROWS
    Mp = Bp * T
    depth = P["norm_mix"].shape[0]
    hd_a, hpg = P["a_hd"], P["a_hpg"]
    groups = len(A_DILATIONS)
    Wa = hpg * hd_a
    tile, mlp_tile = TILES["in_proj"], TILES["mlp"]

    def ride(h):
        return jnp.pad(h, ((0, RIDER_ROWS - Bs), (0, 0)))

    hp, hs = xp.reshape(Mp, D), xs.reshape(Bs, D)
    qkv_prompt, new_rows, p_b, s_b, p_c, s_c = [], [], [], [], [], []
    for li in range(depth):
        kind, j = li % N_MIXERS, li // N_MIXERS
        gmix = (P["norm_mix"], li)
        if kind == 0:
            qkv_p, qkv_s = linear(hp, P["a_w_in"], j, gain=gmix, rider=ride(hs), **tile)
            qkv3 = qkv_p.reshape(Bp, T, 3 * groups * Wa)
            yp = dilated_attention_prompt(qkv3, P["a_bias_prompt"], heads=hpg, hd=hd_a,
                                          tile=TILES["attention_rows"])
            qkv_prompt.append(qkv3)
            qkv5 = qkv_s[:Bs].reshape(Bs, 3, groups, hpg, hd_a)
            ys = dilated_attention_sample(qkv5.reshape(Bs, 3 * groups, hpg, hd_a), caches, j,
                                          P["a_bias_sample"], heads=hpg, hd=hd_a, span=A_BLOCK)
            new_rows.append([qkv5[:, 1:3, g][:, None] for g in range(groups)])
            ys = ys.reshape(Bs, Wa)
            w_mix = P["a_w_out"]
        elif kind == 1:
            proj_p, proj_s = linear(hp, P["b_w_in"], j, gain=gmix, rider=ride(hs), **tile)
            lb = P["lower_bounds"][li][None, :]
            gn = P["b_gnorm"][j][None, :]
            yp, S = gated_linear_attention(proj_p.reshape(Bp, T, -1), lb, gn,
                                           jnp.zeros((Bp,) + b_states.shape[2:], F32), hgrn2=True, chunk=LA_CHUNK)
            p_b.append(S)
            ys, S = gated_linear_attention(_pad_rows(proj_s[:Bs, None], SUBLANES), lb, gn, b_states[j],
                                           hgrn2=True, chunk=SUBLANES, valid=1)
            s_b.append(S)
            w_mix = P["b_w_out"]
        else:
            wk = P["c_wk"]
            n_main = P["c_n_main"]
            proj_p, proj_s = linear(hp, P["c_w_in_t"], j, n_out=n_main, gain=gmix, rider=ride(hs),
                                    transposed=True, **tile)
            gate_args = (gmix, P["c_w_low"][j], P["c_w_gate"], P["c_b_gate"], j)
            gn = P["c_gnorm"][j][None, :]
            yp, S = gated_linear_attention(proj_p.reshape(Bp, T, n_main),
                                           gla_gate(hp, *gate_args, bm=TILES["gate_rows"]).reshape(Bp, T, wk), gn,
                                           jnp.zeros((Bp,) + c_states.shape[2:], F32), hgrn2=False, chunk=LA_CHUNK)
            p_c.append(S)
            ys, S = gated_linear_attention(_pad_rows(proj_s[:Bs, None], SUBLANES),
                                           _pad_rows(gla_gate(hs, *gate_args, bm=SUBLANES)[:, None], SUBLANES), gn,
                                           c_states[j], hgrn2=False, chunk=SUBLANES, valid=1)
            s_c.append(S)
            w_mix = P["c_w_out"]

        mem_kw = dict(heads=P["m_heads"], hd=P["m_hd"])
        hp = mix_out_mem_attend(yp.reshape(Bp, T, -1), w_mix, j, hp.reshape(Bp, T, D), P["norm_mem"], P["mem_w_q"],
                                p_mem, P["mem_w_out"], li, bm=TILES["mix_mem_rows"], **mem_kw).reshape(Mp, D)
        if ys.ndim == 3:
            ys = ys[:, 0]
        hs = mix_out_mem_attend_sample(ys, w_mix, j, hs, P["norm_mem"], P["mem_w_q"], s_mem, P["mem_w_out"], li,
                                       **mem_kw)
        final = P["norm_final"] if li == depth - 1 else None
        hp, hs = mlp(hp, P["norm_mlp"], P["mlp_w_up"], P["mlp_w_down"], li, final_gain=final,
                     rider=ride(hs), **mlp_tile)
        hs = hs[:Bs]

    new_rows = [jnp.stack([rows[g] for rows in new_rows], axis=0) for g in range(groups)]
    p_win = [to_window(qkv_prompt, g, rows=min(A_WINDOWS[g], T), groups=groups, heads=hpg, hd=hd_a)
             for g in range(groups)]
    return (hp.reshape(Bp, T, D), hs.reshape(Bs, 1, D), p_win, new_rows,
            jnp.stack(p_b, axis=0), jnp.stack(s_b, axis=0), jnp.stack(p_c, axis=0), jnp.stack(s_c, axis=0))


def _bias_tables(rel_bias_table, hpg, span):
    steps = np.arange(A_BLOCK)[None, :] + A_BLOCK - np.arange(2 * A_BLOCK)[:, None]
    eye = np.eye(N_BUCKETS, dtype=np.float32)
    prompt, sample = [], []
    for g, d in enumerate(A_DILATIONS):
        tg_t = rel_bias_table[:, g * hpg:(g + 1) * hpg].T
        onehot = eye[:, t5_bucket(np.clip(steps, 0, span) * d).reshape(-1)]
        bias = jnp.dot(tg_t, jnp.asarray(onehot), precision=lax.Precision.HIGHEST).reshape(hpg, 2 * A_BLOCK, A_BLOCK)
        band = jnp.asarray((steps >= 0) & (steps <= span))
        prompt.append(jnp.where(band[None], bias, -jnp.inf))
        dist = np.concatenate([(span - np.arange(span)) * d, np.zeros(1, np.int64)])
        row = jnp.dot(tg_t, jnp.asarray(eye[:, t5_bucket(dist)]), precision=lax.Precision.HIGHEST)
        sample.append(jnp.broadcast_to(row.T[:, :, None], (span + 1, hpg, LANES)))
    return jnp.stack(prompt, axis=0), jnp.stack(sample, axis=0)


def kernel(x_prompt, x_sample, cache_win0, cache_win1, cache_win2, state_hgrn, state_gla, cache_mem, mem_prompt, rel_bias_table, a_w_in, a_w_out, b_w_in, b_w_out, b_lower_bound, b_gnorm, c_w_in, c_w_gate, c_b_gate, c_w_out, c_gnorm, mem_norm, mem_w_kv, mem_w_q, mem_w_out, norm_mix, norm_mem, norm_mlp, mlp_w_up, mlp_w_down, norm_final):
    depth, D = norm_mix.shape
    hpg, hd_a = cache_win0.shape[-2], cache_win0.shape[-1]
    span = A_BLOCK
    assert all(w // d == span for w, d in zip(A_WINDOWS, A_DILATIONS))
    m_heads, m_hd = cache_mem.shape[-2], cache_mem.shape[-1]
    c_heads, c_dk, c_dv = state_gla.shape[2:]
    wk = c_heads * c_dk
    n_main = 2 * wk + 2 * c_heads * c_dv
    bias_prompt, bias_sample = _bias_tables(rel_bias_table, hpg, span)

    sm = jax.nn.softmax(b_lower_bound.astype(F32), axis=0)
    P = dict(
        a_hd=hd_a, a_hpg=hpg, m_heads=m_heads, m_hd=m_hd,
        a_bias_prompt=bias_prompt, a_bias_sample=bias_sample,
        a_w_in=a_w_in, a_w_out=a_w_out.astype(BF16), b_w_in=b_w_in, b_w_out=b_w_out.astype(BF16),
        lower_bounds=jnp.cumsum(sm, axis=0) - sm, b_gnorm=b_gnorm,
        c_w_in_t=jnp.swapaxes(c_w_in, 1, 2), c_w_low=c_w_in[:, :, n_main:], c_w_gate=c_w_gate,
        c_b_gate=c_b_gate[:, None, :], c_w_out=c_w_out.astype(BF16), c_gnorm=c_gnorm,
        c_wk=wk, c_n_main=n_main,
        mem_w_q=mem_w_q.astype(BF16), mem_w_out=mem_w_out.astype(BF16),
        norm_mix=norm_mix[:, None, :], norm_mem=norm_mem[:, None, :], norm_mlp=norm_mlp[:, None, :],
        mlp_w_up=mlp_w_up, mlp_w_down=mlp_w_down, norm_final=norm_final[None, :],
    )

    Bm, Mt, _ = mem_prompt.shape
    mem_flat = mem_prompt.reshape(Bm * Mt, D)
    mem_gain = mem_norm[:, None, :]
    p_mem = jnp.stack([linear(mem_flat, mem_w_kv, li, gain=(mem_gain, li), **TILES["mem_kv"])
                       for li in range(depth)], axis=0).reshape(depth, Bm, Mt, 2 * m_heads * m_hd)

    caches = (cache_win0, cache_win1, cache_win2)
    y_prompt, y_sample, p_win, new_rows, p_hgrn, s_hgrn, p_gla, s_gla = _trunk(
        x_prompt, x_sample, p_mem, cache_mem, caches, state_hgrn, state_gla, P)
    s_win = [window_update(c, n, chunk=TILES["window_copy_rows"]) for c, n in zip(caches, new_rows)]
    return (y_prompt, y_sample, p_win[0], p_win[1], p_win[2], p_hgrn, p_gla,
            p_mem.reshape(depth, Bm, Mt, 2, m_heads, m_hd),
            s_win[0], s_win[1], s_win[2], s_hgrn, s_gla)
```
